```python
import jax, jax.numpy as jnp
from jax import lax
import numpy as np

D_MODEL = 1024
BATCH = 8
SEQ = 4096
DEPTH = 4

D_MIX = D_MODEL
HEAD_DIM = 64
D_ATTN = D_MIX // 2
N_HEADS = D_ATTN // HEAD_DIM
D_CONV = D_MIX - D_ATTN
D_IN = 3 * D_ATTN + 2 * D_CONV
CONV_WIDTH = 31
DILATED_BRANCHES = ((128, 1), (512, 4), (2048, 16))
ATTN_BLOCK = 128
N_GROUPS = 4
EXPERTS_PER_GROUP = 8
N_EXPERTS = N_GROUPS * EXPERTS_PER_GROUP
TOP_K_INNER = 2
D_EXPERT = D_MODEL // 2
MOE_BLOCK = 256
NORM_EPS = 1e-6
N_MOD = 6

kernel_name = "hybrid_dilated_attn_conformer_hmoe"


def rms_norm(x, g):
    x32 = x.astype(jnp.float32)
    y = x32 * lax.rsqrt(jnp.mean(x32 * x32, axis=-1, keepdims=True) + NORM_EPS)
    return (y * g.astype(jnp.float32)).astype(x.dtype)


def layer_norm(x, g, b):
    x32 = x.astype(jnp.float32)
    mu = jnp.mean(x32, axis=-1, keepdims=True)
    var = jnp.mean(jnp.square(x32 - mu), axis=-1, keepdims=True)
    y = (x32 - mu) * lax.rsqrt(var + NORM_EPS)
    return (y * g.astype(jnp.float32) + b.astype(jnp.float32)).astype(x.dtype)


def alibi_slopes(n_heads):
    return jnp.asarray(2.0 ** (-8.0 * np.arange(1, n_heads + 1) / n_heads), dtype=jnp.float32)


def dilated_branch(q, k, v, slopes, window, dil):
    B, H, S, Dh = q.shape
    L = S // dil
    n_back = window // dil
    nb = -(-L // ATTN_BLOCK)
    Lp = nb * ATTN_BLOCK

    def to_blocks(t):
        t = t.reshape(B, H, L, dil, Dh).transpose(0, 1, 3, 2, 4)
        t = jnp.pad(t, ((0, 0), (0, 0), (0, 0), (0, Lp - L), (0, 0)))
        return t.reshape(B, H, dil, nb, ATTN_BLOCK, Dh)

    def with_prev(t):
        prev = jnp.pad(t[:, :, :, :-1], ((0, 0), (0, 0), (0, 0), (1, 0), (0, 0), (0, 0)))
        return jnp.concatenate([prev, t], axis=4)

    qb = to_blocks(q)
    kc = with_prev(to_blocks(k))
    vc = with_prev(to_blocks(v))
    s = jnp.einsum('bhrnqd,bhrnkd->bhrnqk', qb, kc,
                   preferred_element_type=jnp.float32) * (HEAD_DIM ** -0.5)
    qi = np.arange(ATTN_BLOCK)[:, None] + ATTN_BLOCK
    ki = np.arange(2 * ATTN_BLOCK)[None, :]
    delta = qi - ki
    blk_ok = (np.arange(nb)[:, None, None] > 0) | (ki[None] >= ATTN_BLOCK)
    valid = (delta >= 0) & (delta <= n_back) & blk_ok
    bias = -slopes[:, None, None] * jnp.asarray(delta * dil, dtype=jnp.float32)
    s = s + bias[None, :, None, None]
    s = jnp.where(valid[None, None, None], s, -jnp.inf)
    m = jnp.max(s, axis=-1)
    p = jnp.exp(s - m[..., None])
    l = jnp.sum(p, axis=-1)
    o = jnp.einsum('bhrnqk,bhrnkd->bhrnqd', p, vc.astype(jnp.float32))

    def from_blocks(t, trailing):
        t = t.reshape((B, H, dil, Lp) + trailing)[:, :, :, :L]
        t = jnp.moveaxis(t, 2, 3)
        return t.reshape((B, H, S) + trailing)

    return from_blocks(o, (Dh,)), from_blocks(m, ()), from_blocks(l, ())


def dilated_attention(q, k, v, slopes):
    outs = [dilated_branch(q, k, v, slopes, w, d) for (w, d) in DILATED_BRANCHES]
    m_all = jnp.stack([m for (_, m, _) in outs])
    m_max = jnp.max(m_all, axis=0)
    num = 0.0
    den = 0.0
    for (o, m, l) in outs:
        a = jnp.exp(m - m_max)
        num = num + a[..., None] * o
        den = den + a * l
    return num / den[..., None]


def conformer_conv(a, b, conv_w, conv_b, ln_g, ln_b):
    u = a * jax.nn.sigmoid(b)
    y = lax.conv_general_dilated(u, conv_w[:, None, :].astype(u.dtype), window_strides=(1,),
                                 padding=[(CONV_WIDTH - 1, 0)],
                                 dimension_numbers=('NWC', 'WIO', 'NWC'),
                                 feature_group_count=D_CONV) + conv_b
    return jax.nn.silu(layer_norm(y, ln_g, ln_b))


def hier_moe(h, w_rg, b_rg, w_re, b_re, w_gate, w_up, w_down):
    B, S, D = h.shape
    T = B * S
    hf = h.reshape(T, D)
    tix = jnp.arange(T)
    g_logits = (hf @ w_rg + b_rg).astype(jnp.float32)
    g_prob = jax.nn.softmax(g_logits, axis=-1)
    grp = jnp.argmax(g_logits, axis=-1)
    p_grp = g_prob[tix, grp]
    e_logits = (hf @ w_re + b_re).astype(jnp.float32).reshape(T, N_GROUPS, EXPERTS_PER_GROUP)
    e_logits = e_logits[tix, grp]
    top_val, top_idx = lax.top_k(e_logits, TOP_K_INNER)
    top_w = jax.nn.softmax(top_val, axis=-1) * p_grp[:, None]

    A = T * TOP_K_INNER
    eid = (grp[:, None] * EXPERTS_PER_GROUP + top_idx).reshape(A).astype(jnp.int32)
    tok = jnp.repeat(tix, TOP_K_INNER).astype(jnp.int32)
    wts = top_w.reshape(A)
    order = jnp.argsort(eid)
    s_eid, s_tok, s_w = eid[order], tok[order], wts[order]
    counts = jnp.bincount(eid, length=N_EXPERTS)
    start = jnp.cumsum(counts) - counts
    padded = ((counts + MOE_BLOCK - 1) // MOE_BLOCK) * MOE_BLOCK
    pad_end = jnp.cumsum(padded)
    pad_start = pad_end - padded
    dest = pad_start[s_eid] + (jnp.arange(A) - start[s_eid])
    P = (-(-A // MOE_BLOCK)) * MOE_BLOCK + N_EXPERTS * MOE_BLOCK
    nblk = P // MOE_BLOCK
    buf_tok = jnp.full((P,), T, dtype=jnp.int32).at[dest].set(s_tok)
    buf_w = jnp.zeros((P,), dtype=jnp.float32).at[dest].set(s_w)
    blk_eid = jnp.clip(jnp.searchsorted(pad_end, jnp.arange(nblk) * MOE_BLOCK, side='right'),
                       0, N_EXPERTS - 1)
    x_pad = jnp.concatenate([hf, jnp.zeros((1, D), hf.dtype)], axis=0)
    xb = x_pad[buf_tok].reshape(nblk, MOE_BLOCK, D)

    def expert_block(args):
        xblk, e = args
        return (jax.nn.silu(xblk @ w_gate[e]) * (xblk @ w_up[e])) @ w_down[e]

    yb = lax.map(expert_block, (xb, blk_eid)).reshape(P, D)
    out = jnp.zeros((T + 1, D), yb.dtype).at[buf_tok].add(yb * buf_w[:, None].astype(yb.dtype))[:T]
    return out.reshape(B, S, D).astype(h.dtype)


def setup_inputs(seed: int = 0) -> dict:
    key = jax.random.key(seed)
    ks = jax.random.split(key, 24)
    f32 = jnp.float32
    nrm = lambda k, shape, s: jax.random.normal(k, shape, f32) * s
    L = DEPTH
    return {
        "x": nrm(ks[0], (BATCH, SEQ, D_MODEL), 1.0),
        "c": nrm(ks[1], (BATCH, D_MODEL), 1.0),
        "w_mod": nrm(ks[2], (L, D_MODEL, N_MOD * D_MODEL), 0.5 * D_MODEL ** -0.5),
        "b_mod": nrm(ks[3], (L, N_MOD * D_MODEL), 0.01),
        "g_norm1": 1.0 + nrm(ks[4], (L, D_MODEL), 0.02),
        "w_in": nrm(ks[5], (L, D_MODEL, D_IN), D_MODEL ** -0.5),
        "conv_w": nrm(ks[6], (L, CONV_WIDTH, D_CONV), CONV_WIDTH ** -0.5),
        "conv_b": nrm(ks[7], (L, D_CONV), 0.01),
        "conv_ln_g": 1.0 + nrm(ks[8], (L, D_CONV), 0.02),
        "conv_ln_b": nrm(ks[9], (L, D_CONV), 0.01),
        "w_out": nrm(ks[10], (L, D_MIX, D_MODEL), D_MIX ** -0.5),
        "g_norm2": 1.0 + nrm(ks[11], (L, D_MODEL), 0.02),
        "w_router_group": nrm(ks[12], (L, D_MODEL, N_GROUPS), D_MODEL ** -0.5),
        "b_router_group": nrm(ks[13], (L, N_GROUPS), 0.01),
        "w_router_expert": nrm(ks[14], (L, D_MODEL, N_EXPERTS), D_MODEL ** -0.5),
        "b_router_expert": nrm(ks[15], (L, N_EXPERTS), 0.01),
        "w_exp_gate": nrm(ks[16], (L, N_EXPERTS, D_MODEL, D_EXPERT), D_MODEL ** -0.5),
        "w_exp_up": nrm(ks[17], (L, N_EXPERTS, D_MODEL, D_EXPERT), D_MODEL ** -0.5),
        "w_exp_down": nrm(ks[18], (L, N_EXPERTS, D_EXPERT, D_MODEL), D_EXPERT ** -0.5),
        "g_final": 1.0 + nrm(ks[19], (D_MODEL,), 0.02),
    }


def reference(x, c, w_mod, b_mod, g_norm1, w_in, conv_w, conv_b, conv_ln_g, conv_ln_b, w_out,
              g_norm2, w_router_group, b_router_group, w_router_expert, b_router_expert,
              w_exp_gate, w_exp_up, w_exp_down, g_final):
    B, S, D = x.shape
    slopes = alibi_slopes(N_HEADS)
    c_act = jax.nn.silu(c)

    def split_heads(t):
        return t.reshape(B, S, N_HEADS, HEAD_DIM).transpose(0, 2, 1, 3)

    for l in range(DEPTH):
        mod = c_act @ w_mod[l] + b_mod[l]
        sh1, sc1, gt1, sh2, sc2, gt2 = jnp.split(mod[:, None, :], N_MOD, axis=-1)

        h = rms_norm(x, g_norm1[l]) * (1.0 + sc1) + sh1
        u = h @ w_in[l]
        q, k, v, ga, gb = jnp.split(
            u, [D_ATTN, 2 * D_ATTN, 3 * D_ATTN, 3 * D_ATTN + D_CONV], axis=-1)
        attn = dilated_attention(split_heads(q), split_heads(k), split_heads(v), slopes)
        attn = attn.transpose(0, 2, 1, 3).reshape(B, S, D_ATTN).astype(x.dtype)
        conv = conformer_conv(ga, gb, conv_w[l], conv_b[l], conv_ln_g[l], conv_ln_b[l])
        mix = jnp.concatenate([attn, conv], axis=-1) @ w_out[l]
        x = x + gt1 * mix

        h2 = rms_norm(x, g_norm2[l]) * (1.0 + sc2) + sh2
        x = x + gt2 * hier_moe(h2, w_router_group[l], b_router_group[l], w_router_expert[l],
                               b_router_expert[l], w_exp_gate[l], w_exp_up[l], w_exp_down[l])

    return rms_norm(x, g_final)
```

```python
import functools

import numpy as np
import jax
import jax.numpy as jnp
from jax import lax
from jax.experimental import pallas as pl
from jax.experimental.pallas import tpu as pltpu

F32 = jnp.float32
BF16 = jnp.bfloat16

HEAD_DIM = 64
N_HEADS = 8
D_ATTN = N_HEADS * HEAD_DIM
CONV_WIDTH = 31
DILATIONS = (1, 4, 16)
ATTN_BLOCK = 128
N_GROUPS = 4
EXPERTS_PER_GROUP = 8
N_EXPERTS = N_GROUPS * EXPERTS_PER_GROUP
NORM_EPS = 1e-6
N_MOD = 6
MASK_VALUE = -1e30
ROUTE_LANES = 128

SEQ_TILE = 512
CONV_HALO = 32
CONV_ROWS = 64
MOE_BLOCK = 256
DISPATCH_TILE = 2048
COMBINE_TILE = 512
VMEM_LIMIT = 56 * 1024 * 1024


def _cparams(sem):
    return pltpu.CompilerParams(dimension_semantics=sem, vmem_limit_bytes=VMEM_LIMIT)


def _mod_body(c_ref, w_ref, b_ref, o_ref):
    c = c_ref[...]
    c_act = c * jax.nn.sigmoid(c)
    o_ref[0] = jnp.dot(c_act, w_ref[0], preferred_element_type=F32,
                       precision=lax.Precision.HIGHEST) + b_ref[0]


def _modulation(c, w_mod, b_mod):
    depth, d, dm = w_mod.shape
    b = c.shape[0]
    nt = dm // d
    return pl.pallas_call(
        _mod_body,
        grid=(depth, nt),
        in_specs=[pl.BlockSpec((b, d), lambda l, j: (0, 0)),
                  pl.BlockSpec((1, d, d), lambda l, j: (l, 0, j)),
                  pl.BlockSpec((1, 1, d), lambda l, j: (l, 0, j))],
        out_specs=pl.BlockSpec((1, b, d), lambda l, j: (l, 0, j)),
        out_shape=jax.ShapeDtypeStruct((depth, b, dm), F32),
        compiler_params=_cparams(("arbitrary", "arbitrary")),
        name="modulation",
    )(c, w_mod, b_mod.reshape(depth, 1, dm))


def _rms_modulate(x, g, shift, scale):
    ms = jnp.mean(x * x, axis=-1, keepdims=True)
    return (x * lax.rsqrt(ms + NORM_EPS)) * g * (1.0 + scale) + shift


def _inproj_body(x_ref, mod_ref, g_ref, w_ref, q_ref, k_ref, v_ref, u_ref):
    h = _rms_modulate(x_ref[0], g_ref[...], mod_ref[0, 0:1, :], mod_ref[0, 1:2, :])
    y = jnp.dot(h.astype(BF16), w_ref[...], preferred_element_type=F32)
    da = D_ATTN
    q_ref[0] = y[:, 0:da] * (HEAD_DIM ** -0.5)
    k_ref[0] = y[:, da:2 * da]
    v_ref[0] = y[:, 2 * da:3 * da]
    dc = (y.shape[1] - 3 * da) // 2
    a = y[:, 3 * da:3 * da + dc]
    b = y[:, 3 * da + dc:]
    u_ref[0] = a * jax.nn.sigmoid(b)


def _input_projection(x, mod, g, w_in_bf):
    b, s, d = x.shape
    din = w_in_bf.shape[1]
    dc = (din - 3 * D_ATTN) // 2
    ts = SEQ_TILE
    tile = lambda n: pl.BlockSpec((1, ts, n), lambda i, j: (i, j, 0))
    return pl.pallas_call(
        _inproj_body,
        grid=(b, s // ts),
        in_specs=[tile(d),
                  pl.BlockSpec((1, N_MOD, d), lambda i, j: (i, 0, 0)),
                  pl.BlockSpec((1, d), lambda i, j: (0, 0)),
                  pl.BlockSpec((d, din), lambda i, j: (0, 0))],
        out_specs=[tile(D_ATTN), tile(D_ATTN), tile(D_ATTN), tile(dc)],
        out_shape=[jax.ShapeDtypeStruct((b, s, D_ATTN), F32)] * 3
        + [jax.ShapeDtypeStruct((b, s, dc), F32)],
        compiler_params=_cparams(("arbitrary", "arbitrary")),
        name="input_projection",
    )(x, mod, g.reshape(1, d), w_in_bf)


def _attn_bias_table():
    blk = ATTN_BLOCK
    slopes = jnp.asarray(2.0 ** (-8.0 * np.arange(1, N_HEADS + 1) / N_HEADS), dtype=F32)
    qi = np.arange(blk)[:, None] + blk
    ki = np.arange(2 * blk)[None, :]
    delta = qi - ki
    in_band = (delta >= 0) & (delta <= blk)
    tables = []
    for dil in DILATIONS:
        bias = -slopes[:, None, None] * jnp.asarray(delta * dil, dtype=F32)
        later = jnp.where(in_band[None], bias, MASK_VALUE)
        first = jnp.where((in_band & (ki >= blk))[None], bias, MASK_VALUE)
        tables.append(jnp.stack([later, first], axis=1))
    t = jnp.stack(tables, axis=1)
    return t.reshape(N_HEADS // 2, 2, len(DILATIONS), 2, blk, 2 * blk)


def _attn_body(q_ref, k_ref, v_ref, bias_ref, o_ref, acc_o, acc_m, acc_l):
    blk = ATTN_BLOCK
    s_len = q_ref.shape[1]
    lane = lax.broadcasted_iota(jnp.int32, (blk, 2 * HEAD_DIM), 1)
    left = lane < HEAD_DIM
    trans_b = (((1,), (1,)), ((), ()))

    for br, dil in enumerate(DILATIONS):
        nb = s_len // (blk * dil)

        def block(j, carry, br=br, dil=dil, nb=nb):
            r = j // nb
            n = j % nb
            qs = n * (blk * dil) + r
            ks = jnp.maximum(n - 1, 0) * (blk * dil) + r
            first = jnp.where(n == 0, 1, 0)

            def rows(ref, start):
                if dil == 1:
                    return ref[0, pl.ds(start, blk), :]
                return ref[0, pl.ds(start, blk, stride=dil), :]

            q = rows(q_ref, qs)
            kk = jnp.concatenate([rows(k_ref, ks), rows(k_ref, qs)], axis=0).astype(BF16)
            vv = jnp.concatenate([rows(v_ref, ks), rows(v_ref, qs)], axis=0).astype(BF16)
            outs = []
            for hh in range(2):
                qh = jnp.where(left if hh == 0 else ~left, q, 0.0).astype(BF16)
                s = lax.dot_general(qh, kk, trans_b, preferred_element_type=F32)
                s = s + bias_ref[0, hh, br, first]
                m = jnp.max(s, axis=-1, keepdims=True)
                p = jnp.exp(s - m)
                l = jnp.sum(p, axis=-1, keepdims=True)
                o = jnp.dot(p.astype(BF16), vv, preferred_element_type=F32)
                outs.append((o, m, l))
            (o0, m0, l0), (o1, m1, l1) = outs
            o = jnp.where(left, o0, o1)
            m = jnp.where(left, m0, m1)
            l = jnp.where(left, l0, l1)
            if dil == 1:
                dst = pl.ds(qs, blk)
            else:
                dst = pl.ds(qs, blk, stride=dil)
            acc_o[br, dst, :] = o
            acc_m[br, dst, :] = m
            acc_l[br, dst, :] = l
            return carry

        lax.fori_loop(0, s_len // blk, block, 0)

    rows_per_merge = 512

    def merge(i, carry):
        sl = pl.ds(pl.multiple_of(i * rows_per_merge, rows_per_merge), rows_per_merge)
        ms = [acc_m[br, sl, :] for br in range(len(DILATIONS))]
        m_max = jnp.maximum(jnp.maximum(ms[0], ms[1]), ms[2])
        num = 0.0
        den = 0.0
        for br in range(len(DILATIONS)):
            a = jnp.exp(ms[br] - m_max)
            num = num + a * acc_o[br, sl, :]
            den = den + a * acc_l[br, sl, :]
        o_ref[0, sl, :] = (num / den).astype(o_ref.dtype)
        return carry

    lax.fori_loop(0, s_len // rows_per_merge, merge, 0)


def _dilated_attention(q, k, v, bias):
    b, s, da = q.shape
    pw = 2 * HEAD_DIM
    seq = pl.BlockSpec((1, s, pw), lambda i, j: (i, 0, j))
    nbr = len(DILATIONS)
    return pl.pallas_call(
        _attn_body,
        grid=(b, da // pw),
        in_specs=[seq, seq, seq,
                  pl.BlockSpec((1,) + bias.shape[1:], lambda i, j: (j, 0, 0, 0, 0, 0))],
        out_specs=seq,
        out_shape=jax.ShapeDtypeStruct((b, s, da), BF16),
        scratch_shapes=[pltpu.VMEM((nbr, s, pw), F32)] * 3,
        compiler_params=_cparams(("arbitrary", "arbitrary")),
        name="dilated_attention",
    )(q, k, v, bias)


def _conv_body(uc_ref, uh_ref, w_ref, cb_ref, lg_ref, lb_ref, o_ref, win):
    t = pl.program_id(1)
    tc = uc_ref.shape[1]
    win[0:CONV_HALO, :] = jnp.where(t > 0, uh_ref[0], 0.0)
    win[CONV_HALO:, :] = uc_ref[0]
    off = CONV_HALO - (CONV_WIDTH - 1)
    for rb in range(tc // CONV_ROWS):
        r0 = rb * CONV_ROWS
        acc = jnp.zeros((CONV_ROWS, uc_ref.shape[2]), F32)
        for j in range(CONV_WIDTH):
            acc = acc + w_ref[j:j + 1, :] * win[r0 + off + j:r0 + off + j + CONV_ROWS, :]
        y = acc + cb_ref[...]
        mu = jnp.mean(y, axis=-1, keepdims=True)
        yc = y - mu
        var = jnp.mean(yc * yc, axis=-1, keepdims=True)
        z = yc * lax.rsqrt(var + NORM_EPS) * lg_ref[...] + lb_ref[...]
        o_ref[0, r0:r0 + CONV_ROWS, :] = (z * jax.nn.sigmoid(z)).astype(o_ref.dtype)


def _conformer_conv(u, conv_w, conv_b, ln_g, ln_b):
    b, s, dc = u.shape
    tc = SEQ_TILE
    per = tc // CONV_HALO
    row = pl.BlockSpec((1, dc), lambda i, j: (0, 0))
    return pl.pallas_call(
        _conv_body,
        grid=(b, s // tc),
        in_specs=[pl.BlockSpec((1, tc, dc), lambda i, j: (i, j, 0)),
                  pl.BlockSpec((1, CONV_HALO, dc), lambda i, j: (i, jnp.maximum(j * per - 1, 0), 0)),
                  pl.BlockSpec((CONV_WIDTH, dc), lambda i, j: (0, 0)),
                  row, row, row],
        out_specs=pl.BlockSpec((1, tc, dc), lambda i, j: (i, j, 0)),
        out_shape=jax.ShapeDtypeStruct((b, s, dc), BF16),
        scratch_shapes=[pltpu.VMEM((tc + CONV_HALO, dc), F32)],
        compiler_params=_cparams(("arbitrary", "arbitrary")),
        name="conformer_conv",
    )(u, u, conv_w, conv_b.reshape(1, dc), ln_g.reshape(1, dc), ln_b.reshape(1, dc))


def _route(logits):
    lane = lax.broadcasted_iota(jnp.int32, logits.shape, 1)
    big = jnp.int32(1 << 20)
    is_g = lane < N_GROUPS
    gl = jnp.where(is_g, logits, MASK_VALUE)
    gmax = jnp.max(gl, axis=-1, keepdims=True)
    grp = jnp.min(jnp.where(gl == gmax, lane, big), axis=-1, keepdims=True)
    gsum = jnp.sum(jnp.where(is_g, jnp.exp(gl - gmax), 0.0), axis=-1, keepdims=True)
    p_grp = 1.0 / gsum
    lo = N_GROUPS + grp * EXPERTS_PER_GROUP
    el = jnp.where((lane >= lo) & (lane < lo + EXPERTS_PER_GROUP), logits, MASK_VALUE)
    v1 = jnp.max(el, axis=-1, keepdims=True)
    i1 = jnp.min(jnp.where(el == v1, lane, big), axis=-1, keepdims=True)
    el2 = jnp.where(lane == i1, MASK_VALUE, el)
    v2 = jnp.max(el2, axis=-1, keepdims=True)
    i2 = jnp.min(jnp.where(el2 == v2, lane, big), axis=-1, keepdims=True)
    e = jnp.exp(v2 - v1)
    w1 = p_grp / (1.0 + e)
    w2 = p_grp * e / (1.0 + e)
    rec = jnp.where(lane == 0, (i1 - N_GROUPS).astype(F32),
                    jnp.where(lane == 1, (i2 - N_GROUPS).astype(F32),
                              jnp.where(lane == 2, w1, jnp.where(lane == 3, w2, 0.0))))
    return rec


def _outproj_body(a_ref, c_ref, x_ref, mod_ref, g_ref, w_ref, wr_ref, br_ref,
                  xo_ref, h_ref, r_ref):
    da = a_ref.shape[2]
    mix = jnp.dot(a_ref[0], w_ref[0:da, :], preferred_element_type=F32)
    mix = mix + jnp.dot(c_ref[0], w_ref[da:, :], preferred_element_type=F32)
    xn = x_ref[0] + mod_ref[0, 2:3, :] * mix
    xo_ref[0] = xn
    h = _rms_modulate(xn, g_ref[...], mod_ref[0, 3:4, :], mod_ref[0, 4:5, :])
    h_ref[0] = h
    logits = jnp.dot(h, wr_ref[...], preferred_element_type=F32,
                     precision=lax.Precision.HIGHEST) + br_ref[...]
    r_ref[0] = _route(logits)


def _output_projection(attn, conv, x, mod, g, w_out_bf, w_router, b_router):
    b, s, d = x.shape
    ts = SEQ_TILE
    tile = lambda n: pl.BlockSpec((1, ts, n), lambda i, j: (i, j, 0))
    full = lambda r, c: pl.BlockSpec((r, c), lambda i, j: (0, 0))
    return pl.pallas_call(
        _outproj_body,
        grid=(b, s // ts),
        in_specs=[tile(attn.shape[2]), tile(conv.shape[2]), tile(d),
                  pl.BlockSpec((1, N_MOD, d), lambda i, j: (i, 0, 0)),
                  full(1, d), full(d, d), full(d, ROUTE_LANES), full(1, ROUTE_LANES)],
        out_specs=[tile(d), tile(d), tile(ROUTE_LANES)],
        out_shape=[jax.ShapeDtypeStruct((b, s, d), F32),
                   jax.ShapeDtypeStruct((b, s, d), F32),
                   jax.ShapeDtypeStruct((b, s, ROUTE_LANES), F32)],
        compiler_params=_cparams(("arbitrary", "arbitrary")),
        name="output_projection",
    )(attn, conv, x, mod, g.reshape(1, d), w_out_bf, w_router, b_router)


def _row_copy(src, src_row, dst, dst_row, sem):
    return pltpu.make_async_copy(src.at[pl.ds(src_row, 1)], dst.at[pl.ds(dst_row, 1)], sem)


def _dispatch_body(dest_ref, h_hbm, xb_init, xb_hbm, sem):
    del xb_init
    base = pl.program_id(0) * DISPATCH_TILE

    def issue(t, carry):
        tok = base + t
        _row_copy(h_hbm, tok, xb_hbm, dest_ref[2 * tok], sem).start()
        _row_copy(h_hbm, tok, xb_hbm, dest_ref[2 * tok + 1], sem).start()
        return carry

    lax.fori_loop(0, DISPATCH_TILE, issue, 0)

    def drain(t, carry):
        _row_copy(h_hbm, 0, xb_hbm, 0, sem).wait()
        return carry

    lax.fori_loop(0, 2 * DISPATCH_TILE, drain, 0)


def _dispatch(h, dest, n_rows):
    t, d = h.shape
    xb_init = jnp.zeros((n_rows, d), h.dtype)
    any_spec = pl.BlockSpec(memory_space=pl.ANY)
    return pl.pallas_call(
        _dispatch_body,
        grid_spec=pltpu.PrefetchScalarGridSpec(
            num_scalar_prefetch=1,
            grid=(t // DISPATCH_TILE,),
            in_specs=[any_spec, any_spec],
            out_specs=any_spec,
            scratch_shapes=[pltpu.SemaphoreType.DMA(())]),
        out_shape=jax.ShapeDtypeStruct((n_rows, d), h.dtype),
        input_output_aliases={2: 0},
        compiler_params=_cparams(("arbitrary",)),
        name="moe_dispatch",
    )(dest, h, xb_init)


def _expert_body(eid_ref, nused_ref, x_ref, wg_ref, wu_ref, wd_ref, o_ref, wg_bf, wu_bf, wd_bf):
    i = pl.program_id(0)

    @pl.when(i < nused_ref[0])
    def _():
        changed = (i == 0) | (eid_ref[i] != eid_ref[jnp.maximum(i - 1, 0)])

        @pl.when(changed)
        def _():
            wg_bf[...] = wg_ref[0].astype(BF16)
            wu_bf[...] = wu_ref[0].astype(BF16)
            wd_bf[...] = wd_ref[0].astype(BF16)

        x = x_ref[...].astype(BF16)
        g = jnp.dot(x, wg_bf[...], preferred_element_type=F32)
        u = jnp.dot(x, wu_bf[...], preferred_element_type=F32)
        h = (g * jax.nn.sigmoid(g)) * u
        o_ref[...] = jnp.dot(h.astype(BF16), wd_bf[...], preferred_element_type=F32)


def _experts(xb, blk_eid, n_used, w_gate, w_up, w_down):
    p, d = xb.shape
    de = w_gate.shape[2]
    tm = MOE_BLOCK
    last = lambda i, nu: jnp.minimum(i, nu[0] - 1)
    rows = pl.BlockSpec((tm, d), lambda i, eid, nu: (last(i, nu), 0))
    return pl.pallas_call(
        _expert_body,
        grid_spec=pltpu.PrefetchScalarGridSpec(
            num_scalar_prefetch=2,
            grid=(p // tm,),
            in_specs=[rows,
                      pl.BlockSpec((1, d, de), lambda i, eid, nu: (eid[last(i, nu)], 0, 0)),
                      pl.BlockSpec((1, d, de), lambda i, eid, nu: (eid[last(i, nu)], 0, 0)),
                      pl.BlockSpec((1, de, d), lambda i, eid, nu: (eid[last(i, nu)], 0, 0))],
            out_specs=rows,
            scratch_shapes=[pltpu.VMEM((d, de), BF16), pltpu.VMEM((d, de), BF16),
                            pltpu.VMEM((de, d), BF16)]),
        out_shape=jax.ShapeDtypeStruct((p, d), F32),
        compiler_params=_cparams(("arbitrary",)),
        name="moe_experts",
    )(blk_eid, n_used, xb, w_gate, w_up, w_down)


def _combine_body(dest_ref, x_ref, r_ref, mod_ref, g_ref, yb_hbm, o_ref, y0, y1, sem, *, final):
    tcm = x_ref.shape[0]
    base = pl.program_id(0) * tcm

    def issue(t, carry):
        _row_copy(yb_hbm, dest_ref[2 * (base + t)], y0, t, sem.at[0]).start()
        _row_copy(yb_hbm, dest_ref[2 * (base + t) + 1], y1, t, sem.at[1]).start()
        return carry

    lax.fori_loop(0, tcm, issue, 0)

    def drain(t, carry):
        _row_copy(yb_hbm, 0, y0, 0, sem.at[0]).wait()
        _row_copy(yb_hbm, 0, y1, 0, sem.at[1]).wait()
        return carry

    lax.fori_loop(0, tcm, drain, 0)

    moe = r_ref[:, 2:3] * y0[...] + r_ref[:, 3:4] * y1[...]
    xn = x_ref[...] + mod_ref[0, 5:6, :] * moe
    if final:
        ms = jnp.mean(xn * xn, axis=-1, keepdims=True)
        xn = (xn * lax.rsqrt(ms + NORM_EPS)) * g_ref[...]
    o_ref[...] = xn


def _combine(x, route, mod, g_final, yb, dest, seq_len, final):
    t, d = x.shape
    tcm = COMBINE_TILE
    per_seq = seq_len // tcm
    return pl.pallas_call(
        functools.partial(_combine_body, final=final),
        grid_spec=pltpu.PrefetchScalarGridSpec(
            num_scalar_prefetch=1,
            grid=(t // tcm,),
            in_specs=[pl.BlockSpec((tcm, d), lambda i, dst: (i, 0)),
                      pl.BlockSpec((tcm, ROUTE_LANES), lambda i, dst: (i, 0)),
                      pl.BlockSpec((1, N_MOD, d), lambda i, dst: (i // per_seq, 0, 0)),
                      pl.BlockSpec((1, d), lambda i, dst: (0, 0)),
                      pl.BlockSpec(memory_space=pl.ANY)],
            out_specs=pl.BlockSpec((tcm, d), lambda i, dst: (i, 0)),
            scratch_shapes=[pltpu.VMEM((tcm, d), F32), pltpu.VMEM((tcm, d), F32),
                            pltpu.SemaphoreType.DMA((2,))]),
        out_shape=jax.ShapeDtypeStruct((t, d), F32),
        compiler_params=_cparams(("arbitrary",)),
        name="moe_combine",
    )(dest, x, route, mod, g_final.reshape(1, d), yb, )


def _plan_dispatch(route, n_blocks):
    tm = MOE_BLOCK
    eid = route[:, 0:2].astype(jnp.int32)
    onehot = (eid[:, :, None] == jnp.arange(N_EXPERTS, dtype=jnp.int32)).astype(jnp.int32)
    per_tok = onehot.sum(axis=1)
    incl = jnp.cumsum(per_tok, axis=0)
    rank = jnp.take_along_axis(incl - per_tok, eid, axis=1)
    counts = incl[-1]
    padded = ((counts + tm - 1) // tm) * tm
    pad_end = jnp.cumsum(padded)
    pad_start = pad_end - padded
    dest = (pad_start[eid] + rank).reshape(-1).astype(jnp.int32)
    blk_eid = jnp.clip(jnp.searchsorted(pad_end, jnp.arange(n_blocks, dtype=jnp.int32) * tm,
                                        side='right'), 0, N_EXPERTS - 1).astype(jnp.int32)
    n_used = (pad_end[-1:] // tm).astype(jnp.int32)
    return dest, blk_eid, n_used


def kernel(x, c, w_mod, b_mod, g_norm1, w_in, conv_w, conv_b, conv_ln_g, conv_ln_b, w_out, g_norm2, w_router_group, b_router_group, w_router_expert, b_router_expert, w_exp_gate, w_exp_up, w_exp_down, g_final):
    b, s, d = x.shape
    depth = w_mod.shape[0]
    t = b * s
    assert s % (ATTN_BLOCK * max(DILATIONS)) == 0 and s % SEQ_TILE == 0
    assert t % DISPATCH_TILE == 0 and t % COMBINE_TILE == 0 and s % COMBINE_TILE == 0
    n_blocks = -(-2 * t // MOE_BLOCK) + N_EXPERTS
    n_rows = n_blocks * MOE_BLOCK

    mod_all = _modulation(c, w_mod, b_mod).reshape(depth, b, N_MOD, d)
    bias = _attn_bias_table()
    pad = ROUTE_LANES - N_GROUPS - N_EXPERTS
    w_router = jnp.concatenate(
        [w_router_group, w_router_expert, jnp.zeros((depth, d, pad), F32)], axis=-1)
    b_router = jnp.concatenate(
        [b_router_group, b_router_expert, jnp.zeros((depth, pad), F32)], axis=-1)
    w_in_bf = w_in.astype(BF16)
    w_out_bf = w_out.astype(BF16)

    for l in range(depth):
        mod = mod_all[l]
        q, k, v, u = _input_projection(x, mod, g_norm1[l], w_in_bf[l])
        attn = _dilated_attention(q, k, v, bias)
        conv = _conformer_conv(u, conv_w[l], conv_b[l], conv_ln_g[l], conv_ln_b[l])
        x, h2, route = _output_projection(attn, conv, x, mod, g_norm2[l], w_out_bf[l],
                                          w_router[l], b_router[l].reshape(1, ROUTE_LANES))
        route = route.reshape(t, ROUTE_LANES)
        dest, blk_eid, n_used = _plan_dispatch(route, n_blocks)
        xb = _dispatch(h2.reshape(t, d), dest, n_rows)
        yb = _experts(xb, blk_eid, n_used, w_exp_gate[l], w_exp_up[l], w_exp_down[l])
        x = _combine(x.reshape(t, d), route, mod, g_final, yb, dest, s,
                     final=(l == depth - 1)).reshape(b, s, d)
    return x
```

```python
import functools

import numpy as np
import jax
import jax.numpy as jnp
from jax import lax
from jax.experimental import pallas as pl
from jax.experimental.pallas import tpu as pltpu

F32 = jnp.float32
BF16 = jnp.bfloat16

HEAD_DIM = 64
N_HEADS = 8
D_ATTN = N_HEADS * HEAD_DIM
CONV_WIDTH = 31
DILATIONS = (1, 4, 16)
ATTN_BLOCK = 128
ATTN_UNROLL = 2
N_GROUPS = 4
EXPERTS_PER_GROUP = 8
N_EXPERTS = N_GROUPS * EXPERTS_PER_GROUP
NORM_EPS = 1e-6
N_MOD = 6
MASK_VALUE = -1e30
ROUTE_LANES = 128
REC_LANE0, REC_LANE1, REC_W0, REC_W1, REC_RANK0, REC_RANK1 = range(6)

SEQ_TILE = 512
CONV_HALO = 32
CONV_ROWS = 64
MOE_BLOCK = 256
DISPATCH_TILE = 512
COMBINE_TILE = 512
DMA_UNROLL = 8
VMEM_LIMIT = 56 * 1024 * 1024


def _cparams(sem):
    return pltpu.CompilerParams(dimension_semantics=sem, vmem_limit_bytes=VMEM_LIMIT)


def _mod_body(c_ref, w_ref, b_ref, o_ref):
    c = c_ref[...]
    c_act = c * jax.nn.sigmoid(c)
    o_ref[0] = jnp.dot(c_act, w_ref[0], preferred_element_type=F32,
                       precision=lax.Precision.HIGHEST) + b_ref[0]


def _modulation(c, w_mod, b_mod):
    depth, d, dm = w_mod.shape
    b = c.shape[0]
    nt = dm // d
    return pl.pallas_call(
        _mod_body,
        grid=(depth, nt),
        in_specs=[pl.BlockSpec((b, d), lambda l, j: (0, 0)),
                  pl.BlockSpec((1, d, d), lambda l, j: (l, 0, j)),
                  pl.BlockSpec((1, 1, d), lambda l, j: (l, 0, j))],
        out_specs=pl.BlockSpec((1, b, d), lambda l, j: (l, 0, j)),
        out_shape=jax.ShapeDtypeStruct((depth, b, dm), F32),
        compiler_params=_cparams(("arbitrary", "arbitrary")),
        name="modulation",
    )(c, w_mod, b_mod.reshape(depth, 1, dm))


def _rms_modulate(x, g, shift, scale):
    ms = jnp.mean(x * x, axis=-1, keepdims=True)
    return (x * lax.rsqrt(ms + NORM_EPS)) * g * (1.0 + scale) + shift


def _inproj_body(x_ref, mod_ref, g_ref, w_ref, q_ref, k_ref, v_ref, u_ref):
    h = _rms_modulate(x_ref[0], g_ref[...], mod_ref[0, 0:1, :], mod_ref[0, 1:2, :])
    y = jnp.dot(h.astype(BF16), w_ref[0], preferred_element_type=F32)
    da = D_ATTN
    q_ref[0] = y[:, 0:da] * (HEAD_DIM ** -0.5)
    k_ref[0] = y[:, da:2 * da]
    v_ref[0] = y[:, 2 * da:3 * da]
    dc = (y.shape[1] - 3 * da) // 2
    a = y[:, 3 * da:3 * da + dc]
    b = y[:, 3 * da + dc:]
    u_ref[0] = a * jax.nn.sigmoid(b)


def _input_projection(x, mod, g, w_in_bf, layer):
    b, s, d = x.shape
    din = w_in_bf.shape[2]
    dc = (din - 3 * D_ATTN) // 2
    ts = SEQ_TILE
    tile = lambda n: pl.BlockSpec((1, ts, n), lambda i, j: (i, j, 0))
    return pl.pallas_call(
        _inproj_body,
        grid=(b, s // ts),
        in_specs=[tile(d),
                  pl.BlockSpec((1, N_MOD, d), lambda i, j: (i, 0, 0)),
                  pl.BlockSpec((1, d), lambda i, j: (0, 0)),
                  pl.BlockSpec((1, d, din), lambda i, j: (layer, 0, 0))],
        out_specs=[tile(D_ATTN), tile(D_ATTN), tile(D_ATTN), tile(dc)],
        out_shape=[jax.ShapeDtypeStruct((b, s, D_ATTN), F32)] * 3
        + [jax.ShapeDtypeStruct((b, s, dc), F32)],
        compiler_params=_cparams(("arbitrary", "arbitrary")),
        name="input_projection",
    )(x, mod, g.reshape(1, d), w_in_bf)


def _attn_bias_table():
    blk = ATTN_BLOCK
    slopes = jnp.asarray(2.0 ** (-8.0 * np.arange(1, N_HEADS + 1) / N_HEADS), dtype=F32)
    qi = np.arange(blk)[:, None] + blk
    ki = np.arange(2 * blk)[None, :]
    delta = qi - ki
    in_band = (delta >= 0) & (delta <= blk)
    tables = []
    for dil in DILATIONS:
        bias = -slopes[:, None, None] * jnp.asarray(delta * dil, dtype=F32)
        later = jnp.where(in_band[None], bias, MASK_VALUE)
        first = jnp.where((in_band & (ki >= blk))[None], bias, MASK_VALUE)
        tables.append(jnp.stack([later, first], axis=1))
    t = jnp.stack(tables, axis=1)
    return t.reshape(N_HEADS // 2, 2, len(DILATIONS), 2, blk, 2 * blk)


def _attn_body(q_ref, k_ref, v_ref, bias_ref, o_ref, acc_o, acc_m):
    blk = ATTN_BLOCK
    s_len = q_ref.shape[1]
    lane = lax.broadcasted_iota(jnp.int32, (blk, 2 * HEAD_DIM), 1)
    left = lane < HEAD_DIM
    left2 = lax.broadcasted_iota(jnp.int32, (2 * blk, 2 * HEAD_DIM), 1) < HEAD_DIM
    trans_b = (((1,), (1,)), ((), ()))
    n_branch = len(DILATIONS)

    for br, dil in enumerate(DILATIONS):
        nb = s_len // (blk * dil)

        def block(j, carry, br=br, dil=dil, nb=nb):
            r = j // nb
            n = j % nb
            qs = n * (blk * dil) + r
            ks = jnp.maximum(n - 1, 0) * (blk * dil) + r
            first = jnp.where(n == 0, 1, 0)

            def rows(ref, start):
                if dil == 1:
                    return ref[0, pl.ds(start, blk), :]
                return ref[0, pl.ds(start, blk, stride=dil), :]

            dst = pl.ds(qs, blk) if dil == 1 else pl.ds(qs, blk, stride=dil)
            q = rows(q_ref, qs).astype(BF16)
            kk = jnp.concatenate([rows(k_ref, ks), rows(k_ref, qs)], axis=0).astype(BF16)
            vv = jnp.concatenate([rows(v_ref, ks), rows(v_ref, qs)], axis=0).astype(BF16)
            maxes = []
            for hh in range(2):
                mine = left if hh == 0 else ~left
                mine2 = left2 if hh == 0 else ~left2
                qh = jnp.where(mine, q, jnp.zeros_like(q))
                vh = jnp.where(mine2, vv, jnp.ones_like(vv))
                s = lax.dot_general(qh, kk, trans_b, preferred_element_type=F32)
                s = s + bias_ref[0, hh, br, first]
                m = jnp.max(s, axis=-1, keepdims=True)
                p = jnp.exp(s - m).astype(BF16)
                acc_o[2 * br + hh, dst, :] = jnp.dot(p, vh, preferred_element_type=F32)
                maxes.append(m)
            acc_m[br, dst, :] = jnp.where(left, maxes[0], maxes[1])
            return carry

        lax.fori_loop(0, s_len // blk, block, 0, unroll=ATTN_UNROLL)

    rows_per_merge = 256

    def merge(i, carry):
        sl = pl.ds(pl.multiple_of(i * rows_per_merge, rows_per_merge), rows_per_merge)
        left_m = lax.broadcasted_iota(jnp.int32, (rows_per_merge, 2 * HEAD_DIM), 1) < HEAD_DIM
        ms = [acc_m[br, sl, :] for br in range(n_branch)]
        m_max = jnp.maximum(jnp.maximum(ms[0], ms[1]), ms[2])
        tot = [0.0, 0.0]
        for br in range(n_branch):
            a = jnp.exp(ms[br] - m_max)
            a_swapped = pltpu.roll(a, HEAD_DIM, 1)
            tot[0] = tot[0] + jnp.where(left_m, a, a_swapped) * acc_o[2 * br, sl, :]
            tot[1] = tot[1] + jnp.where(left_m, a_swapped, a) * acc_o[2 * br + 1, sl, :]
        out = jnp.where(left_m, tot[0] / pltpu.roll(tot[0], HEAD_DIM, 1),
                        tot[1] / pltpu.roll(tot[1], HEAD_DIM, 1))
        o_ref[0, sl, :] = out.astype(o_ref.dtype)
        return carry

    lax.fori_loop(0, s_len // rows_per_merge, merge, 0)


def _dilated_attention(q, k, v, bias):
    b, s, da = q.shape
    pw = 2 * HEAD_DIM
    seq = pl.BlockSpec((1, s, pw), lambda i, j: (i, 0, j))
    nbr = len(DILATIONS)
    return pl.pallas_call(
        _attn_body,
        grid=(b, da // pw),
        in_specs=[seq, seq, seq,
                  pl.BlockSpec((1,) + bias.shape[1:], lambda i, j: (j, 0, 0, 0, 0, 0))],
        out_specs=seq,
        out_shape=jax.ShapeDtypeStruct((b, s, da), BF16),
        scratch_shapes=[pltpu.VMEM((2 * nbr, s, pw), F32), pltpu.VMEM((nbr, s, pw), F32)],
        compiler_params=_cparams(("arbitrary", "arbitrary")),
        name="dilated_attention",
    )(q, k, v, bias)


def _conv_body(uc_ref, uh_ref, w_ref, cb_ref, lg_ref, lb_ref, o_ref, win):
    t = pl.program_id(1)
    tc = uc_ref.shape[1]
    win[0:CONV_HALO, :] = jnp.where(t > 0, uh_ref[0], 0.0)
    win[CONV_HALO:, :] = uc_ref[0]
    off = CONV_HALO - (CONV_WIDTH - 1)
    for rb in range(tc // CONV_ROWS):
        r0 = rb * CONV_ROWS
        acc = jnp.zeros((CONV_ROWS, uc_ref.shape[2]), F32)
        for j in range(CONV_WIDTH):
            acc = acc + w_ref[j:j + 1, :] * win[r0 + off + j:r0 + off + j + CONV_ROWS, :]
        y = acc + cb_ref[...]
        mu = jnp.mean(y, axis=-1, keepdims=True)
        yc = y - mu
        var = jnp.mean(yc * yc, axis=-1, keepdims=True)
        z = yc * lax.rsqrt(var + NORM_EPS) * lg_ref[...] + lb_ref[...]
        o_ref[0, r0:r0 + CONV_ROWS, :] = (z * jax.nn.sigmoid(z)).astype(o_ref.dtype)


def _conformer_conv(u, conv_w, conv_b, ln_g, ln_b):
    b, s, dc = u.shape
    tc = SEQ_TILE
    per = tc // CONV_HALO
    row = pl.BlockSpec((1, dc), lambda i, j: (0, 0))
    return pl.pallas_call(
        _conv_body,
        grid=(b, s // tc),
        in_specs=[pl.BlockSpec((1, tc, dc), lambda i, j: (i, j, 0)),
                  pl.BlockSpec((1, CONV_HALO, dc), lambda i, j: (i, jnp.maximum(j * per - 1, 0), 0)),
                  pl.BlockSpec((CONV_WIDTH, dc), lambda i, j: (0, 0)),
                  row, row, row],
        out_specs=pl.BlockSpec((1, tc, dc), lambda i, j: (i, j, 0)),
        out_shape=jax.ShapeDtypeStruct((b, s, dc), BF16),
        scratch_shapes=[pltpu.VMEM((tc + CONV_HALO, dc), F32)],
        compiler_params=_cparams(("arbitrary", "arbitrary")),
        name="conformer_conv",
    )(u, u, conv_w, conv_b.reshape(1, dc), ln_g.reshape(1, dc), ln_b.reshape(1, dc))


def _route(logits, seen):
    rows = logits.shape[0]
    lane = lax.broadcasted_iota(jnp.int32, logits.shape, 1)
    big = jnp.int32(1 << 20)
    is_g = lane < N_GROUPS
    gl = jnp.where(is_g, logits, MASK_VALUE)
    gmax = jnp.max(gl, axis=-1, keepdims=True)
    grp = jnp.min(jnp.where(gl == gmax, lane, big), axis=-1, keepdims=True)
    gsum = jnp.sum(jnp.where(is_g, jnp.exp(gl - gmax), 0.0), axis=-1, keepdims=True)
    p_grp = 1.0 / gsum
    lo = N_GROUPS + grp * EXPERTS_PER_GROUP
    el = jnp.where((lane >= lo) & (lane < lo + EXPERTS_PER_GROUP), logits, MASK_VALUE)
    v1 = jnp.max(el, axis=-1, keepdims=True)
    i1 = jnp.min(jnp.where(el == v1, lane, big), axis=-1, keepdims=True)
    el2 = jnp.where(lane == i1, MASK_VALUE, el)
    v2 = jnp.max(el2, axis=-1, keepdims=True)
    i2 = jnp.min(jnp.where(el2 == v2, lane, big), axis=-1, keepdims=True)
    e = jnp.exp(v2 - v1)
    w1 = p_grp / (1.0 + e)
    w2 = p_grp * e / (1.0 + e)

    hit1 = lane == i1
    hit2 = lane == i2
    chosen = jnp.where(hit1 | hit2, 1.0, 0.0)
    ri = lax.broadcasted_iota(jnp.int32, (rows, rows), 0)
    ci = lax.broadcasted_iota(jnp.int32, (rows, rows), 1)
    earlier = jnp.where(ri > ci, 1.0, 0.0).astype(BF16)
    before = jnp.dot(earlier, chosen.astype(BF16), preferred_element_type=F32) + seen
    rank1 = jnp.sum(jnp.where(hit1, before, 0.0), axis=-1, keepdims=True)
    rank2 = jnp.sum(jnp.where(hit2, before, 0.0), axis=-1, keepdims=True)
    seen = seen + jnp.sum(chosen, axis=0, keepdims=True)

    rec = jnp.zeros(logits.shape, F32)
    for pos, val in ((REC_LANE0, i1.astype(F32)), (REC_LANE1, i2.astype(F32)), (REC_W0, w1),
                     (REC_W1, w2), (REC_RANK0, rank1), (REC_RANK1, rank2)):
        rec = jnp.where(lane == pos, val, rec)
    return rec, seen


def _split_bf16(x):
    hi = x.astype(BF16)
    lo = (x - hi.astype(F32)).astype(BF16)
    return hi, lo


def _outproj_body(a_ref, c_ref, x_ref, mod_ref, g_ref, w_ref, wrh_ref, wrl_ref, br_ref,
                  xo_ref, h_ref, r_ref, cnt_ref, seen):
    @pl.when((pl.program_id(0) == 0) & (pl.program_id(1) == 0))
    def _():
        seen[...] = jnp.zeros_like(seen)

    da = a_ref.shape[2]
    mix = jnp.dot(a_ref[0], w_ref[0, 0:da, :], preferred_element_type=F32)
    mix = mix + jnp.dot(c_ref[0], w_ref[0, da:, :], preferred_element_type=F32)
    xn = x_ref[0] + mod_ref[0, 2:3, :] * mix
    xo_ref[0] = xn
    h = _rms_modulate(xn, g_ref[...], mod_ref[0, 3:4, :], mod_ref[0, 4:5, :])
    h_ref[0] = h
    h_hi, h_lo = _split_bf16(h)
    logits = (jnp.dot(h_hi, wrh_ref[0], preferred_element_type=F32)
              + jnp.dot(h_lo, wrh_ref[0], preferred_element_type=F32)
              + jnp.dot(h_hi, wrl_ref[0], preferred_element_type=F32)) + br_ref[0]
    rec, new_seen = _route(logits, seen[...])
    r_ref[0] = rec
    seen[...] = new_seen
    cnt_ref[...] = new_seen


def _output_projection(attn, conv, x, mod, g, w_out_bf, wr_hi, wr_lo, b_router, layer):
    b, s, d = x.shape
    ts = SEQ_TILE
    tile = lambda n: pl.BlockSpec((1, ts, n), lambda i, j: (i, j, 0))
    const = lambda r, c: pl.BlockSpec((r, c), lambda i, j: (0, 0))
    per_layer = lambda r, c: pl.BlockSpec((1, r, c), lambda i, j: (layer, 0, 0))
    return pl.pallas_call(
        _outproj_body,
        grid=(b, s // ts),
        in_specs=[tile(attn.shape[2]), tile(conv.shape[2]), tile(d),
                  pl.BlockSpec((1, N_MOD, d), lambda i, j: (i, 0, 0)),
                  const(1, d), per_layer(d, d), per_layer(d, ROUTE_LANES),
                  per_layer(d, ROUTE_LANES), per_layer(1, ROUTE_LANES)],
        out_specs=[tile(d), tile(d), tile(ROUTE_LANES), const(1, ROUTE_LANES)],
        out_shape=[jax.ShapeDtypeStruct((b, s, d), F32),
                   jax.ShapeDtypeStruct((b, s, d), F32),
                   jax.ShapeDtypeStruct((b, s, ROUTE_LANES), F32),
                   jax.ShapeDtypeStruct((1, ROUTE_LANES), F32)],
        scratch_shapes=[pltpu.VMEM((1, ROUTE_LANES), F32)],
        compiler_params=_cparams(("arbitrary", "arbitrary")),
        name="output_projection",
    )(attn, conv, x, mod, g.reshape(1, d), w_out_bf, wr_hi, wr_lo, b_router)


def _row_copy(src, src_row, dst, dst_row, sem):
    return pltpu.make_async_copy(src.at[pl.ds(src_row, 1)], dst.at[pl.ds(dst_row, 1)], sem)


def _wait_rows(like, n_rows, sem):
    pltpu.make_async_copy(like.at[pl.ds(0, n_rows)], like.at[pl.ds(0, n_rows)], sem).wait()


def _dest_row(lane_ref, rank_ref, start_ref, a):
    return start_ref[lane_ref[a] - N_GROUPS] + rank_ref[a]


def _dispatch_body(lane_ref, rank_ref, start_ref, h_ref, xb_hbm, sem):
    td = h_ref.shape[0]
    base = pl.program_id(0) * td

    def issue(t, carry):
        for choice in range(2):
            dst = _dest_row(lane_ref, rank_ref, start_ref, 2 * (base + t) + choice)
            _row_copy(h_ref, t, xb_hbm, dst, sem).start()
        return carry

    lax.fori_loop(0, td, issue, 0, unroll=DMA_UNROLL)
    _wait_rows(xb_hbm, 2 * td, sem)


def _dispatch(h, lanes, ranks, starts, n_rows):
    t, d = h.shape
    td = DISPATCH_TILE
    return pl.pallas_call(
        _dispatch_body,
        grid_spec=pltpu.PrefetchScalarGridSpec(
            num_scalar_prefetch=3,
            grid=(t // td,),
            in_specs=[pl.BlockSpec((td, d), lambda i, *_: (i, 0))],
            out_specs=pl.BlockSpec(memory_space=pl.ANY),
            scratch_shapes=[pltpu.SemaphoreType.DMA(())]),
        out_shape=jax.ShapeDtypeStruct((n_rows, d), h.dtype),
        compiler_params=_cparams(("arbitrary",)),
        name="moe_dispatch",
    )(lanes, ranks, starts, h)


def _expert_body(eid_ref, valid_ref, nused_ref, x_ref, wg_ref, wu_ref, wd_ref, o_ref,
                 wg_bf, wu_bf, wd_bf):
    i = pl.program_id(0)

    @pl.when(i < nused_ref[0])
    def _():
        changed = (i == 0) | (eid_ref[i] != eid_ref[jnp.maximum(i - 1, 0)])

        @pl.when(changed)
        def _():
            wg_bf[...] = wg_ref[0, 0].astype(BF16)
            wu_bf[...] = wu_ref[0, 0].astype(BF16)
            wd_bf[...] = wd_ref[0, 0].astype(BF16)

        row = lax.broadcasted_iota(jnp.int32, x_ref.shape, 0)
        x = jnp.where(row < valid_ref[i], x_ref[...], 0.0).astype(BF16)
        g = jnp.dot(x, wg_bf[...], preferred_element_type=F32)
        u = jnp.dot(x, wu_bf[...], preferred_element_type=F32)
        h = (g * jax.nn.sigmoid(g)) * u
        o_ref[...] = jnp.dot(h.astype(BF16), wd_bf[...], preferred_element_type=F32)

    @pl.when(i >= nused_ref[0])
    def _():
        o_ref[...] = jnp.zeros_like(o_ref)


def _experts(xb, blk_eid, blk_valid, n_used, w_gate, w_up, w_down, layer):
    p, d = xb.shape
    de = w_gate.shape[3]
    tm = MOE_BLOCK
    last = lambda i, nu: jnp.minimum(i, nu[0] - 1)
    rows = pl.BlockSpec((tm, d), lambda i, eid, nv, nu: (last(i, nu), 0))
    weight = lambda r, c: pl.BlockSpec(
        (1, 1, r, c), lambda i, eid, nv, nu: (layer, eid[last(i, nu)], 0, 0))
    return pl.pallas_call(
        _expert_body,
        grid_spec=pltpu.PrefetchScalarGridSpec(
            num_scalar_prefetch=3,
            grid=(p // tm,),
            in_specs=[rows, weight(d, de), weight(d, de), weight(de, d)],
            out_specs=pl.BlockSpec((tm, d), lambda i, eid, nv, nu: (i, 0)),
            scratch_shapes=[pltpu.VMEM((d, de), BF16), pltpu.VMEM((d, de), BF16),
                            pltpu.VMEM((de, d), BF16)]),
        out_shape=jax.ShapeDtypeStruct((p, d), F32),
        compiler_params=_cparams(("arbitrary",)),
        name="moe_experts",
    )(blk_eid, blk_valid, n_used, xb, w_gate, w_up, w_down)


def _combine_body(lane_ref, rank_ref, start_ref, x_ref, r_ref, mod_ref, g_ref, yb_hbm, o_ref,
                  y0, y1, sem, *, final):
    tcm = x_ref.shape[0]
    base = pl.program_id(0) * tcm

    def issue(t, carry):
        for choice, buf in enumerate((y0, y1)):
            src = _dest_row(lane_ref, rank_ref, start_ref, 2 * (base + t) + choice)
            _row_copy(yb_hbm, src, buf, t, sem.at[choice]).start()
        return carry

    lax.fori_loop(0, tcm, issue, 0, unroll=DMA_UNROLL)
    _wait_rows(y0, tcm, sem.at[0])
    _wait_rows(y1, tcm, sem.at[1])

    moe = r_ref[:, REC_W0:REC_W0 + 1] * y0[...] + r_ref[:, REC_W1:REC_W1 + 1] * y1[...]
    xn = x_ref[...] + mod_ref[0, 5:6, :] * moe
    if final:
        ms = jnp.mean(xn * xn, axis=-1, keepdims=True)
        xn = (xn * lax.rsqrt(ms + NORM_EPS)) * g_ref[...]
    o_ref[...] = xn


def _combine(x, route, mod, g_final, yb, lanes, ranks, starts, seq_len, final):
    t, d = x.shape
    tcm = COMBINE_TILE
    per_seq = seq_len // tcm
    return pl.pallas_call(
        functools.partial(_combine_body, final=final),
        grid_spec=pltpu.PrefetchScalarGridSpec(
            num_scalar_prefetch=3,
            grid=(t // tcm,),
            in_specs=[pl.BlockSpec((tcm, d), lambda i, *_: (i, 0)),
                      pl.BlockSpec((tcm, ROUTE_LANES), lambda i, *_: (i, 0)),
                      pl.BlockSpec((1, N_MOD, d), lambda i, *_: (i // per_seq, 0, 0)),
                      pl.BlockSpec((1, d), lambda i, *_: (0, 0)),
                      pl.BlockSpec(memory_space=pl.ANY)],
            out_specs=pl.BlockSpec((tcm, d), lambda i, *_: (i, 0)),
            scratch_shapes=[pltpu.VMEM((tcm, d), F32), pltpu.VMEM((tcm, d), F32),
                            pltpu.SemaphoreType.DMA((2,))]),
        out_shape=jax.ShapeDtypeStruct((t, d), F32),
        compiler_params=_cparams(("arbitrary",)),
        name="moe_combine",
    )(lanes, ranks, starts, x, route, mod, g_final.reshape(1, d), yb)


def _plan_blocks(counts, n_blocks):
    tm = MOE_BLOCK
    padded = ((counts + tm - 1) // tm) * tm
    pad_end = jnp.cumsum(padded)
    pad_start = pad_end - padded
    blk_row = jnp.arange(n_blocks, dtype=jnp.int32) * tm
    blk_eid = jnp.clip(jnp.searchsorted(pad_end, blk_row, side='right'), 0, N_EXPERTS - 1)
    blk_valid = jnp.clip(pad_start[blk_eid] + counts[blk_eid] - blk_row, 0, tm)
    n_used = pad_end[-1:] // tm
    i32 = lambda a: a.astype(jnp.int32)
    return i32(pad_start), i32(blk_eid), i32(blk_valid), i32(n_used)


def kernel(x, c, w_mod, b_mod, g_norm1, w_in, conv_w, conv_b, conv_ln_g, conv_ln_b, w_out, g_norm2, w_router_group, b_router_group, w_router_expert, b_router_expert, w_exp_gate, w_exp_up, w_exp_down, g_final):
    b, s, d = x.shape
    depth = w_mod.shape[0]
    t = b * s
    assert s % (ATTN_BLOCK * max(DILATIONS)) == 0 and s % SEQ_TILE == 0
    assert t % DISPATCH_TILE == 0 and s % COMBINE_TILE == 0
    n_blocks = -(-2 * t // MOE_BLOCK) + N_EXPERTS
    n_rows = n_blocks * MOE_BLOCK

    mod_all = _modulation(c, w_mod, b_mod).reshape(depth, b, N_MOD, d)
    bias = _attn_bias_table()
    pad = ROUTE_LANES - N_GROUPS - N_EXPERTS
    w_router = jnp.concatenate(
        [w_router_group, w_router_expert, jnp.zeros((depth, d, pad), F32)], axis=-1)
    b_router = jnp.concatenate(
        [b_router_group, b_router_expert, jnp.zeros((depth, pad), F32)], axis=-1)
    b_router = b_router.reshape(depth, 1, ROUTE_LANES)
    wr_hi = w_router.astype(BF16)
    wr_lo = (w_router - wr_hi.astype(F32)).astype(BF16)
    w_in_bf = w_in.astype(BF16)
    w_out_bf = w_out.astype(BF16)

    for l in range(depth):
        mod = mod_all[l]
        q, k, v, u = _input_projection(x, mod, g_norm1[l], w_in_bf, l)
        attn = _dilated_attention(q, k, v, bias)
        conv = _conformer_conv(u, conv_w[l], conv_b[l], conv_ln_g[l], conv_ln_b[l])
        x, h2, route, seen = _output_projection(attn, conv, x, mod, g_norm2[l], w_out_bf,
                                                wr_hi, wr_lo, b_router, l)
        route = route.reshape(t, ROUTE_LANES)
        lanes = route[:, REC_LANE0:REC_LANE1 + 1].astype(jnp.int32).reshape(-1)
        ranks = route[:, REC_RANK0:REC_RANK1 + 1].astype(jnp.int32).reshape(-1)
        counts = seen[0, N_GROUPS:N_GROUPS + N_EXPERTS].astype(jnp.int32)
        starts, blk_eid, blk_valid, n_used = _plan_blocks(counts, n_blocks)
        xb = _dispatch(h2.reshape(t, d), lanes, ranks, starts, n_rows)
        yb = _experts(xb, blk_eid, blk_valid, n_used, w_exp_gate, w_exp_up, w_exp_down, l)
        x = _combine(x.reshape(t, d), route, mod, g_final, yb, lanes, ranks, starts, s,
                     final=(l == depth - 1)).reshape(b, s, d)
    return x
```

```python
import functools

import numpy as np
import jax
import jax.numpy as jnp
from jax import lax
from jax.experimental import pallas as pl
from jax.experimental.pallas import tpu as pltpu

F32 = jnp.float32
BF16 = jnp.bfloat16

SUBLANES = 8
LANES = 128
HEAD_DIM = 64
N_HEADS = 8
D_ATTN = N_HEADS * HEAD_DIM
CONV_WIDTH = 31
DILATIONS = (1, 4, 16)
ATTN_BLOCK = 128
ATTN_UNROLL = 6
N_GROUPS = 4
EXPERTS_PER_GROUP = 8
N_EXPERTS = N_GROUPS * EXPERTS_PER_GROUP
NORM_EPS = 1e-6
N_MOD = 6
MASK_VALUE = -1e30
ROUTE_LANES = 128
REC_LANE0, REC_LANE1, REC_W0, REC_W1, REC_RANK0, REC_RANK1 = range(6)

SEQ_TILE = 512
CONV_HALO = 32
CONV_ROWS = 64
MOE_BLOCK = 256
DISPATCH_TILE = 512
COMBINE_TILE = 512
DMA_UNROLL = 8
VMEM_LIMIT = 56 * 1024 * 1024


def _cparams(sem):
    return pltpu.CompilerParams(dimension_semantics=sem, vmem_limit_bytes=VMEM_LIMIT)


def _mod_body(c_ref, w_ref, b_ref, o_ref):
    c = c_ref[...]
    c_act = c * jax.nn.sigmoid(c)
    o_ref[0] = jnp.dot(c_act, w_ref[0], preferred_element_type=F32,
                       precision=lax.Precision.HIGHEST) + b_ref[0]


def _modulation(c, w_mod, b_mod):
    depth, d, dm = w_mod.shape
    b = c.shape[0]
    nt = dm // d
    return pl.pallas_call(
        _mod_body,
        grid=(depth, nt),
        in_specs=[pl.BlockSpec((b, d), lambda l, j: (0, 0)),
                  pl.BlockSpec((1, d, d), lambda l, j: (l, 0, j)),
                  pl.BlockSpec((1, 1, d), lambda l, j: (l, 0, j))],
        out_specs=pl.BlockSpec((1, b, d), lambda l, j: (l, 0, j)),
        out_shape=jax.ShapeDtypeStruct((depth, b, dm), F32),
        compiler_params=_cparams(("arbitrary", "arbitrary")),
        name="modulation",
    )(c, w_mod, b_mod.reshape(depth, 1, dm))


def _rms_modulate(x, g, shift, scale):
    ms = jnp.mean(x * x, axis=-1, keepdims=True)
    return (x * lax.rsqrt(ms + NORM_EPS)) * g * (1.0 + scale) + shift


def _inproj_body(x_ref, mod_ref, g_ref, w_ref, q_ref, k_ref, v_ref, u_ref):
    h = _rms_modulate(x_ref[0], g_ref[...], mod_ref[0, 0:1, :], mod_ref[0, 1:2, :])
    y = jnp.dot(h.astype(BF16), w_ref[0], preferred_element_type=F32)
    da = D_ATTN
    q_ref[0] = y[:, 0:da] * (HEAD_DIM ** -0.5)
    k_ref[0] = y[:, da:2 * da]
    v_ref[0] = y[:, 2 * da:3 * da]
    dc = (y.shape[1] - 3 * da) // 2
    a = y[:, 3 * da:3 * da + dc]
    b = y[:, 3 * da + dc:]
    u_ref[0] = a * jax.nn.sigmoid(b)


def _input_projection(x, mod, g, w_in_bf, layer):
    b, s, d = x.shape
    din = w_in_bf.shape[2]
    dc = (din - 3 * D_ATTN) // 2
    ts = SEQ_TILE
    tile = lambda n: pl.BlockSpec((1, ts, n), lambda i, j: (i, j, 0))
    return pl.pallas_call(
        _inproj_body,
        grid=(b, s // ts),
        in_specs=[tile(d),
                  pl.BlockSpec((1, N_MOD, d), lambda i, j: (i, 0, 0)),
                  pl.BlockSpec((1, d), lambda i, j: (0, 0)),
                  pl.BlockSpec((1, d, din), lambda i, j: (layer, 0, 0))],
        out_specs=[tile(D_ATTN), tile(D_ATTN), tile(D_ATTN), tile(dc)],
        out_shape=[jax.ShapeDtypeStruct((b, s, D_ATTN), F32)] * 3
        + [jax.ShapeDtypeStruct((b, s, dc), F32)],
        compiler_params=_cparams(("arbitrary", "arbitrary")),
        name="input_projection",
    )(x, mod, g.reshape(1, d), w_in_bf)


def _attn_bias_table():
    blk = ATTN_BLOCK
    slopes = jnp.asarray(2.0 ** (-8.0 * np.arange(1, N_HEADS + 1) / N_HEADS), dtype=F32)
    qi = np.arange(blk)[:, None] + blk
    ki = np.arange(2 * blk)[None, :]
    delta = qi - ki
    in_band = (delta >= 0) & (delta <= blk)
    tables = []
    for dil in DILATIONS:
        bias = -slopes[:, None, None] * jnp.asarray(delta * dil, dtype=F32)
        later = jnp.where(in_band[None], bias, MASK_VALUE)
        first = jnp.where((in_band & (ki >= blk))[None], bias, MASK_VALUE)
        tables.append(jnp.stack([later, first], axis=1))
    t = jnp.stack(tables, axis=1)
    return t.reshape(N_HEADS // 2, 2, len(DILATIONS), 2, blk, 2 * blk)


def _attn_body(q_ref, k_ref, v_ref, bias_ref, o_ref, acc_o, acc_m, s_buf, p_buf):
    blk = ATTN_BLOCK
    s_len = q_ref.shape[1]
    lane = lax.broadcasted_iota(jnp.int32, (blk, 2 * HEAD_DIM), 1)
    left = lane < HEAD_DIM
    left2 = lax.broadcasted_iota(jnp.int32, (2 * blk, 2 * HEAD_DIM), 1) < HEAD_DIM
    trans_b = (((1,), (1,)), ((), ()))
    n_branch = len(DILATIONS)
    n_blocks = s_len // blk

    for br, dil in enumerate(DILATIONS):
        nb = s_len // (blk * dil)

        def starts(j, dil=dil, nb=nb):
            r = j // nb
            n = j % nb
            qs = n * (blk * dil) + r
            ks = jnp.maximum(n - 1, 0) * (blk * dil) + r
            return qs, ks, jnp.where(n == 0, 1, 0)

        def span(start, dil=dil):
            return pl.ds(start, blk) if dil == 1 else pl.ds(start, blk, stride=dil)

        def both(ref, qs, ks):
            return jnp.concatenate([ref[0, span(ks), :], ref[0, span(qs), :]], axis=0).astype(BF16)

        def scores(j, slot, br=br):
            qs, ks, first = starts(j)
            q = q_ref[0, span(qs), :].astype(BF16)
            kk = both(k_ref, qs, ks)
            zero = jnp.zeros_like(q)
            q2 = jnp.concatenate([jnp.where(left, q, zero), jnp.where(left, zero, q)], axis=0)
            s = lax.dot_general(q2, kk, trans_b, preferred_element_type=F32)
            s_buf[slot] = s.reshape(2, blk, 2 * blk) + bias_ref[0, :, br, first]

        def softmax(j, slot, br=br):
            qs, _, _ = starts(j)
            maxes = []
            for hh in range(2):
                s = s_buf[slot, hh]
                m = jnp.max(s, axis=-1, keepdims=True)
                p_buf[slot, hh] = jnp.exp(s - m).astype(BF16)
                maxes.append(m)
            acc_m[br, span(qs), :] = jnp.where(left, maxes[0], maxes[1])

        def values(j, slot, br=br):
            qs, ks, _ = starts(j)
            vv = both(v_ref, qs, ks)
            rhs = jnp.concatenate([vv, jnp.ones_like(vv)], axis=1)
            o = jnp.dot(p_buf[slot].reshape(2 * blk, 2 * blk), rhs, preferred_element_type=F32)
            pw = 2 * HEAD_DIM
            acc_o[2 * br, span(qs), :] = jnp.where(left, o[0:blk, 0:pw], o[0:blk, pw:])
            acc_o[2 * br + 1, span(qs), :] = jnp.where(left, o[blk:, pw:], o[blk:, 0:pw])

        nslot = ATTN_UNROLL
        scores(0, 0)
        softmax(0, 0)
        scores(1, 1)

        def steady(i, carry):
            for u in range(ATTN_UNROLL):
                j = 2 + ATTN_UNROLL * i + u
                values(j - 2, u % nslot)
                softmax(j - 1, (u + 1) % nslot)
                scores(j, (u + 2) % nslot)
            return carry

        trips = (n_blocks - 2) // ATTN_UNROLL
        lax.fori_loop(0, trips, steady, 0)
        for j in range(2 + ATTN_UNROLL * trips, n_blocks):
            values(j - 2, (j - 2) % nslot)
            softmax(j - 1, (j - 1) % nslot)
            scores(j, j % nslot)
        values(n_blocks - 2, (n_blocks - 2) % nslot)
        softmax(n_blocks - 1, (n_blocks - 1) % nslot)
        values(n_blocks - 1, (n_blocks - 1) % nslot)

    rows_per_merge = 256

    def merge(i, carry):
        sl = pl.ds(pl.multiple_of(i * rows_per_merge, rows_per_merge), rows_per_merge)
        left_m = lax.broadcasted_iota(jnp.int32, (rows_per_merge, 2 * HEAD_DIM), 1) < HEAD_DIM
        ms = [acc_m[br, sl, :] for br in range(n_branch)]
        m_max = jnp.maximum(jnp.maximum(ms[0], ms[1]), ms[2])
        tot = [0.0, 0.0]
        for br in range(n_branch):
            a = jnp.exp(ms[br] - m_max)
            a_swapped = pltpu.roll(a, HEAD_DIM, 1)
            tot[0] = tot[0] + jnp.where(left_m, a, a_swapped) * acc_o[2 * br, sl, :]
            tot[1] = tot[1] + jnp.where(left_m, a_swapped, a) * acc_o[2 * br + 1, sl, :]
        out = jnp.where(left_m, tot[0] / pltpu.roll(tot[0], HEAD_DIM, 1),
                        tot[1] / pltpu.roll(tot[1], HEAD_DIM, 1))
        o_ref[0, sl, :] = out.astype(o_ref.dtype)
        return carry

    lax.fori_loop(0, s_len // rows_per_merge, merge, 0, unroll=2)


def _dilated_attention(q, k, v, bias):
    b, s, da = q.shape
    pw = 2 * HEAD_DIM
    seq = pl.BlockSpec((1, s, pw), lambda i, j: (i, 0, j))
    nbr = len(DILATIONS)
    return pl.pallas_call(
        _attn_body,
        grid=(b, da // pw),
        in_specs=[seq, seq, seq,
                  pl.BlockSpec((1,) + bias.shape[1:], lambda i, j: (j, 0, 0, 0, 0, 0))],
        out_specs=seq,
        out_shape=jax.ShapeDtypeStruct((b, s, da), BF16),
        scratch_shapes=[pltpu.VMEM((2 * nbr, s, pw), F32), pltpu.VMEM((nbr, s, pw), F32),
                        pltpu.VMEM((ATTN_UNROLL, 2, ATTN_BLOCK, 2 * ATTN_BLOCK), F32),
                        pltpu.VMEM((ATTN_UNROLL, 2, ATTN_BLOCK, 2 * ATTN_BLOCK), BF16)],
        compiler_params=_cparams(("arbitrary", "arbitrary")),
        name="dilated_attention",
    )(q, k, v, bias)


def _conv_body(uc_ref, uh_ref, w_ref, cb_ref, lg_ref, lb_ref, o_ref, win, shifted):
    t = pl.program_id(1)
    tc = uc_ref.shape[1]
    win[0:CONV_HALO, :] = jnp.where(t > 0, uh_ref[0], 0.0)
    win[CONV_HALO:, :] = uc_ref[0]
    off = CONV_HALO - (CONV_WIDTH - 1)
    for b in range(SUBLANES):
        n_rows = tc + SUBLANES * ((CONV_WIDTH - 1 - b) // SUBLANES)
        shifted[b, 0:n_rows, :] = win[off + b:off + b + n_rows, :]
    for rb in range(tc // CONV_ROWS):
        r0 = rb * CONV_ROWS
        acc = jnp.zeros((CONV_ROWS, uc_ref.shape[2]), F32)
        for j in range(CONV_WIDTH):
            a, b = divmod(j, SUBLANES)
            rows = pl.ds(r0 + SUBLANES * a, CONV_ROWS)
            acc = acc + w_ref[j:j + 1, :] * shifted[b, rows, :]
        y = acc + cb_ref[...]
        mu = jnp.mean(y, axis=-1, keepdims=True)
        yc = y - mu
        var = jnp.mean(yc * yc, axis=-1, keepdims=True)
        z = yc * lax.rsqrt(var + NORM_EPS) * lg_ref[...] + lb_ref[...]
        o_ref[0, r0:r0 + CONV_ROWS, :] = (z * jax.nn.sigmoid(z)).astype(o_ref.dtype)


def _conformer_conv(u, conv_w, conv_b, ln_g, ln_b):
    b, s, dc = u.shape
    tc = SEQ_TILE
    per = tc // CONV_HALO
    row = pl.BlockSpec((1, dc), lambda i, j: (0, 0))
    return pl.pallas_call(
        _conv_body,
        grid=(b, s // tc),
        in_specs=[pl.BlockSpec((1, tc, dc), lambda i, j: (i, j, 0)),
                  pl.BlockSpec((1, CONV_HALO, dc), lambda i, j: (i, jnp.maximum(j * per - 1, 0), 0)),
                  pl.BlockSpec((CONV_WIDTH, dc), lambda i, j: (0, 0)),
                  row, row, row],
        out_specs=pl.BlockSpec((1, tc, dc), lambda i, j: (i, j, 0)),
        out_shape=jax.ShapeDtypeStruct((b, s, dc), BF16),
        scratch_shapes=[pltpu.VMEM((tc + CONV_HALO, dc), F32),
                        pltpu.VMEM((SUBLANES, tc + CONV_HALO - SUBLANES, dc), F32)],
        compiler_params=_cparams(("arbitrary", "arbitrary")),
        name="conformer_conv",
    )(u, u, conv_w, conv_b.reshape(1, dc), ln_g.reshape(1, dc), ln_b.reshape(1, dc))


def _route(logits, seen):
    rows = logits.shape[0]
    lane = lax.broadcasted_iota(jnp.int32, logits.shape, 1)
    big = jnp.int32(1 << 20)
    is_g = lane < N_GROUPS
    gl = jnp.where(is_g, logits, MASK_VALUE)
    gmax = jnp.max(gl, axis=-1, keepdims=True)
    grp = jnp.min(jnp.where(gl == gmax, lane, big), axis=-1, keepdims=True)
    gsum = jnp.sum(jnp.where(is_g, jnp.exp(gl - gmax), 0.0), axis=-1, keepdims=True)
    p_grp = 1.0 / gsum
    lo = N_GROUPS + grp * EXPERTS_PER_GROUP
    el = jnp.where((lane >= lo) & (lane < lo + EXPERTS_PER_GROUP), logits, MASK_VALUE)
    v1 = jnp.max(el, axis=-1, keepdims=True)
    i1 = jnp.min(jnp.where(el == v1, lane, big), axis=-1, keepdims=True)
    el2 = jnp.where(lane == i1, MASK_VALUE, el)
    v2 = jnp.max(el2, axis=-1, keepdims=True)
    i2 = jnp.min(jnp.where(el2 == v2, lane, big), axis=-1, keepdims=True)
    e = jnp.exp(v2 - v1)
    w1 = p_grp / (1.0 + e)
    w2 = p_grp * e / (1.0 + e)

    hit1 = lane == i1
    hit2 = lane == i2
    chosen = jnp.where(hit1 | hit2, 1.0, 0.0)
    ri = lax.broadcasted_iota(jnp.int32, (rows, rows), 0)
    ci = lax.broadcasted_iota(jnp.int32, (rows, rows), 1)
    earlier = jnp.where(ri > ci, 1.0, 0.0).astype(BF16)
    before = jnp.dot(earlier, chosen.astype(BF16), preferred_element_type=F32) + seen
    rank1 = jnp.sum(jnp.where(hit1, before, 0.0), axis=-1, keepdims=True)
    rank2 = jnp.sum(jnp.where(hit2, before, 0.0), axis=-1, keepdims=True)
    seen = seen + jnp.sum(chosen, axis=0, keepdims=True)

    rec = jnp.zeros(logits.shape, F32)
    for pos, val in ((REC_LANE0, i1.astype(F32)), (REC_LANE1, i2.astype(F32)), (REC_W0, w1),
                     (REC_W1, w2), (REC_RANK0, rank1), (REC_RANK1, rank2)):
        rec = jnp.where(lane == pos, val, rec)
    return rec, seen


def _split_bf16(x):
    hi = x.astype(BF16)
    lo = (x - hi.astype(F32)).astype(BF16)
    return hi, lo


def _outproj_body(a_ref, c_ref, x_ref, mod_ref, g_ref, w_ref, wrh_ref, wrl_ref, br_ref,
                  xo_ref, h_ref, r_ref, cnt_ref, seen):
    @pl.when((pl.program_id(0) == 0) & (pl.program_id(1) == 0))
    def _():
        seen[...] = jnp.zeros_like(seen)

    da = a_ref.shape[2]
    mix = jnp.dot(a_ref[0], w_ref[0, 0:da, :], preferred_element_type=F32)
    mix = mix + jnp.dot(c_ref[0], w_ref[0, da:, :], preferred_element_type=F32)
    xn = x_ref[0] + mod_ref[0, 2:3, :] * mix
    xo_ref[0] = xn
    h = _rms_modulate(xn, g_ref[...], mod_ref[0, 3:4, :], mod_ref[0, 4:5, :])
    _store_token_tiles(h_ref, h)
    h_hi, h_lo = _split_bf16(h)
    logits = (jnp.dot(h_hi, wrh_ref[0], preferred_element_type=F32)
              + jnp.dot(h_lo, wrh_ref[0], preferred_element_type=F32)
              + jnp.dot(h_hi, wrl_ref[0], preferred_element_type=F32)) + br_ref[0]
    rec, new_seen = _route(logits, seen[...])
    r_ref[0] = rec
    seen[...] = new_seen
    cnt_ref[...] = new_seen


def _output_projection(attn, conv, x, mod, g, w_out_bf, wr_hi, wr_lo, b_router, layer):
    b, s, d = x.shape
    ts = SEQ_TILE
    tile = lambda n: pl.BlockSpec((1, ts, n), lambda i, j: (i, j, 0))
    const = lambda r, c: pl.BlockSpec((r, c), lambda i, j: (0, 0))
    per_layer = lambda r, c: pl.BlockSpec((1, r, c), lambda i, j: (layer, 0, 0))
    return pl.pallas_call(
        _outproj_body,
        grid=(b, s // ts),
        in_specs=[tile(attn.shape[2]), tile(conv.shape[2]), tile(d),
                  pl.BlockSpec((1, N_MOD, d), lambda i, j: (i, 0, 0)),
                  const(1, d), per_layer(d, d), per_layer(d, ROUTE_LANES),
                  per_layer(d, ROUTE_LANES), per_layer(1, ROUTE_LANES)],
        out_specs=[tile(d), pl.BlockSpec((ts * SUBLANES, LANES), lambda i, j: (i * (s // ts) + j, 0)),
                   tile(ROUTE_LANES), const(1, ROUTE_LANES)],
        out_shape=[jax.ShapeDtypeStruct((b, s, d), F32),
                   jax.ShapeDtypeStruct((b * s * SUBLANES, LANES), F32),
                   jax.ShapeDtypeStruct((b, s, ROUTE_LANES), F32),
                   jax.ShapeDtypeStruct((1, ROUTE_LANES), F32)],
        scratch_shapes=[pltpu.VMEM((1, ROUTE_LANES), F32)],
        compiler_params=_cparams(("arbitrary", "arbitrary")),
        name="output_projection",
    )(attn, conv, x, mod, g.reshape(1, d), w_out_bf, wr_hi, wr_lo, b_router)


def _load_token_tiles(ref):
    n = ref.shape[0] // SUBLANES
    return jnp.concatenate([ref[pl.ds(a, n, stride=SUBLANES), :] for a in range(SUBLANES)], axis=1)


def _store_token_tiles(ref, val):
    n = val.shape[0]
    for a in range(SUBLANES):
        ref[pl.ds(a, n, stride=SUBLANES), :] = val[:, a * LANES:(a + 1) * LANES]


def _tile_of(ref, token):
    return ref.at[pl.ds(pl.multiple_of(token * SUBLANES, SUBLANES), SUBLANES)]


def _row_copy(src, src_token, dst, dst_token, sem):
    return pltpu.make_async_copy(_tile_of(src, src_token), _tile_of(dst, dst_token), sem)


def _wait_rows(like, n_tokens, sem):
    n = n_tokens * SUBLANES
    pltpu.make_async_copy(like.at[pl.ds(0, n)], like.at[pl.ds(0, n)], sem).wait()


def _dest_row(lane_ref, rank_ref, start_ref, a):
    return start_ref[lane_ref[a] - N_GROUPS] + rank_ref[a]


def _dispatch_body(lane_ref, rank_ref, start_ref, h_ref, xb_hbm, sem):
    td = h_ref.shape[0] // SUBLANES
    base = pl.program_id(0) * td

    def issue(t, carry):
        for choice in range(2):
            dst = _dest_row(lane_ref, rank_ref, start_ref, 2 * (base + t) + choice)
            _row_copy(h_ref, t, xb_hbm, dst, sem).start()
        return carry

    lax.fori_loop(0, td, issue, 0, unroll=DMA_UNROLL)
    _wait_rows(xb_hbm, 2 * td, sem)


def _dispatch(h, lanes, ranks, starts, n_rows):
    t = h.shape[0] // SUBLANES
    td = DISPATCH_TILE
    return pl.pallas_call(
        _dispatch_body,
        grid_spec=pltpu.PrefetchScalarGridSpec(
            num_scalar_prefetch=3,
            grid=(t // td,),
            in_specs=[pl.BlockSpec((td * SUBLANES, LANES), lambda i, *_: (i, 0))],
            out_specs=pl.BlockSpec(memory_space=pl.ANY),
            scratch_shapes=[pltpu.SemaphoreType.DMA(())]),
        out_shape=jax.ShapeDtypeStruct((n_rows * SUBLANES, LANES), h.dtype),
        compiler_params=_cparams(("arbitrary",)),
        name="moe_dispatch",
    )(lanes, ranks, starts, h)


def _expert_body(eid_ref, valid_ref, nused_ref, x_ref, wg_ref, wu_ref, wd_ref, o_ref,
                 wg_bf, wu_bf, wd_bf):
    i = pl.program_id(0)

    @pl.when(i < nused_ref[0])
    def _():
        changed = (i == 0) | (eid_ref[i] != eid_ref[jnp.maximum(i - 1, 0)])

        @pl.when(changed)
        def _():
            wg_bf[...] = wg_ref[0, 0].astype(BF16)
            wu_bf[...] = wu_ref[0, 0].astype(BF16)
            wd_bf[...] = wd_ref[0, 0].astype(BF16)

        x = _load_token_tiles(x_ref)
        row = lax.broadcasted_iota(jnp.int32, x.shape, 0)
        x = jnp.where(row < valid_ref[i], x, 0.0).astype(BF16)
        g = jnp.dot(x, wg_bf[...], preferred_element_type=F32)
        u = jnp.dot(x, wu_bf[...], preferred_element_type=F32)
        h = (g * jax.nn.sigmoid(g)) * u
        _store_token_tiles(o_ref, jnp.dot(h.astype(BF16), wd_bf[...], preferred_element_type=F32))

    @pl.when(i >= nused_ref[0])
    def _():
        o_ref[...] = jnp.zeros_like(o_ref)


def _experts(xb, blk_eid, blk_valid, n_used, w_gate, w_up, w_down, layer):
    p = xb.shape[0] // SUBLANES
    d, de = w_gate.shape[2:]
    tm = MOE_BLOCK
    last = lambda i, nu: jnp.minimum(i, nu[0] - 1)
    rows = pl.BlockSpec((tm * SUBLANES, LANES), lambda i, eid, nv, nu: (last(i, nu), 0))
    weight = lambda r, c: pl.BlockSpec(
        (1, 1, r, c), lambda i, eid, nv, nu: (layer, eid[last(i, nu)], 0, 0))
    return pl.pallas_call(
        _expert_body,
        grid_spec=pltpu.PrefetchScalarGridSpec(
            num_scalar_prefetch=3,
            grid=(p // tm,),
            in_specs=[rows, weight(d, de), weight(d, de), weight(de, d)],
            out_specs=pl.BlockSpec((tm * SUBLANES, LANES), lambda i, eid, nv, nu: (i, 0)),
            scratch_shapes=[pltpu.VMEM((d, de), BF16), pltpu.VMEM((d, de), BF16),
                            pltpu.VMEM((de, d), BF16)]),
        out_shape=jax.ShapeDtypeStruct((p * SUBLANES, LANES), F32),
        compiler_params=_cparams(("arbitrary",)),
        name="moe_experts",
    )(blk_eid, blk_valid, n_used, xb, w_gate, w_up, w_down)


def _combine_body(lane_ref, rank_ref, start_ref, x_ref, r_ref, mod_ref, g_ref, yb_hbm, o_ref,
                  y0, y1, sem, *, final):
    tcm = x_ref.shape[0]
    base = pl.program_id(0) * tcm

    def issue(t, carry):
        for choice, buf in enumerate((y0, y1)):
            src = _dest_row(lane_ref, rank_ref, start_ref, 2 * (base + t) + choice)
            _row_copy(yb_hbm, src, buf, t, sem.at[choice]).start()
        return carry

    lax.fori_loop(0, tcm, issue, 0, unroll=DMA_UNROLL)
    _wait_rows(y0, tcm, sem.at[0])
    _wait_rows(y1, tcm, sem.at[1])

    moe = (r_ref[:, REC_W0:REC_W0 + 1] * _load_token_tiles(y0)
           + r_ref[:, REC_W1:REC_W1 + 1] * _load_token_tiles(y1))
    xn = x_ref[...] + mod_ref[0, 5:6, :] * moe
    if final:
        ms = jnp.mean(xn * xn, axis=-1, keepdims=True)
        xn = (xn * lax.rsqrt(ms + NORM_EPS)) * g_ref[...]
    o_ref[...] = xn


def _combine(x, route, mod, g_final, yb, lanes, ranks, starts, seq_len, final):
    t, d = x.shape
    tcm = COMBINE_TILE
    per_seq = seq_len // tcm
    return pl.pallas_call(
        functools.partial(_combine_body, final=final),
        grid_spec=pltpu.PrefetchScalarGridSpec(
            num_scalar_prefetch=3,
            grid=(t // tcm,),
            in_specs=[pl.BlockSpec((tcm, d), lambda i, *_: (i, 0)),
                      pl.BlockSpec((tcm, ROUTE_LANES), lambda i, *_: (i, 0)),
                      pl.BlockSpec((1, N_MOD, d), lambda i, *_: (i // per_seq, 0, 0)),
                      pl.BlockSpec((1, d), lambda i, *_: (0, 0)),
                      pl.BlockSpec(memory_space=pl.ANY)],
            out_specs=pl.BlockSpec((tcm, d), lambda i, *_: (i, 0)),
            scratch_shapes=[pltpu.VMEM((tcm * SUBLANES, LANES), F32),
                            pltpu.VMEM((tcm * SUBLANES, LANES), F32),
                            pltpu.SemaphoreType.DMA((2,))]),
        out_shape=jax.ShapeDtypeStruct((t, d), F32),
        compiler_params=_cparams(("arbitrary",)),
        name="moe_combine",
    )(lanes, ranks, starts, x, route, mod, g_final.reshape(1, d), yb)


def _plan_blocks(counts, n_blocks):
    tm = MOE_BLOCK
    padded = ((counts + tm - 1) // tm) * tm
    pad_end = jnp.cumsum(padded)
    pad_start = pad_end - padded
    blk_row = jnp.arange(n_blocks, dtype=jnp.int32) * tm
    blk_eid = jnp.minimum(jnp.sum(pad_end[None, :] <= blk_row[:, None], axis=1), N_EXPERTS - 1)
    blk_valid = jnp.clip(pad_start[blk_eid] + counts[blk_eid] - blk_row, 0, tm)
    n_used = pad_end[-1:] // tm
    i32 = lambda a: a.astype(jnp.int32)
    return i32(pad_start), i32(blk_eid), i32(blk_valid), i32(n_used)


def kernel(x, c, w_mod, b_mod, g_norm1, w_in, conv_w, conv_b, conv_ln_g, conv_ln_b, w_out, g_norm2, w_router_group, b_router_group, w_router_expert, b_router_expert, w_exp_gate, w_exp_up, w_exp_down, g_final):
    b, s, d = x.shape
    depth = w_mod.shape[0]
    t = b * s
    assert d == SUBLANES * LANES
    assert s % (ATTN_BLOCK * max(DILATIONS)) == 0 and s % SEQ_TILE == 0
    assert t % DISPATCH_TILE == 0 and s % COMBINE_TILE == 0
    n_blocks = -(-2 * t // MOE_BLOCK) + N_EXPERTS
    n_rows = n_blocks * MOE_BLOCK

    mod_all = _modulation(c, w_mod, b_mod).reshape(depth, b, N_MOD, d)
    bias = _attn_bias_table()
    pad = ROUTE_LANES - N_GROUPS - N_EXPERTS
    w_router = jnp.concatenate(
        [w_router_group, w_router_expert, jnp.zeros((depth, d, pad), F32)], axis=-1)
    b_router = jnp.concatenate(
        [b_router_group, b_router_expert, jnp.zeros((depth, pad), F32)], axis=-1)
    b_router = b_router.reshape(depth, 1, ROUTE_LANES)
    wr_hi = w_router.astype(BF16)
    wr_lo = (w_router - wr_hi.astype(F32)).astype(BF16)
    w_in_bf = w_in.astype(BF16)
    w_out_bf = w_out.astype(BF16)

    for l in range(depth):
        mod = mod_all[l]
        q, k, v, u = _input_projection(x, mod, g_norm1[l], w_in_bf, l)
        attn = _dilated_attention(q, k, v, bias)
        conv = _conformer_conv(u, conv_w[l], conv_b[l], conv_ln_g[l], conv_ln_b[l])
        x, h2, route, seen = _output_projection(attn, conv, x, mod, g_norm2[l], w_out_bf,
                                                wr_hi, wr_lo, b_router, l)
        route = route.reshape(t, ROUTE_LANES)
        lanes = route[:, REC_LANE0:REC_LANE1 + 1].astype(jnp.int32).reshape(-1)
        ranks = route[:, REC_RANK0:REC_RANK1 + 1].astype(jnp.int32).reshape(-1)
        counts = seen[0, N_GROUPS:N_GROUPS + N_EXPERTS].astype(jnp.int32)
        starts, blk_eid, blk_valid, n_used = _plan_blocks(counts, n_blocks)
        xb = _dispatch(h2, lanes, ranks, starts, n_rows)
        yb = _experts(xb, blk_eid, blk_valid, n_used, w_exp_gate, w_exp_up, w_exp_down, l)
        x = _combine(x.reshape(t, d), route, mod, g_final, yb, lanes, ranks, starts, s,
                     final=(l == depth - 1)).reshape(b, s, d)
    return x
```

```python
import functools

import numpy as np
import jax
import jax.numpy as jnp
from jax import lax
from jax.experimental import pallas as pl
from jax.experimental.pallas import tpu as pltpu

F32 = jnp.float32
BF16 = jnp.bfloat16

SUBLANES = 8
LANES = 128
HEAD_DIM = 64
N_HEADS = 8
D_ATTN = N_HEADS * HEAD_DIM
CONV_WIDTH = 31
DILATIONS = (1, 4, 16)
ATTN_BLOCK = 128
ATTN_UNROLL = 6
DEINT = 4
N_GROUPS = 4
EXPERTS_PER_GROUP = 8
N_EXPERTS = N_GROUPS * EXPERTS_PER_GROUP
NORM_EPS = 1e-6
N_MOD = 6
MASK_VALUE = -1e30
ROUTE_LANES = 128
REC_LANE0, REC_LANE1, REC_W0, REC_W1, REC_RANK0, REC_RANK1 = range(6)

SEQ_TILE = 512
CONV_HALO = 32
CONV_ROWS = 64
MOE_BLOCK = 512
DISPATCH_TILE = 512
COMBINE_TILE = 512
DMA_UNROLL = 8
VMEM_LIMIT = 56 * 1024 * 1024


def _cparams(sem):
    return pltpu.CompilerParams(dimension_semantics=sem, vmem_limit_bytes=VMEM_LIMIT)


def _mod_body(c_ref, w_ref, b_ref, o_ref):
    c = c_ref[...]
    c_act = c * jax.nn.sigmoid(c)
    o_ref[0] = jnp.dot(c_act, w_ref[0], preferred_element_type=F32,
                       precision=lax.Precision.HIGHEST) + b_ref[0]


def _modulation(c, w_mod, b_mod):
    depth, d, dm = w_mod.shape
    b = c.shape[0]
    nt = dm // d
    return pl.pallas_call(
        _mod_body,
        grid=(depth, nt),
        in_specs=[pl.BlockSpec((b, d), lambda l, j: (0, 0)),
                  pl.BlockSpec((1, d, d), lambda l, j: (l, 0, j)),
                  pl.BlockSpec((1, 1, d), lambda l, j: (l, 0, j))],
        out_specs=pl.BlockSpec((1, b, d), lambda l, j: (l, 0, j)),
        out_shape=jax.ShapeDtypeStruct((depth, b, dm), F32),
        compiler_params=_cparams(("arbitrary", "arbitrary")),
        name="modulation",
    )(c, w_mod, b_mod.reshape(depth, 1, dm))


def _rms_modulate(x, g, shift, scale):
    ms = jnp.mean(x * x, axis=-1, keepdims=True)
    return (x * lax.rsqrt(ms + NORM_EPS)) * g * (1.0 + scale) + shift


def _inproj_body(x_ref, mod_ref, g_ref, w_ref, q_ref, k_ref, v_ref, u_ref):
    h = _rms_modulate(x_ref[0], g_ref[...], mod_ref[0, 0:1, :], mod_ref[0, 1:2, :])
    y = jnp.dot(h.astype(BF16), w_ref[0], preferred_element_type=F32)
    da = D_ATTN
    q_ref[0] = y[:, 0:da] * (HEAD_DIM ** -0.5)
    k_ref[0] = y[:, da:2 * da]
    v_ref[0] = y[:, 2 * da:3 * da]
    dc = (y.shape[1] - 3 * da) // 2
    a = y[:, 3 * da:3 * da + dc]
    b = y[:, 3 * da + dc:]
    u_ref[0] = a * jax.nn.sigmoid(b)


def _input_projection(x, mod, g, w_in_bf, layer):
    b, s, d = x.shape
    din = w_in_bf.shape[2]
    dc = (din - 3 * D_ATTN) // 2
    ts = SEQ_TILE
    tile = lambda n: pl.BlockSpec((1, ts, n), lambda i, j: (i, j, 0))
    return pl.pallas_call(
        _inproj_body,
        grid=(b, s // ts),
        in_specs=[tile(d),
                  pl.BlockSpec((1, N_MOD, d), lambda i, j: (i, 0, 0)),
                  pl.BlockSpec((1, d), lambda i, j: (0, 0)),
                  pl.BlockSpec((1, d, din), lambda i, j: (layer, 0, 0))],
        out_specs=[tile(D_ATTN), tile(D_ATTN), tile(D_ATTN), tile(dc)],
        out_shape=[jax.ShapeDtypeStruct((b, s, D_ATTN), F32)] * 3
        + [jax.ShapeDtypeStruct((b, s, dc), F32)],
        compiler_params=_cparams(("arbitrary", "arbitrary")),
        name="input_projection",
    )(x, mod, g.reshape(1, d), w_in_bf)


def _attn_bias_table():
    blk = ATTN_BLOCK
    slopes = jnp.asarray(2.0 ** (-8.0 * np.arange(1, N_HEADS + 1) / N_HEADS), dtype=F32)
    qi = np.arange(blk)[:, None] + blk
    ki = np.arange(2 * blk)[None, :]
    delta = qi - ki
    in_band = (delta >= 0) & (delta <= blk)
    tables = []
    for dil in DILATIONS:
        bias = -slopes[:, None, None] * jnp.asarray(delta * dil, dtype=F32)
        later = jnp.where(in_band[None], bias, MASK_VALUE)
        first = jnp.where((in_band & (ki >= blk))[None], bias, MASK_VALUE)
        tables.append(jnp.stack([later, first], axis=1))
    t = jnp.stack(tables, axis=1)
    return t.reshape(N_HEADS // 2, 2, len(DILATIONS), 2, blk, 2 * blk)


def _attn_body(q_ref, k_ref, v_ref, bias_ref, o_ref, x4, acc_o, acc_l, acc_m, s_buf, p_buf):
    blk = ATTN_BLOCK
    pw = 2 * HEAD_DIM
    s_len = q_ref.shape[1]
    seg = s_len // DEINT
    left = lax.broadcasted_iota(jnp.int32, (blk, pw), 1) < HEAD_DIM
    trans_b = (((1,), (1,)), ((), ()))
    n_branch = len(DILATIONS)
    n_blocks = s_len // blk
    natural = (q_ref.at[0], k_ref.at[0], v_ref.at[0])
    deint = (x4.at[0], x4.at[1], x4.at[2])

    for t, ref in enumerate(natural):
        for r in range(DEINT):
            x4[t, r * seg:(r + 1) * seg, :] = ref[pl.ds(r, seg, stride=DEINT), :]

    for br, dil in enumerate(DILATIONS):
        nb = s_len // (blk * dil)
        step = dil // DEINT if dil > DEINT else 1
        srcs = natural if dil == 1 else deint

        def starts(j, dil=dil, nb=nb, step=step):
            r = j // nb
            n = j % nb
            if dil == 1:
                base = 0
            else:
                base = (r % DEINT) * seg + r // DEINT
            qs = base + n * (blk * step)
            ks = base + jnp.maximum(n - 1, 0) * (blk * step)
            return qs, ks, jnp.where(n == 0, 1, 0)

        def span(start, step=step):
            return pl.ds(start, blk) if step == 1 else pl.ds(start, blk, stride=step)

        def both(ref, qs, ks):
            return jnp.concatenate([ref[span(ks), :], ref[span(qs), :]], axis=0).astype(BF16)

        def scores(j, slot, br=br, srcs=srcs):
            qs, ks, first = starts(j)
            q = srcs[0][span(qs), :].astype(BF16)
            kk = both(srcs[1], qs, ks)
            zero = jnp.zeros_like(q)
            q2 = jnp.concatenate([jnp.where(left, q, zero), jnp.where(left, zero, q)], axis=0)
            s = lax.dot_general(q2, kk, trans_b, preferred_element_type=F32)
            s_buf[slot] = s.reshape(2, blk, 2 * blk) + bias_ref[0, :, br, first]

        def softmax(j, slot, br=br):
            qs, _, _ = starts(j)
            maxes = []
            for hh in range(2):
                s = s_buf[slot, hh]
                m = jnp.max(s, axis=-1, keepdims=True)
                p_buf[slot, hh] = jnp.exp(s - m).astype(BF16)
                maxes.append(m)
            acc_m[br, span(qs), :] = jnp.where(left, maxes[0], maxes[1])

        def values(j, slot, br=br, srcs=srcs):
            qs, ks, _ = starts(j)
            vv = both(srcs[2], qs, ks)
            rhs = jnp.concatenate([vv, jnp.ones_like(vv)], axis=1)
            o = jnp.dot(p_buf[slot].reshape(2 * blk, 2 * blk), rhs, preferred_element_type=F32)
            acc_o[br, span(qs), :] = jnp.where(left, o[0:blk, 0:pw], o[blk:, 0:pw])
            acc_l[br, span(qs), :] = jnp.where(left, o[0:blk, pw:], o[blk:, pw:])

        nslot = ATTN_UNROLL
        scores(0, 0)
        softmax(0, 0)
        scores(1, 1)

        def steady(i, carry):
            for u in range(ATTN_UNROLL):
                j = 2 + ATTN_UNROLL * i + u
                values(j - 2, u % nslot)
                softmax(j - 1, (u + 1) % nslot)
                scores(j, (u + 2) % nslot)
            return carry

        trips = (n_blocks - 2) // ATTN_UNROLL
        lax.fori_loop(0, trips, steady, 0)
        for j in range(2 + ATTN_UNROLL * trips, n_blocks):
            values(j - 2, (j - 2) % nslot)
            softmax(j - 1, (j - 1) % nslot)
            scores(j, j % nslot)
        values(n_blocks - 2, (n_blocks - 2) % nslot)
        softmax(n_blocks - 1, (n_blocks - 1) % nslot)
        values(n_blocks - 1, (n_blocks - 1) % nslot)

    rows = 256

    def merge(c, carry):
        idx = pl.multiple_of(c * rows, rows)
        r = idx // seg
        nat = pl.ds(DEINT * (idx - r * seg) + r, rows, stride=DEINT)
        spans = [nat if dil == 1 else pl.ds(idx, rows) for dil in DILATIONS]
        ms = [acc_m[br, spans[br], :] for br in range(n_branch)]
        m_max = functools.reduce(jnp.maximum, ms)
        num = 0.0
        den = 0.0
        for br in range(n_branch):
            a = jnp.exp(ms[br] - m_max)
            num = num + a * acc_o[br, spans[br], :]
            den = den + a * acc_l[br, spans[br], :]
        o_ref[0, nat, :] = num / den
        return carry

    lax.fori_loop(0, s_len // rows, merge, 0, unroll=2)


def _dilated_attention(q, k, v, bias):
    b, s, da = q.shape
    pw = 2 * HEAD_DIM
    assert DILATIONS == (1, DEINT, DEINT * DEINT)
    seq = pl.BlockSpec((1, s, pw), lambda i, j: (i, 0, j))
    nbr = len(DILATIONS)
    stage = (ATTN_UNROLL, 2, ATTN_BLOCK, 2 * ATTN_BLOCK)
    return pl.pallas_call(
        _attn_body,
        grid=(b, da // pw),
        in_specs=[seq, seq, seq,
                  pl.BlockSpec((1,) + bias.shape[1:], lambda i, j: (j, 0, 0, 0, 0, 0))],
        out_specs=seq,
        out_shape=jax.ShapeDtypeStruct((b, s, da), F32),
        scratch_shapes=[pltpu.VMEM((3, s, pw), F32)] + [pltpu.VMEM((nbr, s, pw), F32)] * 3
        + [pltpu.VMEM(stage, F32), pltpu.VMEM(stage, BF16)],
        compiler_params=_cparams(("arbitrary", "arbitrary")),
        name="dilated_attention",
    )(q, k, v, bias)


def _conv_body(uc_ref, uh_ref, w_ref, cb_ref, lg_ref, lb_ref, o_ref, win, shifted):
    t = pl.program_id(1)
    tc = uc_ref.shape[1]
    win[0:CONV_HALO, :] = jnp.where(t > 0, uh_ref[0], 0.0)
    win[CONV_HALO:, :] = uc_ref[0]
    off = CONV_HALO - (CONV_WIDTH - 1)
    for b in range(SUBLANES):
        n_rows = tc + SUBLANES * ((CONV_WIDTH - 1 - b) // SUBLANES)
        shifted[b, 0:n_rows, :] = win[off + b:off + b + n_rows, :]
    for rb in range(tc // CONV_ROWS):
        r0 = rb * CONV_ROWS
        acc = jnp.zeros((CONV_ROWS, uc_ref.shape[2]), F32)
        for j in range(CONV_WIDTH):
            a, b = divmod(j, SUBLANES)
            rows = pl.ds(r0 + SUBLANES * a, CONV_ROWS)
            acc = acc + w_ref[j:j + 1, :] * shifted[b, rows, :]
        y = acc + cb_ref[...]
        mu = jnp.mean(y, axis=-1, keepdims=True)
        yc = y - mu
        var = jnp.mean(yc * yc, axis=-1, keepdims=True)
        z = yc * lax.rsqrt(var + NORM_EPS) * lg_ref[...] + lb_ref[...]
        o_ref[0, r0:r0 + CONV_ROWS, :] = (z * jax.nn.sigmoid(z)).astype(o_ref.dtype)


def _conformer_conv(u, conv_w, conv_b, ln_g, ln_b):
    b, s, dc = u.shape
    tc = SEQ_TILE
    per = tc // CONV_HALO
    row = pl.BlockSpec((1, dc), lambda i, j: (0, 0))
    return pl.pallas_call(
        _conv_body,
        grid=(b, s // tc),
        in_specs=[pl.BlockSpec((1, tc, dc), lambda i, j: (i, j, 0)),
                  pl.BlockSpec((1, CONV_HALO, dc), lambda i, j: (i, jnp.maximum(j * per - 1, 0), 0)),
                  pl.BlockSpec((CONV_WIDTH, dc), lambda i, j: (0, 0)),
                  row, row, row],
        out_specs=pl.BlockSpec((1, tc, dc), lambda i, j: (i, j, 0)),
        out_shape=jax.ShapeDtypeStruct((b, s, dc), BF16),
        scratch_shapes=[pltpu.VMEM((tc + CONV_HALO, dc), F32),
                        pltpu.VMEM((SUBLANES, tc + CONV_HALO - SUBLANES, dc), F32)],
        compiler_params=_cparams(("arbitrary", "arbitrary")),
        name="conformer_conv",
    )(u, u, conv_w, conv_b.reshape(1, dc), ln_g.reshape(1, dc), ln_b.reshape(1, dc))


def _route(logits, seen):
    rows = logits.shape[0]
    lane = lax.broadcasted_iota(jnp.int32, logits.shape, 1)
    big = jnp.int32(1 << 20)
    is_g = lane < N_GROUPS
    gl = jnp.where(is_g, logits, MASK_VALUE)
    gmax = jnp.max(gl, axis=-1, keepdims=True)
    grp = jnp.min(jnp.where(gl == gmax, lane, big), axis=-1, keepdims=True)
    gsum = jnp.sum(jnp.where(is_g, jnp.exp(gl - gmax), 0.0), axis=-1, keepdims=True)
    p_grp = 1.0 / gsum
    lo = N_GROUPS + grp * EXPERTS_PER_GROUP
    el = jnp.where((lane >= lo) & (lane < lo + EXPERTS_PER_GROUP), logits, MASK_VALUE)
    v1 = jnp.max(el, axis=-1, keepdims=True)
    i1 = jnp.min(jnp.where(el == v1, lane, big), axis=-1, keepdims=True)
    el2 = jnp.where(lane == i1, MASK_VALUE, el)
    v2 = jnp.max(el2, axis=-1, keepdims=True)
    i2 = jnp.min(jnp.where(el2 == v2, lane, big), axis=-1, keepdims=True)
    e = jnp.exp(v2 - v1)
    w1 = p_grp / (1.0 + e)
    w2 = p_grp * e / (1.0 + e)

    hit1 = lane == i1
    hit2 = lane == i2
    chosen = jnp.where(hit1 | hit2, 1.0, 0.0)
    ri = lax.broadcasted_iota(jnp.int32, (rows, rows), 0)
    ci = lax.broadcasted_iota(jnp.int32, (rows, rows), 1)
    earlier = jnp.where(ri > ci, 1.0, 0.0).astype(BF16)
    before = jnp.dot(earlier, chosen.astype(BF16), preferred_element_type=F32) + seen
    rank1 = jnp.sum(jnp.where(hit1, before, 0.0), axis=-1, keepdims=True)
    rank2 = jnp.sum(jnp.where(hit2, before, 0.0), axis=-1, keepdims=True)
    seen = seen + jnp.sum(chosen, axis=0, keepdims=True)

    rec = jnp.zeros(logits.shape, F32)
    for pos, val in ((REC_LANE0, i1.astype(F32)), (REC_LANE1, i2.astype(F32)), (REC_W0, w1),
                     (REC_W1, w2), (REC_RANK0, rank1), (REC_RANK1, rank2)):
        rec = jnp.where(lane == pos, val, rec)
    return rec, seen


def _split_bf16(x):
    hi = x.astype(BF16)
    lo = (x - hi.astype(F32)).astype(BF16)
    return hi, lo


def _outproj_body(a_ref, c_ref, x_ref, mod_ref, g_ref, w_ref, wrh_ref, wrl_ref, br_ref,
                  xo_ref, h_ref, r_ref, cnt_ref, seen):
    @pl.when((pl.program_id(0) == 0) & (pl.program_id(1) == 0))
    def _():
        seen[...] = jnp.zeros_like(seen)

    da = a_ref.shape[2]
    mix = jnp.dot(a_ref[0].astype(BF16), w_ref[0, 0:da, :], preferred_element_type=F32)
    mix = mix + jnp.dot(c_ref[0], w_ref[0, da:, :], preferred_element_type=F32)
    xn = x_ref[0] + mod_ref[0, 2:3, :] * mix
    xo_ref[0] = xn
    h = _rms_modulate(xn, g_ref[...], mod_ref[0, 3:4, :], mod_ref[0, 4:5, :])
    _store_token_tiles(h_ref, h)
    h_hi, h_lo = _split_bf16(h)
    logits = (jnp.dot(h_hi, wrh_ref[0], preferred_element_type=F32)
              + jnp.dot(h_lo, wrh_ref[0], preferred_element_type=F32)
              + jnp.dot(h_hi, wrl_ref[0], preferred_element_type=F32)) + br_ref[0]
    rec, new_seen = _route(logits, seen[...])
    r_ref[0] = rec
    seen[...] = new_seen
    cnt_ref[...] = new_seen


def _output_projection(attn, conv, x, mod, g, w_out_bf, wr_hi, wr_lo, b_router, layer):
    b, s, d = x.shape
    ts = SEQ_TILE
    tile = lambda n: pl.BlockSpec((1, ts, n), lambda i, j: (i, j, 0))
    const = lambda r, c: pl.BlockSpec((r, c), lambda i, j: (0, 0))
    per_layer = lambda r, c: pl.BlockSpec((1, r, c), lambda i, j: (layer, 0, 0))
    return pl.pallas_call(
        _outproj_body,
        grid=(b, s // ts),
        in_specs=[tile(attn.shape[2]), tile(conv.shape[2]), tile(d),
                  pl.BlockSpec((1, N_MOD, d), lambda i, j: (i, 0, 0)),
                  const(1, d), per_layer(d, d), per_layer(d, ROUTE_LANES),
                  per_layer(d, ROUTE_LANES), per_layer(1, ROUTE_LANES)],
        out_specs=[tile(d), pl.BlockSpec((ts * SUBLANES, LANES), lambda i, j: (i * (s // ts) + j, 0)),
                   tile(ROUTE_LANES), const(1, ROUTE_LANES)],
        out_shape=[jax.ShapeDtypeStruct((b, s, d), F32),
                   jax.ShapeDtypeStruct((b * s * SUBLANES, LANES), F32),
                   jax.ShapeDtypeStruct((b, s, ROUTE_LANES), F32),
                   jax.ShapeDtypeStruct((1, ROUTE_LANES), F32)],
        scratch_shapes=[pltpu.VMEM((1, ROUTE_LANES), F32)],
        compiler_params=_cparams(("arbitrary", "arbitrary")),
        name="output_projection",
    )(attn, conv, x, mod, g.reshape(1, d), w_out_bf, wr_hi, wr_lo, b_router)


def _load_token_tiles(ref):
    n = ref.shape[0] // SUBLANES
    return jnp.concatenate([ref[pl.ds(a, n, stride=SUBLANES), :] for a in range(SUBLANES)], axis=1)


def _store_token_tiles(ref, val):
    n = val.shape[0]
    for a in range(SUBLANES):
        ref[pl.ds(a, n, stride=SUBLANES), :] = val[:, a * LANES:(a + 1) * LANES]


def _tile_of(ref, token):
    return ref.at[pl.ds(pl.multiple_of(token * SUBLANES, SUBLANES), SUBLANES)]


def _row_copy(src, src_token, dst, dst_token, sem):
    return pltpu.make_async_copy(_tile_of(src, src_token), _tile_of(dst, dst_token), sem)


def _wait_rows(like, n_tokens, sem):
    n = n_tokens * SUBLANES
    pltpu.make_async_copy(like.at[pl.ds(0, n)], like.at[pl.ds(0, n)], sem).wait()


def _dest_row(lane_ref, rank_ref, start_ref, a):
    return start_ref[lane_ref[a] - N_GROUPS] + rank_ref[a]


def _dispatch_body(lane_ref, rank_ref, start_ref, h_ref, xb_hbm, sem):
    td = h_ref.shape[0] // SUBLANES
    base = pl.program_id(0) * td

    def issue(t, carry):
        for choice in range(2):
            dst = _dest_row(lane_ref, rank_ref, start_ref, 2 * (base + t) + choice)
            _row_copy(h_ref, t, xb_hbm, dst, sem).start()
        return carry

    lax.fori_loop(0, td, issue, 0, unroll=DMA_UNROLL)
    _wait_rows(xb_hbm, 2 * td, sem)


def _dispatch(h, lanes, ranks, starts, n_rows):
    t = h.shape[0] // SUBLANES
    td = DISPATCH_TILE
    return pl.pallas_call(
        _dispatch_body,
        grid_spec=pltpu.PrefetchScalarGridSpec(
            num_scalar_prefetch=3,
            grid=(t // td,),
            in_specs=[pl.BlockSpec((td * SUBLANES, LANES), lambda i, *_: (i, 0))],
            out_specs=pl.BlockSpec(memory_space=pl.ANY),
            scratch_shapes=[pltpu.SemaphoreType.DMA(())]),
        out_shape=jax.ShapeDtypeStruct((n_rows * SUBLANES, LANES), h.dtype),
        compiler_params=_cparams(("arbitrary",)),
        name="moe_dispatch",
    )(lanes, ranks, starts, h)


def _expert_body(eid_ref, valid_ref, nused_ref, x_ref, wg_ref, wu_ref, wd_ref, o_ref,
                 wg_bf, wu_bf, wd_bf):
    i = pl.program_id(0)

    @pl.when(i < nused_ref[0])
    def _():
        changed = (i == 0) | (eid_ref[i] != eid_ref[jnp.maximum(i - 1, 0)])

        @pl.when(changed)
        def _():
            wg_bf[...] = wg_ref[0, 0].astype(BF16)
            wu_bf[...] = wu_ref[0, 0].astype(BF16)
            wd_bf[...] = wd_ref[0, 0].astype(BF16)

        x = _load_token_tiles(x_ref)
        row = lax.broadcasted_iota(jnp.int32, x.shape, 0)
        x = jnp.where(row < valid_ref[i], x, 0.0).astype(BF16)
        g = jnp.dot(x, wg_bf[...], preferred_element_type=F32)
        u = jnp.dot(x, wu_bf[...], preferred_element_type=F32)
        h = (g * jax.nn.sigmoid(g)) * u
        _store_token_tiles(o_ref, jnp.dot(h.astype(BF16), wd_bf[...], preferred_element_type=F32))

    @pl.when(i >= nused_ref[0])
    def _():
        o_ref[...] = jnp.zeros_like(o_ref)


def _experts(xb, blk_eid, blk_valid, n_used, w_gate, w_up, w_down, layer):
    p = xb.shape[0] // SUBLANES
    d, de = w_gate.shape[2:]
    tm = MOE_BLOCK
    last = lambda i, nu: jnp.minimum(i, nu[0] - 1)
    rows = pl.BlockSpec((tm * SUBLANES, LANES), lambda i, eid, nv, nu: (last(i, nu), 0))
    weight = lambda r, c: pl.BlockSpec(
        (1, 1, r, c), lambda i, eid, nv, nu: (layer, eid[last(i, nu)], 0, 0))
    return pl.pallas_call(
        _expert_body,
        grid_spec=pltpu.PrefetchScalarGridSpec(
            num_scalar_prefetch=3,
            grid=(p // tm,),
            in_specs=[rows, weight(d, de), weight(d, de), weight(de, d)],
            out_specs=pl.BlockSpec((tm * SUBLANES, LANES), lambda i, eid, nv, nu: (i, 0)),
            scratch_shapes=[pltpu.VMEM((d, de), BF16), pltpu.VMEM((d, de), BF16),
                            pltpu.VMEM((de, d), BF16)]),
        out_shape=jax.ShapeDtypeStruct((p * SUBLANES, LANES), F32),
        compiler_params=_cparams(("arbitrary",)),
        name="moe_experts",
    )(blk_eid, blk_valid, n_used, xb, w_gate, w_up, w_down)


def _combine_body(lane_ref, rank_ref, start_ref, x_ref, r_ref, mod_ref, g_ref, yb_hbm, o_ref,
                  ybuf, sem, *, final):
    tcm = x_ref.shape[0]
    i = pl.program_id(0)
    slot = i % 2

    def gather(step, into):
        def issue(t, carry):
            for choice in range(2):
                src = _dest_row(lane_ref, rank_ref, start_ref, 2 * (step * tcm + t) + choice)
                _row_copy(yb_hbm, src, ybuf.at[into, choice], t, sem.at[into, choice]).start()
            return carry

        lax.fori_loop(0, tcm, issue, 0, unroll=DMA_UNROLL)

    @pl.when(i == 0)
    def _():
        gather(0, 0)

    @pl.when(i + 1 < pl.num_programs(0))
    def _():
        gather(i + 1, 1 - slot)

    for choice in range(2):
        _wait_rows(ybuf.at[slot, choice], tcm, sem.at[slot, choice])
    moe = (r_ref[:, REC_W0:REC_W0 + 1] * _load_token_tiles(ybuf.at[slot, 0])
           + r_ref[:, REC_W1:REC_W1 + 1] * _load_token_tiles(ybuf.at[slot, 1]))
    xn = x_ref[...] + mod_ref[0, 5:6, :] * moe
    if final:
        ms = jnp.mean(xn * xn, axis=-1, keepdims=True)
        xn = (xn * lax.rsqrt(ms + NORM_EPS)) * g_ref[...]
    o_ref[...] = xn


def _combine(x, route, mod, g_final, yb, lanes, ranks, starts, seq_len, final):
    t, d = x.shape
    tcm = COMBINE_TILE
    per_seq = seq_len // tcm
    return pl.pallas_call(
        functools.partial(_combine_body, final=final),
        grid_spec=pltpu.PrefetchScalarGridSpec(
            num_scalar_prefetch=3,
            grid=(t // tcm,),
            in_specs=[pl.BlockSpec((tcm, d), lambda i, *_: (i, 0)),
                      pl.BlockSpec((tcm, ROUTE_LANES), lambda i, *_: (i, 0)),
                      pl.BlockSpec((1, N_MOD, d), lambda i, *_: (i // per_seq, 0, 0)),
                      pl.BlockSpec((1, d), lambda i, *_: (0, 0)),
                      pl.BlockSpec(memory_space=pl.ANY)],
            out_specs=pl.BlockSpec((tcm, d), lambda i, *_: (i, 0)),
            scratch_shapes=[pltpu.VMEM((2, 2, tcm * SUBLANES, LANES), F32),
                            pltpu.SemaphoreType.DMA((2, 2))]),
        out_shape=jax.ShapeDtypeStruct((t, d), F32),
        compiler_params=_cparams(("arbitrary",)),
        name="moe_combine",
    )(lanes, ranks, starts, x, route, mod, g_final.reshape(1, d), yb)


def _plan_blocks(counts, n_blocks):
    tm = MOE_BLOCK
    padded = ((counts + tm - 1) // tm) * tm
    pad_end = jnp.cumsum(padded)
    pad_start = pad_end - padded
    blk_row = jnp.arange(n_blocks, dtype=jnp.int32) * tm
    blk_eid = jnp.minimum(jnp.sum(pad_end[None, :] <= blk_row[:, None], axis=1), N_EXPERTS - 1)
    blk_valid = jnp.clip(pad_start[blk_eid] + counts[blk_eid] - blk_row, 0, tm)
    n_used = pad_end[-1:] // tm
    i32 = lambda a: a.astype(jnp.int32)
    return i32(pad_start), i32(blk_eid), i32(blk_valid), i32(n_used)


def kernel(x, c, w_mod, b_mod, g_norm1, w_in, conv_w, conv_b, conv_ln_g, conv_ln_b, w_out, g_norm2, w_router_group, b_router_group, w_router_expert, b_router_expert, w_exp_gate, w_exp_up, w_exp_down, g_final):
    b, s, d = x.shape
    depth = w_mod.shape[0]
    t = b * s
    assert d == SUBLANES * LANES
    assert s % (ATTN_BLOCK * max(DILATIONS)) == 0 and s % SEQ_TILE == 0
    assert t % DISPATCH_TILE == 0 and s % COMBINE_TILE == 0
    n_blocks = -(-2 * t // MOE_BLOCK) + N_EXPERTS
    n_rows = n_blocks * MOE_BLOCK

    mod_all = _modulation(c, w_mod, b_mod).reshape(depth, b, N_MOD, d)
    bias = _attn_bias_table()
    pad = ROUTE_LANES - N_GROUPS - N_EXPERTS
    w_router = jnp.concatenate(
        [w_router_group, w_router_expert, jnp.zeros((depth, d, pad), F32)], axis=-1)
    b_router = jnp.concatenate(
        [b_router_group, b_router_expert, jnp.zeros((depth, pad), F32)], axis=-1)
    b_router = b_router.reshape(depth, 1, ROUTE_LANES)
    wr_hi = w_router.astype(BF16)
    wr_lo = (w_router - wr_hi.astype(F32)).astype(BF16)
    w_in_bf = w_in.astype(BF16)
    w_out_bf = w_out.astype(BF16)

    for l in range(depth):
        mod = mod_all[l]
        q, k, v, u = _input_projection(x, mod, g_norm1[l], w_in_bf, l)
        attn = _dilated_attention(q, k, v, bias)
        conv = _conformer_conv(u, conv_w[l], conv_b[l], conv_ln_g[l], conv_ln_b[l])
        x, h2, route, seen = _output_projection(attn, conv, x, mod, g_norm2[l], w_out_bf,
                                                wr_hi, wr_lo, b_router, l)
        route = route.reshape(t, ROUTE_LANES)
        lanes = route[:, REC_LANE0:REC_LANE1 + 1].astype(jnp.int32).reshape(-1)
        ranks = route[:, REC_RANK0:REC_RANK1 + 1].astype(jnp.int32).reshape(-1)
        counts = seen[0, N_GROUPS:N_GROUPS + N_EXPERTS].astype(jnp.int32)
        starts, blk_eid, blk_valid, n_used = _plan_blocks(counts, n_blocks)
        xb = _dispatch(h2, lanes, ranks, starts, n_rows)
        yb = _experts(xb, blk_eid, blk_valid, n_used, w_exp_gate, w_exp_up, w_exp_down, l)
        x = _combine(x.reshape(t, d), route, mod, g_final, yb, lanes, ranks, starts, s,
                     final=(l == depth - 1)).reshape(b, s, d)
    return x
```

```python
import functools

import numpy as np
import jax
import jax.numpy as jnp
from jax import lax
from jax.experimental import pallas as pl
from jax.experimental.pallas import tpu as pltpu

F32 = jnp.float32
BF16 = jnp.bfloat16

SUBLANES = 8
LANES = 128
HEAD_DIM = 64
N_HEADS = 8
D_ATTN = N_HEADS * HEAD_DIM
CONV_WIDTH = 31
DILATIONS = (1, 4, 16)
ATTN_BLOCK = 128
ATTN_UNROLL = 6
DEINT = 4
N_GROUPS = 4
EXPERTS_PER_GROUP = 8
N_EXPERTS = N_GROUPS * EXPERTS_PER_GROUP
NORM_EPS = 1e-6
N_MOD = 6
MASK_VALUE = -1e30
ROUTE_LANES = 128
REC_LANE0, REC_LANE1, REC_W0, REC_W1, REC_RANK0, REC_RANK1 = range(6)

SEQ_TILE = 512
CONV_HALO = 32
CONV_ROWS = 64
MOE_BLOCK = 512
DISPATCH_TILE = 512
COMBINE_TILE = 512
DMA_UNROLL = 8
VMEM_LIMIT = 56 * 1024 * 1024


def _cparams(sem):
    return pltpu.CompilerParams(dimension_semantics=sem, vmem_limit_bytes=VMEM_LIMIT)


def _mod_body(c_ref, w_ref, b_ref, o_ref):
    c = c_ref[...]
    c_act = c * jax.nn.sigmoid(c)
    o_ref[0] = jnp.dot(c_act, w_ref[0], preferred_element_type=F32,
                       precision=lax.Precision.HIGHEST) + b_ref[0]


def _modulation(c, w_mod, b_mod):
    depth, d, dm = w_mod.shape
    b = c.shape[0]
    nt = dm // d
    return pl.pallas_call(
        _mod_body,
        grid=(depth, nt),
        in_specs=[pl.BlockSpec((b, d), lambda l, j: (0, 0)),
                  pl.BlockSpec((1, d, d), lambda l, j: (l, 0, j)),
                  pl.BlockSpec((1, 1, d), lambda l, j: (l, 0, j))],
        out_specs=pl.BlockSpec((1, b, d), lambda l, j: (l, 0, j)),
        out_shape=jax.ShapeDtypeStruct((depth, b, dm), F32),
        compiler_params=_cparams(("arbitrary", "arbitrary")),
        name="modulation",
    )(c, w_mod, b_mod.reshape(depth, 1, dm))


def _rms_modulate(x, g, shift, scale):
    ms = jnp.mean(x * x, axis=-1, keepdims=True)
    return (x * lax.rsqrt(ms + NORM_EPS)) * g * (1.0 + scale) + shift


def _project_in(x, mod_ref, g_ref, w_ref, q_ref, k_ref, v_ref, u_ref):
    h = _rms_modulate(x, g_ref[...], mod_ref[0, 0:1, :], mod_ref[0, 1:2, :])
    y = jnp.dot(h.astype(BF16), w_ref[0], preferred_element_type=F32)
    da = D_ATTN
    q_ref[0] = y[:, 0:da] * (HEAD_DIM ** -0.5)
    k_ref[0] = y[:, da:2 * da]
    v_ref[0] = y[:, 2 * da:3 * da]
    dc = (y.shape[1] - 3 * da) // 2
    a = y[:, 3 * da:3 * da + dc]
    b = y[:, 3 * da + dc:]
    u_ref[0] = a * jax.nn.sigmoid(b)


def _inproj_body(x_ref, mod_ref, g_ref, w_ref, q_ref, k_ref, v_ref, u_ref):
    _project_in(x_ref[0], mod_ref, g_ref, w_ref, q_ref, k_ref, v_ref, u_ref)


def _input_projection(x, mod, g, w_in_bf, layer):
    b, s, d = x.shape
    din = w_in_bf.shape[2]
    dc = (din - 3 * D_ATTN) // 2
    ts = SEQ_TILE
    tile = lambda n: pl.BlockSpec((1, ts, n), lambda i, j: (i, j, 0))
    return pl.pallas_call(
        _inproj_body,
        grid=(b, s // ts),
        in_specs=[tile(d),
                  pl.BlockSpec((1, N_MOD, d), lambda i, j: (i, 0, 0)),
                  pl.BlockSpec((1, d), lambda i, j: (0, 0)),
                  pl.BlockSpec((1, d, din), lambda i, j: (layer, 0, 0))],
        out_specs=[tile(D_ATTN), tile(D_ATTN), tile(D_ATTN), tile(dc)],
        out_shape=[jax.ShapeDtypeStruct((b, s, D_ATTN), F32)] * 3
        + [jax.ShapeDtypeStruct((b, s, dc), F32)],
        compiler_params=_cparams(("arbitrary", "arbitrary")),
        name="input_projection",
    )(x, mod, g.reshape(1, d), w_in_bf)


def _attn_bias_table():
    blk = ATTN_BLOCK
    slopes = jnp.asarray(2.0 ** (-8.0 * np.arange(1, N_HEADS + 1) / N_HEADS), dtype=F32)
    qi = np.arange(blk)[:, None] + blk
    ki = np.arange(2 * blk)[None, :]
    delta = qi - ki
    in_band = (delta >= 0) & (delta <= blk)
    tables = []
    for dil in DILATIONS:
        bias = -slopes[:, None, None] * jnp.asarray(delta * dil, dtype=F32)
        later = jnp.where(in_band[None], bias, MASK_VALUE)
        first = jnp.where((in_band & (ki >= blk))[None], bias, MASK_VALUE)
        tables.append(jnp.stack([later, first], axis=1))
    t = jnp.stack(tables, axis=1)
    return t.reshape(N_HEADS // 2, 2, len(DILATIONS), 2, blk, 2 * blk)


def _attn_body(q_ref, k_ref, v_ref, bias_ref, o_ref, x4, acc_o, acc_l, acc_m, s_buf, p_buf):
    blk = ATTN_BLOCK
    pw = 2 * HEAD_DIM
    s_len = q_ref.shape[1]
    seg = s_len // DEINT
    left = lax.broadcasted_iota(jnp.int32, (blk, pw), 1) < HEAD_DIM
    trans_b = (((1,), (1,)), ((), ()))
    n_branch = len(DILATIONS)
    n_blocks = s_len // blk
    natural = (q_ref.at[0], k_ref.at[0], v_ref.at[0])
    deint = (x4.at[0], x4.at[1], x4.at[2])

    for t, ref in enumerate(natural):
        for r in range(DEINT):
            x4[t, r * seg:(r + 1) * seg, :] = ref[pl.ds(r, seg, stride=DEINT), :]

    for br, dil in enumerate(DILATIONS):
        nb = s_len // (blk * dil)
        step = dil // DEINT if dil > DEINT else 1
        srcs = natural if dil == 1 else deint

        def starts(j, dil=dil, nb=nb, step=step):
            r = j // nb
            n = j % nb
            if dil == 1:
                base = 0
            else:
                base = (r % DEINT) * seg + r // DEINT
            qs = base + n * (blk * step)
            ks = base + jnp.maximum(n - 1, 0) * (blk * step)
            return qs, ks, jnp.where(n == 0, 1, 0)

        def span(start, step=step):
            return pl.ds(start, blk) if step == 1 else pl.ds(start, blk, stride=step)

        def both(ref, qs, ks):
            return jnp.concatenate([ref[span(ks), :], ref[span(qs), :]], axis=0).astype(BF16)

        def scores(j, slot, br=br, srcs=srcs):
            qs, ks, first = starts(j)
            q = srcs[0][span(qs), :].astype(BF16)
            kk = both(srcs[1], qs, ks)
            zero = jnp.zeros_like(q)
            q2 = jnp.concatenate([jnp.where(left, q, zero), jnp.where(left, zero, q)], axis=0)
            s = lax.dot_general(q2, kk, trans_b, preferred_element_type=F32)
            s_buf[slot] = s.reshape(2, blk, 2 * blk) + bias_ref[0, :, br, first]

        def softmax(j, slot, br=br):
            qs, _, _ = starts(j)
            maxes = []
            for hh in range(2):
                s = s_buf[slot, hh]
                m = jnp.max(s, axis=-1, keepdims=True)
                p_buf[slot, hh] = jnp.exp(s - m).astype(BF16)
                maxes.append(m)
            acc_m[br, span(qs), :] = jnp.where(left, maxes[0], maxes[1])

        def values(j, slot, br=br, srcs=srcs):
            qs, ks, _ = starts(j)
            vv = both(srcs[2], qs, ks)
            rhs = jnp.concatenate([vv, jnp.ones_like(vv)], axis=1)
            o = jnp.dot(p_buf[slot].reshape(2 * blk, 2 * blk), rhs, preferred_element_type=F32)
            acc_o[br, span(qs), :] = jnp.where(left, o[0:blk, 0:pw], o[blk:, 0:pw])
            acc_l[br, span(qs), :] = jnp.where(left, o[0:blk, pw:], o[blk:, pw:])

        nslot = ATTN_UNROLL
        scores(0, 0)
        softmax(0, 0)
        scores(1, 1)

        def steady(i, carry):
            for u in range(ATTN_UNROLL):
                j = 2 + ATTN_UNROLL * i + u
                values(j - 2, u % nslot)
                softmax(j - 1, (u + 1) % nslot)
                scores(j, (u + 2) % nslot)
            return carry

        trips = (n_blocks - 2) // ATTN_UNROLL
        lax.fori_loop(0, trips, steady, 0)
        for j in range(2 + ATTN_UNROLL * trips, n_blocks):
            values(j - 2, (j - 2) % nslot)
            softmax(j - 1, (j - 1) % nslot)
            scores(j, j % nslot)
        values(n_blocks - 2, (n_blocks - 2) % nslot)
        softmax(n_blocks - 1, (n_blocks - 1) % nslot)
        values(n_blocks - 1, (n_blocks - 1) % nslot)

    rows = 256

    def merge(c, carry):
        idx = pl.multiple_of(c * rows, rows)
        r = idx // seg
        nat = pl.ds(DEINT * (idx - r * seg) + r, rows, stride=DEINT)
        spans = [nat if dil == 1 else pl.ds(idx, rows) for dil in DILATIONS]
        ms = [acc_m[br, spans[br], :] for br in range(n_branch)]
        m_max = functools.reduce(jnp.maximum, ms)
        num = 0.0
        den = 0.0
        for br in range(n_branch):
            a = jnp.exp(ms[br] - m_max)
            num = num + a * acc_o[br, spans[br], :]
            den = den + a * acc_l[br, spans[br], :]
        o_ref[0, nat, :] = num / den
        return carry

    lax.fori_loop(0, s_len // rows, merge, 0, unroll=2)


def _dilated_attention(q, k, v, bias):
    b, s, da = q.shape
    pw = 2 * HEAD_DIM
    assert DILATIONS == (1, DEINT, DEINT * DEINT)
    seq = pl.BlockSpec((1, s, pw), lambda i, j: (i, 0, j))
    nbr = len(DILATIONS)
    stage = (ATTN_UNROLL, 2, ATTN_BLOCK, 2 * ATTN_BLOCK)
    return pl.pallas_call(
        _attn_body,
        grid=(b, da // pw),
        in_specs=[seq, seq, seq,
                  pl.BlockSpec((1,) + bias.shape[1:], lambda i, j: (j, 0, 0, 0, 0, 0))],
        out_specs=seq,
        out_shape=jax.ShapeDtypeStruct((b, s, da), F32),
        scratch_shapes=[pltpu.VMEM((3, s, pw), F32)] + [pltpu.VMEM((nbr, s, pw), F32)] * 3
        + [pltpu.VMEM(stage, F32), pltpu.VMEM(stage, BF16)],
        compiler_params=_cparams(("arbitrary", "arbitrary")),
        name="dilated_attention",
    )(q, k, v, bias)


def _conv_body(uc_ref, uh_ref, w_ref, cb_ref, lg_ref, lb_ref, o_ref, win, shifted):
    t = pl.program_id(1)
    tc = uc_ref.shape[1]
    win[0:CONV_HALO, :] = jnp.where(t > 0, uh_ref[0], 0.0)
    win[CONV_HALO:, :] = uc_ref[0]
    off = CONV_HALO - (CONV_WIDTH - 1)
    for b in range(SUBLANES):
        n_rows = tc + SUBLANES * ((CONV_WIDTH - 1 - b) // SUBLANES)
        shifted[b, 0:n_rows, :] = win[off + b:off + b + n_rows, :]
    for rb in range(tc // CONV_ROWS):
        r0 = rb * CONV_ROWS
        acc = jnp.zeros((CONV_ROWS, uc_ref.shape[2]), F32)
        for j in range(CONV_WIDTH):
            a, b = divmod(j, SUBLANES)
            rows = pl.ds(r0 + SUBLANES * a, CONV_ROWS)
            acc = acc + w_ref[j:j + 1, :] * shifted[b, rows, :]
        y = acc + cb_ref[...]
        mu = jnp.mean(y, axis=-1, keepdims=True)
        yc = y - mu
        var = jnp.mean(yc * yc, axis=-1, keepdims=True)
        z = yc * lax.rsqrt(var + NORM_EPS) * lg_ref[...] + lb_ref[...]
        o_ref[0, r0:r0 + CONV_ROWS, :] = (z * jax.nn.sigmoid(z)).astype(o_ref.dtype)


def _conformer_conv(u, conv_w, conv_b, ln_g, ln_b):
    b, s, dc = u.shape
    tc = SEQ_TILE
    per = tc // CONV_HALO
    row = pl.BlockSpec((1, dc), lambda i, j: (0, 0))
    return pl.pallas_call(
        _conv_body,
        grid=(b, s // tc),
        in_specs=[pl.BlockSpec((1, tc, dc), lambda i, j: (i, j, 0)),
                  pl.BlockSpec((1, CONV_HALO, dc), lambda i, j: (i, jnp.maximum(j * per - 1, 0), 0)),
                  pl.BlockSpec((CONV_WIDTH, dc), lambda i, j: (0, 0)),
                  row, row, row],
        out_specs=pl.BlockSpec((1, tc, dc), lambda i, j: (i, j, 0)),
        out_shape=jax.ShapeDtypeStruct((b, s, dc), BF16),
        scratch_shapes=[pltpu.VMEM((tc + CONV_HALO, dc), F32),
                        pltpu.VMEM((SUBLANES, tc + CONV_HALO - SUBLANES, dc), F32)],
        compiler_params=_cparams(("arbitrary", "arbitrary")),
        name="conformer_conv",
    )(u, u, conv_w, conv_b.reshape(1, dc), ln_g.reshape(1, dc), ln_b.reshape(1, dc))


def _route(logits, seen):
    rows = logits.shape[0]
    lane = lax.broadcasted_iota(jnp.int32, logits.shape, 1)
    big = jnp.int32(1 << 20)
    is_g = lane < N_GROUPS
    gl = jnp.where(is_g, logits, MASK_VALUE)
    gmax = jnp.max(gl, axis=-1, keepdims=True)
    grp = jnp.min(jnp.where(gl == gmax, lane, big), axis=-1, keepdims=True)
    gsum = jnp.sum(jnp.where(is_g, jnp.exp(gl - gmax), 0.0), axis=-1, keepdims=True)
    p_grp = 1.0 / gsum
    lo = N_GROUPS + grp * EXPERTS_PER_GROUP
    el = jnp.where((lane >= lo) & (lane < lo + EXPERTS_PER_GROUP), logits, MASK_VALUE)
    v1 = jnp.max(el, axis=-1, keepdims=True)
    i1 = jnp.min(jnp.where(el == v1, lane, big), axis=-1, keepdims=True)
    el2 = jnp.where(lane == i1, MASK_VALUE, el)
    v2 = jnp.max(el2, axis=-1, keepdims=True)
    i2 = jnp.min(jnp.where(el2 == v2, lane, big), axis=-1, keepdims=True)
    e = jnp.exp(v2 - v1)
    w1 = p_grp / (1.0 + e)
    w2 = p_grp * e / (1.0 + e)

    hit1 = lane == i1
    hit2 = lane == i2
    chosen = jnp.where(hit1 | hit2, 1.0, 0.0)
    ri = lax.broadcasted_iota(jnp.int32, (rows, rows), 0)
    ci = lax.broadcasted_iota(jnp.int32, (rows, rows), 1)
    earlier = jnp.where(ri > ci, 1.0, 0.0).astype(BF16)
    before = jnp.dot(earlier, chosen.astype(BF16), preferred_element_type=F32) + seen
    rank1 = jnp.sum(jnp.where(hit1, before, 0.0), axis=-1, keepdims=True)
    rank2 = jnp.sum(jnp.where(hit2, before, 0.0), axis=-1, keepdims=True)
    seen = seen + jnp.sum(chosen, axis=0, keepdims=True)

    rec = jnp.zeros(logits.shape, F32)
    for pos, val in ((REC_LANE0, i1.astype(F32)), (REC_LANE1, i2.astype(F32)), (REC_W0, w1),
                     (REC_W1, w2), (REC_RANK0, rank1), (REC_RANK1, rank2)):
        rec = jnp.where(lane == pos, val, rec)
    return rec, seen


def _split_bf16(x):
    hi = x.astype(BF16)
    lo = (x - hi.astype(F32)).astype(BF16)
    return hi, lo


def _outproj_body(a_ref, c_ref, x_ref, mod_ref, g_ref, w_ref, wrh_ref, wrl_ref, br_ref,
                  xo_ref, h_ref, r_ref, cnt_ref, seen):
    @pl.when((pl.program_id(0) == 0) & (pl.program_id(1) == 0))
    def _():
        seen[...] = jnp.zeros_like(seen)

    da = a_ref.shape[2]
    mix = jnp.dot(a_ref[0].astype(BF16), w_ref[0, 0:da, :], preferred_element_type=F32)
    mix = mix + jnp.dot(c_ref[0], w_ref[0, da:, :], preferred_element_type=F32)
    xn = x_ref[0] + mod_ref[0, 2:3, :] * mix
    xo_ref[0] = xn
    h = _rms_modulate(xn, g_ref[...], mod_ref[0, 3:4, :], mod_ref[0, 4:5, :])
    _store_token_tiles(h_ref, h)
    h_hi, h_lo = _split_bf16(h)
    logits = (jnp.dot(h_hi, wrh_ref[0], preferred_element_type=F32)
              + jnp.dot(h_lo, wrh_ref[0], preferred_element_type=F32)
              + jnp.dot(h_hi, wrl_ref[0], preferred_element_type=F32)) + br_ref[0]
    rec, new_seen = _route(logits, seen[...])
    r_ref[0] = rec
    seen[...] = new_seen
    cnt_ref[...] = new_seen


def _output_projection(attn, conv, x, mod, g, w_out_bf, wr_hi, wr_lo, b_router, layer):
    b, s, d = x.shape
    ts = SEQ_TILE
    tile = lambda n: pl.BlockSpec((1, ts, n), lambda i, j: (i, j, 0))
    const = lambda r, c: pl.BlockSpec((r, c), lambda i, j: (0, 0))
    per_layer = lambda r, c: pl.BlockSpec((1, r, c), lambda i, j: (layer, 0, 0))
    return pl.pallas_call(
        _outproj_body,
        grid=(b, s // ts),
        in_specs=[tile(attn.shape[2]), tile(conv.shape[2]), tile(d),
                  pl.BlockSpec((1, N_MOD, d), lambda i, j: (i, 0, 0)),
                  const(1, d), per_layer(d, d), per_layer(d, ROUTE_LANES),
                  per_layer(d, ROUTE_LANES), per_layer(1, ROUTE_LANES)],
        out_specs=[tile(d), pl.BlockSpec((ts * SUBLANES, LANES), lambda i, j: (i * (s // ts) + j, 0)),
                   tile(ROUTE_LANES), const(1, ROUTE_LANES)],
        out_shape=[jax.ShapeDtypeStruct((b, s, d), F32),
                   jax.ShapeDtypeStruct((b * s * SUBLANES, LANES), F32),
                   jax.ShapeDtypeStruct((b, s, ROUTE_LANES), F32),
                   jax.ShapeDtypeStruct((1, ROUTE_LANES), F32)],
        scratch_shapes=[pltpu.VMEM((1, ROUTE_LANES), F32)],
        compiler_params=_cparams(("arbitrary", "arbitrary")),
        name="output_projection",
    )(attn, conv, x, mod, g.reshape(1, d), w_out_bf, wr_hi, wr_lo, b_router)


def _load_token_tiles(ref):
    n = ref.shape[0] // SUBLANES
    return jnp.concatenate([ref[pl.ds(a, n, stride=SUBLANES), :] for a in range(SUBLANES)], axis=1)


def _store_token_tiles(ref, val):
    n = val.shape[0]
    for a in range(SUBLANES):
        ref[pl.ds(a, n, stride=SUBLANES), :] = val[:, a * LANES:(a + 1) * LANES]


def _tile_of(ref, token):
    return ref.at[pl.ds(pl.multiple_of(token * SUBLANES, SUBLANES), SUBLANES)]


def _row_copy(src, src_token, dst, dst_token, sem):
    return pltpu.make_async_copy(_tile_of(src, src_token), _tile_of(dst, dst_token), sem)


def _wait_rows(like, n_tokens, sem):
    n = n_tokens * SUBLANES
    pltpu.make_async_copy(like.at[pl.ds(0, n)], like.at[pl.ds(0, n)], sem).wait()


def _dispatch_body(dest_ref, h_ref, xb_hbm, sem):
    td = h_ref.shape[0] // SUBLANES
    base = pl.program_id(0) * td

    def issue(t, carry):
        for choice in range(2):
            _row_copy(h_ref, t, xb_hbm, dest_ref[2 * (base + t) + choice], sem).start()
        return carry

    lax.fori_loop(0, td, issue, 0, unroll=DMA_UNROLL)
    _wait_rows(xb_hbm, 2 * td, sem)


def _dispatch(h, dest, n_rows):
    t = h.shape[0] // SUBLANES
    td = DISPATCH_TILE
    return pl.pallas_call(
        _dispatch_body,
        grid_spec=pltpu.PrefetchScalarGridSpec(
            num_scalar_prefetch=1,
            grid=(t // td,),
            in_specs=[pl.BlockSpec((td * SUBLANES, LANES), lambda i, *_: (i, 0))],
            out_specs=pl.BlockSpec(memory_space=pl.ANY),
            scratch_shapes=[pltpu.SemaphoreType.DMA(())]),
        out_shape=jax.ShapeDtypeStruct((n_rows * SUBLANES, LANES), h.dtype),
        compiler_params=_cparams(("arbitrary",)),
        name="moe_dispatch",
    )(dest, h)


def _expert_body(eid_ref, valid_ref, nused_ref, x_ref, wg_ref, wu_ref, wd_ref, o_ref,
                 wg_bf, wu_bf, wd_bf):
    i = pl.program_id(0)

    @pl.when(i < nused_ref[0])
    def _():
        changed = (i == 0) | (eid_ref[i] != eid_ref[jnp.maximum(i - 1, 0)])

        @pl.when(changed)
        def _():
            wg_bf[...] = wg_ref[0, 0].astype(BF16)
            wu_bf[...] = wu_ref[0, 0].astype(BF16)
            wd_bf[...] = wd_ref[0, 0].astype(BF16)

        x = _load_token_tiles(x_ref)
        row = lax.broadcasted_iota(jnp.int32, x.shape, 0)
        x = jnp.where(row < valid_ref[i], x, 0.0).astype(BF16)
        g = jnp.dot(x, wg_bf[...], preferred_element_type=F32)
        u = jnp.dot(x, wu_bf[...], preferred_element_type=F32)
        h = (g * jax.nn.sigmoid(g)) * u
        _store_token_tiles(o_ref, jnp.dot(h.astype(BF16), wd_bf[...], preferred_element_type=F32))

    @pl.when(i >= nused_ref[0])
    def _():
        o_ref[...] = jnp.zeros_like(o_ref)


def _experts(xb, blk_eid, blk_valid, n_used, w_gate, w_up, w_down, layer):
    p = xb.shape[0] // SUBLANES
    d, de = w_gate.shape[2:]
    tm = MOE_BLOCK
    last = lambda i, nu: jnp.minimum(i, nu[0] - 1)
    rows = pl.BlockSpec((tm * SUBLANES, LANES), lambda i, eid, nv, nu: (last(i, nu), 0))
    weight = lambda r, c: pl.BlockSpec(
        (1, 1, r, c), lambda i, eid, nv, nu: (layer, eid[last(i, nu)], 0, 0))
    return pl.pallas_call(
        _expert_body,
        grid_spec=pltpu.PrefetchScalarGridSpec(
            num_scalar_prefetch=3,
            grid=(p // tm,),
            in_specs=[rows, weight(d, de), weight(d, de), weight(de, d)],
            out_specs=pl.BlockSpec((tm * SUBLANES, LANES), lambda i, eid, nv, nu: (i, 0)),
            scratch_shapes=[pltpu.VMEM((d, de), BF16), pltpu.VMEM((d, de), BF16),
                            pltpu.VMEM((de, d), BF16)]),
        out_shape=jax.ShapeDtypeStruct((p * SUBLANES, LANES), F32),
        compiler_params=_cparams(("arbitrary",)),
        name="moe_experts",
    )(blk_eid, blk_valid, n_used, xb, w_gate, w_up, w_down)


def _gather_rows(dest_ref, yb_hbm, ybuf, sem, tile, into, n_tokens, unroll):
    def issue(t, carry):
        for choice in range(2):
            src = dest_ref[2 * (tile * n_tokens + t) + choice]
            _row_copy(yb_hbm, src, ybuf.at[into, choice], t, sem.at[into, choice]).start()
        return carry

    lax.fori_loop(0, n_tokens, issue, 0, unroll=unroll)


def _gathered_moe(r_ref, ybuf, sem, slot, n_tokens):
    for choice in range(2):
        _wait_rows(ybuf.at[slot, choice], n_tokens, sem.at[slot, choice])
    return (r_ref[:, REC_W0:REC_W0 + 1] * _load_token_tiles(ybuf.at[slot, 0])
            + r_ref[:, REC_W1:REC_W1 + 1] * _load_token_tiles(ybuf.at[slot, 1]))


def _combine_final_body(dest_ref, x_ref, r_ref, mod_ref, g_ref, yb_hbm, o_ref, ybuf, sem):
    tcm = x_ref.shape[0]
    i = pl.program_id(0)
    slot = i % 2

    @pl.when(i == 0)
    def _():
        _gather_rows(dest_ref, yb_hbm, ybuf, sem, 0, 0, tcm, DMA_UNROLL)

    @pl.when(i + 1 < pl.num_programs(0))
    def _():
        _gather_rows(dest_ref, yb_hbm, ybuf, sem, i + 1, 1 - slot, tcm, DMA_UNROLL)

    xn = x_ref[...] + mod_ref[0, 5:6, :] * _gathered_moe(r_ref, ybuf, sem, slot, tcm)
    ms = jnp.mean(xn * xn, axis=-1, keepdims=True)
    o_ref[...] = (xn * lax.rsqrt(ms + NORM_EPS)) * g_ref[...]


def _combine_final(x, route, mod, g_final, yb, dest, seq_len):
    t, d = x.shape
    tcm = COMBINE_TILE
    per_seq = seq_len // tcm
    return pl.pallas_call(
        _combine_final_body,
        grid_spec=pltpu.PrefetchScalarGridSpec(
            num_scalar_prefetch=1,
            grid=(t // tcm,),
            in_specs=[pl.BlockSpec((tcm, d), lambda i, *_: (i, 0)),
                      pl.BlockSpec((tcm, ROUTE_LANES), lambda i, *_: (i, 0)),
                      pl.BlockSpec((1, N_MOD, d), lambda i, *_: (i // per_seq, 0, 0)),
                      pl.BlockSpec((1, d), lambda i, *_: (0, 0)),
                      pl.BlockSpec(memory_space=pl.ANY)],
            out_specs=pl.BlockSpec((tcm, d), lambda i, *_: (i, 0)),
            scratch_shapes=[pltpu.VMEM((2, 2, tcm * SUBLANES, LANES), F32),
                            pltpu.SemaphoreType.DMA((2, 2))]),
        out_shape=jax.ShapeDtypeStruct((t, d), F32),
        compiler_params=_cparams(("arbitrary",)),
        name="moe_combine_final",
    )(dest, x, route, mod, g_final.reshape(1, d), yb)


def _combine_inproj_body(dest_ref, x_ref, r_ref, modp_ref, mod_ref, g_ref, w_ref, yb_hbm,
                         xo_ref, q_ref, k_ref, v_ref, u_ref, ybuf, sem):
    ts = x_ref.shape[1]
    per_seq = pl.num_programs(1)
    n_tiles = pl.num_programs(0) * per_seq
    i = pl.program_id(0) * per_seq + pl.program_id(1)
    slot = i % 2

    @pl.when(i == 0)
    def _():
        _gather_rows(dest_ref, yb_hbm, ybuf, sem, 0, 0, ts, DMA_UNROLL)

    moe = _gathered_moe(r_ref, ybuf, sem, slot, ts)
    _gather_rows(dest_ref, yb_hbm, ybuf, sem, jnp.minimum(i + 1, n_tiles - 1), 1 - slot, ts, True)
    xn = x_ref[0] + modp_ref[0, 5:6, :] * moe
    xo_ref[0] = xn
    _project_in(xn, mod_ref, g_ref, w_ref, q_ref, k_ref, v_ref, u_ref)

    @pl.when(i == n_tiles - 1)
    def _():
        for choice in range(2):
            _wait_rows(ybuf.at[1 - slot, choice], ts, sem.at[1 - slot, choice])


def _combine_input_projection(x, route, mod_prev, yb, dest, mod, g, w_in_bf, layer):
    b, s, d = x.shape
    din = w_in_bf.shape[2]
    dc = (din - 3 * D_ATTN) // 2
    ts = SEQ_TILE
    per_seq = s // ts
    tile = lambda n: pl.BlockSpec((1, ts, n), lambda i, j, *_: (i, j, 0))
    mods = pl.BlockSpec((1, N_MOD, d), lambda i, j, *_: (i, 0, 0))
    return pl.pallas_call(
        _combine_inproj_body,
        grid_spec=pltpu.PrefetchScalarGridSpec(
            num_scalar_prefetch=1,
            grid=(b, per_seq),
            in_specs=[tile(d),
                      pl.BlockSpec((ts, ROUTE_LANES), lambda i, j, *_: (i * per_seq + j, 0)),
                      mods, mods,
                      pl.BlockSpec((1, d), lambda i, j, *_: (0, 0)),
                      pl.BlockSpec((1, d, din), lambda i, j, *_: (layer, 0, 0)),
                      pl.BlockSpec(memory_space=pl.ANY)],
            out_specs=[tile(d), tile(D_ATTN), tile(D_ATTN), tile(D_ATTN), tile(dc)],
            scratch_shapes=[pltpu.VMEM((2, 2, ts * SUBLANES, LANES), F32),
                            pltpu.SemaphoreType.DMA((2, 2))]),
        out_shape=[jax.ShapeDtypeStruct((b, s, d), F32)]
        + [jax.ShapeDtypeStruct((b, s, D_ATTN), F32)] * 3
        + [jax.ShapeDtypeStruct((b, s, dc), F32)],
        compiler_params=_cparams(("arbitrary", "arbitrary")),
        name="combine_input_projection",
    )(dest, x, route, mod_prev, mod, g.reshape(1, d), w_in_bf, yb)


def _plan_blocks(counts, n_blocks):
    tm = MOE_BLOCK
    padded = ((counts + tm - 1) // tm) * tm
    pad_end = jnp.cumsum(padded)
    pad_start = pad_end - padded
    blk_row = jnp.arange(n_blocks, dtype=jnp.int32) * tm
    blk_eid = jnp.minimum(jnp.sum(pad_end[None, :] <= blk_row[:, None], axis=1), N_EXPERTS - 1)
    blk_valid = jnp.clip(pad_start[blk_eid] + counts[blk_eid] - blk_row, 0, tm)
    n_used = pad_end[-1:] // tm
    i32 = lambda a: a.astype(jnp.int32)
    return i32(pad_start), i32(blk_eid), i32(blk_valid), i32(n_used)


def kernel(x, c, w_mod, b_mod, g_norm1, w_in, conv_w, conv_b, conv_ln_g, conv_ln_b, w_out, g_norm2, w_router_group, b_router_group, w_router_expert, b_router_expert, w_exp_gate, w_exp_up, w_exp_down, g_final):
    b, s, d = x.shape
    depth = w_mod.shape[0]
    t = b * s
    assert d == SUBLANES * LANES
    assert s % (ATTN_BLOCK * max(DILATIONS)) == 0 and s % SEQ_TILE == 0
    assert t % DISPATCH_TILE == 0 and s % COMBINE_TILE == 0
    n_blocks = -(-2 * t // MOE_BLOCK) + N_EXPERTS
    n_rows = n_blocks * MOE_BLOCK

    mod_all = _modulation(c, w_mod, b_mod).reshape(depth, b, N_MOD, d)
    bias = _attn_bias_table()
    pad = ROUTE_LANES - N_GROUPS - N_EXPERTS
    w_router = jnp.concatenate(
        [w_router_group, w_router_expert, jnp.zeros((depth, d, pad), F32)], axis=-1)
    b_router = jnp.concatenate(
        [b_router_group, b_router_expert, jnp.zeros((depth, pad), F32)], axis=-1)
    b_router = b_router.reshape(depth, 1, ROUTE_LANES)
    wr_hi = w_router.astype(BF16)
    wr_lo = (w_router - wr_hi.astype(F32)).astype(BF16)
    w_in_bf = w_in.astype(BF16)
    w_out_bf = w_out.astype(BF16)

    pending = None
    for l in range(depth):
        mod = mod_all[l]
        if pending is None:
            q, k, v, u = _input_projection(x, mod, g_norm1[l], w_in_bf, l)
        else:
            x, q, k, v, u = _combine_input_projection(x, *pending, mod, g_norm1[l], w_in_bf, l)
        attn = _dilated_attention(q, k, v, bias)
        conv = _conformer_conv(u, conv_w[l], conv_b[l], conv_ln_g[l], conv_ln_b[l])
        x, h2, route, seen = _output_projection(attn, conv, x, mod, g_norm2[l], w_out_bf,
                                                wr_hi, wr_lo, b_router, l)
        route = route.reshape(t, ROUTE_LANES)
        counts = seen[0, N_GROUPS:N_GROUPS + N_EXPERTS].astype(jnp.int32)
        starts, blk_eid, blk_valid, n_used = _plan_blocks(counts, n_blocks)
        eid = route[:, REC_LANE0:REC_LANE1 + 1].astype(jnp.int32) - N_GROUPS
        ranks = route[:, REC_RANK0:REC_RANK1 + 1].astype(jnp.int32)
        onehot = eid[:, :, None] == jnp.arange(N_EXPERTS, dtype=jnp.int32)
        dest = (jnp.sum(jnp.where(onehot, starts, 0), axis=-1) + ranks).reshape(-1)
        xb = _dispatch(h2, dest, n_rows)
        yb = _experts(xb, blk_eid, blk_valid, n_used, w_exp_gate, w_exp_up, w_exp_down, l)
        pending = (route, mod, yb, dest)
    route, mod, yb, dest = pending
    return _combine_final(x.reshape(t, d), route, mod, g_final, yb, dest, s).reshape(b, s, d)
```

```python
import functools

import numpy as np
import jax
import jax.numpy as jnp
from jax import lax
from jax.experimental import pallas as pl
from jax.experimental.pallas import tpu as pltpu

F32 = jnp.float32
BF16 = jnp.bfloat16

SUBLANES = 8
LANES = 128
HEAD_DIM = 64
N_HEADS = 8
D_ATTN = N_HEADS * HEAD_DIM
CONV_WIDTH = 31
DILATIONS = (1, 4, 16)
ATTN_BLOCK = 128
ATTN_UNROLL = 10
DEINT = 4
N_GROUPS = 4
EXPERTS_PER_GROUP = 8
N_EXPERTS = N_GROUPS * EXPERTS_PER_GROUP
NORM_EPS = 1e-6
N_MOD = 6
MASK_VALUE = -1e30
ROUTE_LANES = 128
REC_LANE0, REC_LANE1, REC_W0, REC_W1, REC_RANK0, REC_RANK1 = range(6)

SEQ_TILE = 512
CONV_HALO = 32
CONV_ROWS = 64
MOE_BLOCK = 512
DISPATCH_TILE = 512
COMBINE_TILE = 512
DMA_UNROLL = 8
VMEM_LIMIT = 56 * 1024 * 1024


def _cparams(sem):
    return pltpu.CompilerParams(dimension_semantics=sem, vmem_limit_bytes=VMEM_LIMIT)


def _mod_body(c_ref, w_ref, b_ref, o_ref):
    c = c_ref[...]
    c_act = c * jax.nn.sigmoid(c)
    o_ref[0] = jnp.dot(c_act, w_ref[0], preferred_element_type=F32,
                       precision=lax.Precision.HIGHEST) + b_ref[0]


def _modulation(c, w_mod, b_mod):
    depth, d, dm = w_mod.shape
    b = c.shape[0]
    nt = dm // d
    return pl.pallas_call(
        _mod_body,
        grid=(depth, nt),
        in_specs=[pl.BlockSpec((b, d), lambda l, j: (0, 0)),
                  pl.BlockSpec((1, d, d), lambda l, j: (l, 0, j)),
                  pl.BlockSpec((1, 1, d), lambda l, j: (l, 0, j))],
        out_specs=pl.BlockSpec((1, b, d), lambda l, j: (l, 0, j)),
        out_shape=jax.ShapeDtypeStruct((depth, b, dm), F32),
        compiler_params=_cparams(("arbitrary", "arbitrary")),
        name="modulation",
    )(c, w_mod, b_mod.reshape(depth, 1, dm))


def _rms_modulate(x, g, shift, scale):
    ms = jnp.mean(x * x, axis=-1, keepdims=True)
    return (x * lax.rsqrt(ms + NORM_EPS)) * g * (1.0 + scale) + shift


def _project_in(x, mod_ref, g_ref, w_ref, q_ref, k_ref, v_ref, u_ref):
    h = _rms_modulate(x, g_ref[...], mod_ref[0, 0:1, :], mod_ref[0, 1:2, :])
    y = jnp.dot(h.astype(BF16), w_ref[0], preferred_element_type=F32)
    da = D_ATTN
    q_ref[0] = (y[:, 0:da] * (HEAD_DIM ** -0.5)).astype(q_ref.dtype)
    k_ref[0] = y[:, da:2 * da].astype(k_ref.dtype)
    v_ref[0] = y[:, 2 * da:3 * da].astype(v_ref.dtype)
    dc = (y.shape[1] - 3 * da) // 2
    a = y[:, 3 * da:3 * da + dc]
    b = y[:, 3 * da + dc:]
    u_ref[0] = (a * jax.nn.sigmoid(b)).astype(u_ref.dtype)


def _inproj_body(x_ref, mod_ref, g_ref, w_ref, q_ref, k_ref, v_ref, u_ref):
    _project_in(x_ref[0], mod_ref, g_ref, w_ref, q_ref, k_ref, v_ref, u_ref)


def _input_projection(x, mod, g, w_in_bf, layer):
    b, s, d = x.shape
    din = w_in_bf.shape[2]
    dc = (din - 3 * D_ATTN) // 2
    ts = SEQ_TILE
    tile = lambda n: pl.BlockSpec((1, ts, n), lambda i, j: (i, j, 0))
    return pl.pallas_call(
        _inproj_body,
        grid=(b, s // ts),
        in_specs=[tile(d),
                  pl.BlockSpec((1, N_MOD, d), lambda i, j: (i, 0, 0)),
                  pl.BlockSpec((1, d), lambda i, j: (0, 0)),
                  pl.BlockSpec((1, d, din), lambda i, j: (layer, 0, 0))],
        out_specs=[tile(D_ATTN), tile(D_ATTN), tile(D_ATTN), tile(dc)],
        out_shape=[jax.ShapeDtypeStruct((b, s, D_ATTN), BF16)] * 3
        + [jax.ShapeDtypeStruct((b, s, dc), BF16)],
        compiler_params=_cparams(("arbitrary", "arbitrary")),
        name="input_projection",
    )(x, mod, g.reshape(1, d), w_in_bf)


def _attn_bias_table():
    blk = ATTN_BLOCK
    slopes = jnp.asarray(2.0 ** (-8.0 * np.arange(1, N_HEADS + 1) / N_HEADS), dtype=F32)
    qi = np.arange(blk)[:, None] + blk
    ki = np.arange(2 * blk)[None, :]
    delta = qi - ki
    in_band = (delta >= 0) & (delta <= blk)
    tables = []
    for dil in DILATIONS:
        bias = -slopes[:, None, None] * jnp.asarray(delta * dil, dtype=F32)
        later = jnp.where(in_band[None], bias, MASK_VALUE)
        first = jnp.where((in_band & (ki >= blk))[None], bias, MASK_VALUE)
        tables.append(jnp.stack([later, first], axis=1))
    t = jnp.stack(tables, axis=1)
    return t.reshape(N_HEADS // 2, 2, len(DILATIONS), 2, blk, 2 * blk)


def _attn_body(q_ref, k_ref, v_ref, bias_ref, o_ref, xf, x4, acc_o, acc_l, acc_m, s_buf, p_buf):
    blk = ATTN_BLOCK
    pw = 2 * HEAD_DIM
    s_len = q_ref.shape[1]
    seg = s_len // DEINT
    left = lax.broadcasted_iota(jnp.int32, (blk, pw), 1) < HEAD_DIM
    trans_b = (((1,), (1,)), ((), ()))
    n_branch = len(DILATIONS)
    n_blocks = s_len // blk
    natural = (q_ref.at[0], k_ref.at[0], v_ref.at[0])
    deint = (x4.at[0], x4.at[1], x4.at[2])

    for t, ref in enumerate(natural):
        xf[...] = ref[...].astype(F32)
        for r in range(DEINT):
            x4[t, r * seg:(r + 1) * seg, :] = xf[pl.ds(r, seg, stride=DEINT), :]

    for br, dil in enumerate(DILATIONS):
        nb = s_len // (blk * dil)
        step = dil // DEINT if dil > DEINT else 1
        srcs = natural if dil == 1 else deint

        def starts(j, dil=dil, nb=nb, step=step):
            r = j // nb
            n = j % nb
            if dil == 1:
                base = 0
            else:
                base = (r % DEINT) * seg + r // DEINT
            qs = base + n * (blk * step)
            ks = base + jnp.maximum(n - 1, 0) * (blk * step)
            return qs, ks, jnp.where(n == 0, 1, 0)

        def span(start, step=step):
            return pl.ds(start, blk) if step == 1 else pl.ds(start, blk, stride=step)

        def both(ref, qs, ks):
            return jnp.concatenate([ref[span(ks), :], ref[span(qs), :]], axis=0).astype(BF16)

        def scores(j, slot, br=br, srcs=srcs):
            qs, ks, first = starts(j)
            q = srcs[0][span(qs), :].astype(BF16)
            kk = both(srcs[1], qs, ks)
            zero = jnp.zeros_like(q)
            q2 = jnp.concatenate([jnp.where(left, q, zero), jnp.where(left, zero, q)], axis=0)
            s = lax.dot_general(q2, kk, trans_b, preferred_element_type=F32)
            s_buf[slot] = s.reshape(2, blk, 2 * blk) + bias_ref[0, :, br, first]

        def softmax(j, slot, br=br):
            qs, _, _ = starts(j)
            maxes = []
            for hh in range(2):
                s = s_buf[slot, hh]
                m = jnp.max(s, axis=-1, keepdims=True)
                p_buf[slot, hh] = jnp.exp(s - m).astype(BF16)
                maxes.append(m)
            acc_m[br, span(qs), :] = jnp.where(left, maxes[0], maxes[1])

        def values(j, slot, br=br, srcs=srcs):
            qs, ks, _ = starts(j)
            vv = both(srcs[2], qs, ks)
            rhs = jnp.concatenate([vv, jnp.ones_like(vv)], axis=1)
            o = jnp.dot(p_buf[slot].reshape(2 * blk, 2 * blk), rhs, preferred_element_type=F32)
            acc_o[br, span(qs), :] = jnp.where(left, o[0:blk, 0:pw], o[blk:, 0:pw])
            acc_l[br, span(qs), :] = jnp.where(left, o[0:blk, pw:], o[blk:, pw:])

        nslot = ATTN_UNROLL
        scores(0, 0)
        softmax(0, 0)
        scores(1, 1)

        def steady(i, carry):
            for u in range(ATTN_UNROLL):
                j = 2 + ATTN_UNROLL * i + u
                values(j - 2, u % nslot)
                softmax(j - 1, (u + 1) % nslot)
                scores(j, (u + 2) % nslot)
            return carry

        trips = (n_blocks - 2) // ATTN_UNROLL
        lax.fori_loop(0, trips, steady, 0)
        for j in range(2 + ATTN_UNROLL * trips, n_blocks):
            values(j - 2, (j - 2) % nslot)
            softmax(j - 1, (j - 1) % nslot)
            scores(j, j % nslot)
        values(n_blocks - 2, (n_blocks - 2) % nslot)
        softmax(n_blocks - 1, (n_blocks - 1) % nslot)
        values(n_blocks - 1, (n_blocks - 1) % nslot)

    rows = 256

    def merge(c, carry):
        idx = pl.multiple_of(c * rows, rows)
        r = idx // seg
        nat = pl.ds(DEINT * (idx - r * seg) + r, rows, stride=DEINT)
        spans = [nat if dil == 1 else pl.ds(idx, rows) for dil in DILATIONS]
        ms = [acc_m[br, spans[br], :] for br in range(n_branch)]
        m_max = functools.reduce(jnp.maximum, ms)
        num = 0.0
        den = 0.0
        for br in range(n_branch):
            a = jnp.exp(ms[br] - m_max)
            num = num + a * acc_o[br, spans[br], :]
            den = den + a * acc_l[br, spans[br], :]
        o_ref[0, nat, :] = num / den
        return carry

    lax.fori_loop(0, s_len // rows, merge, 0, unroll=2)


def _dilated_attention(q, k, v, bias):
    b, s, da = q.shape
    pw = 2 * HEAD_DIM
    assert DILATIONS == (1, DEINT, DEINT * DEINT)
    seq = pl.BlockSpec((1, s, pw), lambda i, j: (i, 0, j))
    nbr = len(DILATIONS)
    stage = (ATTN_UNROLL, 2, ATTN_BLOCK, 2 * ATTN_BLOCK)
    return pl.pallas_call(
        _attn_body,
        grid=(b, da // pw),
        in_specs=[seq, seq, seq,
                  pl.BlockSpec((1,) + bias.shape[1:], lambda i, j: (j, 0, 0, 0, 0, 0))],
        out_specs=seq,
        out_shape=jax.ShapeDtypeStruct((b, s, da), F32),
        scratch_shapes=[pltpu.VMEM((s, pw), F32), pltpu.VMEM((3, s, pw), F32)]
        + [pltpu.VMEM((nbr, s, pw), F32)] * 3
        + [pltpu.VMEM(stage, F32), pltpu.VMEM(stage, BF16)],
        compiler_params=_cparams(("arbitrary", "arbitrary")),
        name="dilated_attention",
    )(q, k, v, bias)


def _conv_body(uc_ref, uh_ref, w_ref, cb_ref, lg_ref, lb_ref, o_ref, win, shifted):
    t = pl.program_id(1)
    tc = uc_ref.shape[1]
    win[0:CONV_HALO, :] = jnp.where(t > 0, uh_ref[0].astype(F32), 0.0)
    win[CONV_HALO:, :] = uc_ref[0].astype(F32)
    off = CONV_HALO - (CONV_WIDTH - 1)
    for b in range(SUBLANES):
        n_rows = tc + SUBLANES * ((CONV_WIDTH - 1 - b) // SUBLANES)
        shifted[b, 0:n_rows, :] = win[off + b:off + b + n_rows, :]
    for rb in range(tc // CONV_ROWS):
        r0 = rb * CONV_ROWS
        acc = jnp.zeros((CONV_ROWS, uc_ref.shape[2]), F32)
        for j in range(CONV_WIDTH):
            a, b = divmod(j, SUBLANES)
            rows = pl.ds(r0 + SUBLANES * a, CONV_ROWS)
            acc = acc + w_ref[j:j + 1, :] * shifted[b, rows, :]
        y = acc + cb_ref[...]
        mu = jnp.mean(y, axis=-1, keepdims=True)
        yc = y - mu
        var = jnp.mean(yc * yc, axis=-1, keepdims=True)
        z = yc * lax.rsqrt(var + NORM_EPS) * lg_ref[...] + lb_ref[...]
        o_ref[0, r0:r0 + CONV_ROWS, :] = (z * jax.nn.sigmoid(z)).astype(o_ref.dtype)


def _conformer_conv(u, conv_w, conv_b, ln_g, ln_b):
    b, s, dc = u.shape
    tc = SEQ_TILE
    per = tc // CONV_HALO
    row = pl.BlockSpec((1, dc), lambda i, j: (0, 0))
    return pl.pallas_call(
        _conv_body,
        grid=(b, s // tc),
        in_specs=[pl.BlockSpec((1, tc, dc), lambda i, j: (i, j, 0)),
                  pl.BlockSpec((1, CONV_HALO, dc), lambda i, j: (i, jnp.maximum(j * per - 1, 0), 0)),
                  pl.BlockSpec((CONV_WIDTH, dc), lambda i, j: (0, 0)),
                  row, row, row],
        out_specs=pl.BlockSpec((1, tc, dc), lambda i, j: (i, j, 0)),
        out_shape=jax.ShapeDtypeStruct((b, s, dc), BF16),
        scratch_shapes=[pltpu.VMEM((tc + CONV_HALO, dc), F32),
                        pltpu.VMEM((SUBLANES, tc + CONV_HALO - SUBLANES, dc), F32)],
        compiler_params=_cparams(("arbitrary", "arbitrary")),
        name="conformer_conv",
    )(u, u, conv_w, conv_b.reshape(1, dc), ln_g.reshape(1, dc), ln_b.reshape(1, dc))


def _route(logits, seen):
    rows = logits.shape[0]
    lane = lax.broadcasted_iota(jnp.int32, logits.shape, 1)
    big = jnp.int32(1 << 20)
    is_g = lane < N_GROUPS
    gl = jnp.where(is_g, logits, MASK_VALUE)
    gmax = jnp.max(gl, axis=-1, keepdims=True)
    grp = jnp.min(jnp.where(gl == gmax, lane, big), axis=-1, keepdims=True)
    gsum = jnp.sum(jnp.where(is_g, jnp.exp(gl - gmax), 0.0), axis=-1, keepdims=True)
    p_grp = 1.0 / gsum
    lo = N_GROUPS + grp * EXPERTS_PER_GROUP
    el = jnp.where((lane >= lo) & (lane < lo + EXPERTS_PER_GROUP), logits, MASK_VALUE)
    v1 = jnp.max(el, axis=-1, keepdims=True)
    i1 = jnp.min(jnp.where(el == v1, lane, big), axis=-1, keepdims=True)
    el2 = jnp.where(lane == i1, MASK_VALUE, el)
    v2 = jnp.max(el2, axis=-1, keepdims=True)
    i2 = jnp.min(jnp.where(el2 == v2, lane, big), axis=-1, keepdims=True)
    e = jnp.exp(v2 - v1)
    w1 = p_grp / (1.0 + e)
    w2 = p_grp * e / (1.0 + e)

    hit1 = lane == i1
    hit2 = lane == i2
    chosen = jnp.where(hit1 | hit2, 1.0, 0.0)
    ri = lax.broadcasted_iota(jnp.int32, (rows, rows), 0)
    ci = lax.broadcasted_iota(jnp.int32, (rows, rows), 1)
    earlier = jnp.where(ri > ci, 1.0, 0.0).astype(BF16)
    before = jnp.dot(earlier, chosen.astype(BF16), preferred_element_type=F32) + seen
    rank1 = jnp.sum(jnp.where(hit1, before, 0.0), axis=-1, keepdims=True)
    rank2 = jnp.sum(jnp.where(hit2, before, 0.0), axis=-1, keepdims=True)
    seen = seen + jnp.sum(chosen, axis=0, keepdims=True)

    rec = jnp.zeros(logits.shape, F32)
    for pos, val in ((REC_LANE0, i1.astype(F32)), (REC_LANE1, i2.astype(F32)), (REC_W0, w1),
                     (REC_W1, w2), (REC_RANK0, rank1), (REC_RANK1, rank2)):
        rec = jnp.where(lane == pos, val, rec)
    return rec, seen


def _split_bf16(x):
    hi = x.astype(BF16)
    lo = (x - hi.astype(F32)).astype(BF16)
    return hi, lo


def _outproj_body(a_ref, c_ref, x_ref, mod_ref, g_ref, w_ref, wrh_ref, wrl_ref, br_ref,
                  xo_ref, h_ref, r_ref, cnt_ref, seen):
    @pl.when((pl.program_id(0) == 0) & (pl.program_id(1) == 0))
    def _():
        seen[...] = jnp.zeros_like(seen)

    da = a_ref.shape[2]
    mix = jnp.dot(a_ref[0].astype(BF16), w_ref[0, 0:da, :], preferred_element_type=F32)
    mix = mix + jnp.dot(c_ref[0], w_ref[0, da:, :], preferred_element_type=F32)
    xn = x_ref[0] + mod_ref[0, 2:3, :] * mix
    xo_ref[0] = xn
    h = _rms_modulate(xn, g_ref[...], mod_ref[0, 3:4, :], mod_ref[0, 4:5, :])
    _store_token_tiles(h_ref, h)
    h_hi, h_lo = _split_bf16(h)
    logits = (jnp.dot(h_hi, wrh_ref[0], preferred_element_type=F32)
              + jnp.dot(h_lo, wrh_ref[0], preferred_element_type=F32)
              + jnp.dot(h_hi, wrl_ref[0], preferred_element_type=F32)) + br_ref[0]
    rec, new_seen = _route(logits, seen[...])
    r_ref[0] = rec
    seen[...] = new_seen
    cnt_ref[...] = new_seen


def _output_projection(attn, conv, x, mod, g, w_out_bf, wr_hi, wr_lo, b_router, layer):
    b, s, d = x.shape
    ts = SEQ_TILE
    tile = lambda n: pl.BlockSpec((1, ts, n), lambda i, j: (i, j, 0))
    const = lambda r, c: pl.BlockSpec((r, c), lambda i, j: (0, 0))
    per_layer = lambda r, c: pl.BlockSpec((1, r, c), lambda i, j: (layer, 0, 0))
    return pl.pallas_call(
        _outproj_body,
        grid=(b, s // ts),
        in_specs=[tile(attn.shape[2]), tile(conv.shape[2]), tile(d),
                  pl.BlockSpec((1, N_MOD, d), lambda i, j: (i, 0, 0)),
                  const(1, d), per_layer(d, d), per_layer(d, ROUTE_LANES),
                  per_layer(d, ROUTE_LANES), per_layer(1, ROUTE_LANES)],
        out_specs=[tile(d), pl.BlockSpec((ts * SUBLANES, LANES), lambda i, j: (i * (s // ts) + j, 0)),
                   tile(ROUTE_LANES), const(1, ROUTE_LANES)],
        out_shape=[jax.ShapeDtypeStruct((b, s, d), F32),
                   jax.ShapeDtypeStruct((b * s * SUBLANES, LANES), F32),
                   jax.ShapeDtypeStruct((b, s, ROUTE_LANES), F32),
                   jax.ShapeDtypeStruct((1, ROUTE_LANES), F32)],
        scratch_shapes=[pltpu.VMEM((1, ROUTE_LANES), F32)],
        compiler_params=_cparams(("arbitrary", "arbitrary")),
        name="output_projection",
    )(attn, conv, x, mod, g.reshape(1, d), w_out_bf, wr_hi, wr_lo, b_router)


def _load_token_tiles(ref):
    n = ref.shape[0] // SUBLANES
    return jnp.concatenate([ref[pl.ds(a, n, stride=SUBLANES), :] for a in range(SUBLANES)], axis=1)


def _store_token_tiles(ref, val):
    n = val.shape[0]
    for a in range(SUBLANES):
        ref[pl.ds(a, n, stride=SUBLANES), :] = val[:, a * LANES:(a + 1) * LANES]


def _tile_of(ref, token):
    return ref.at[pl.ds(pl.multiple_of(token * SUBLANES, SUBLANES), SUBLANES)]


def _row_copy(src, src_token, dst, dst_token, sem):
    return pltpu.make_async_copy(_tile_of(src, src_token), _tile_of(dst, dst_token), sem)


def _wait_rows(like, n_tokens, sem):
    n = n_tokens * SUBLANES
    pltpu.make_async_copy(like.at[pl.ds(0, n)], like.at[pl.ds(0, n)], sem).wait()


def _dispatch_body(dest_ref, h_ref, xb_hbm, sem):
    td = h_ref.shape[0] // SUBLANES
    base = pl.program_id(0) * td

    def issue(t, carry):
        for choice in range(2):
            _row_copy(h_ref, t, xb_hbm, dest_ref[2 * (base + t) + choice], sem).start()
        return carry

    lax.fori_loop(0, td, issue, 0, unroll=DMA_UNROLL)
    _wait_rows(xb_hbm, 2 * td, sem)


def _dispatch(h, dest, n_rows):
    t = h.shape[0] // SUBLANES
    td = DISPATCH_TILE
    return pl.pallas_call(
        _dispatch_body,
        grid_spec=pltpu.PrefetchScalarGridSpec(
            num_scalar_prefetch=1,
            grid=(t // td,),
            in_specs=[pl.BlockSpec((td * SUBLANES, LANES), lambda i, *_: (i, 0))],
            out_specs=pl.BlockSpec(memory_space=pl.ANY),
            scratch_shapes=[pltpu.SemaphoreType.DMA(())]),
        out_shape=jax.ShapeDtypeStruct((n_rows * SUBLANES, LANES), h.dtype),
        compiler_params=_cparams(("arbitrary",)),
        name="moe_dispatch",
    )(dest, h)


def _expert_body(eid_ref, valid_ref, nused_ref, x_ref, wg_ref, wu_ref, wd_ref, o_ref,
                 wg_bf, wu_bf, wd_bf):
    i = pl.program_id(0)

    @pl.when(i < nused_ref[0])
    def _():
        changed = (i == 0) | (eid_ref[i] != eid_ref[jnp.maximum(i - 1, 0)])

        @pl.when(changed)
        def _():
            wg_bf[...] = wg_ref[0, 0].astype(BF16)
            wu_bf[...] = wu_ref[0, 0].astype(BF16)
            wd_bf[...] = wd_ref[0, 0].astype(BF16)

        x = _load_token_tiles(x_ref)
        row = lax.broadcasted_iota(jnp.int32, x.shape, 0)
        x = jnp.where(row < valid_ref[i], x, 0.0).astype(BF16)
        g = jnp.dot(x, wg_bf[...], preferred_element_type=F32)
        u = jnp.dot(x, wu_bf[...], preferred_element_type=F32)
        h = (g * jax.nn.sigmoid(g)) * u
        _store_token_tiles(o_ref, jnp.dot(h.astype(BF16), wd_bf[...], preferred_element_type=F32))

    @pl.when(i >= nused_ref[0])
    def _():
        o_ref[...] = jnp.zeros_like(o_ref)


def _experts(xb, blk_eid, blk_valid, n_used, w_gate, w_up, w_down, layer):
    p = xb.shape[0] // SUBLANES
    d, de = w_gate.shape[2:]
    tm = MOE_BLOCK
    last = lambda i, nu: jnp.minimum(i, nu[0] - 1)
    rows = pl.BlockSpec((tm * SUBLANES, LANES), lambda i, eid, nv, nu: (last(i, nu), 0))
    weight = lambda r, c: pl.BlockSpec(
        (1, 1, r, c), lambda i, eid, nv, nu: (layer, eid[last(i, nu)], 0, 0))
    return pl.pallas_call(
        _expert_body,
        grid_spec=pltpu.PrefetchScalarGridSpec(
            num_scalar_prefetch=3,
            grid=(p // tm,),
            in_specs=[rows, weight(d, de), weight(d, de), weight(de, d)],
            out_specs=pl.BlockSpec((tm * SUBLANES, LANES), lambda i, eid, nv, nu: (i, 0)),
            scratch_shapes=[pltpu.VMEM((d, de), BF16), pltpu.VMEM((d, de), BF16),
                            pltpu.VMEM((de, d), BF16)]),
        out_shape=jax.ShapeDtypeStruct((p * SUBLANES, LANES), F32),
        compiler_params=_cparams(("arbitrary",)),
        name="moe_experts",
    )(blk_eid, blk_valid, n_used, xb, w_gate, w_up, w_down)


def _gather_rows(dest_ref, yb_hbm, ybuf, sem, tile, into, n_tokens, unroll):
    def issue(t, carry):
        for choice in range(2):
            src = dest_ref[2 * (tile * n_tokens + t) + choice]
            _row_copy(yb_hbm, src, ybuf.at[into, choice], t, sem.at[into, choice]).start()
        return carry

    lax.fori_loop(0, n_tokens, issue, 0, unroll=unroll)


def _gathered_moe(r_ref, ybuf, sem, slot, n_tokens):
    for choice in range(2):
        _wait_rows(ybuf.at[slot, choice], n_tokens, sem.at[slot, choice])
    return (r_ref[:, REC_W0:REC_W0 + 1] * _load_token_tiles(ybuf.at[slot, 0])
            + r_ref[:, REC_W1:REC_W1 + 1] * _load_token_tiles(ybuf.at[slot, 1]))


def _combine_final_body(dest_ref, x_ref, r_ref, mod_ref, g_ref, yb_hbm, o_ref, ybuf, sem):
    tcm = x_ref.shape[0]
    i = pl.program_id(0)
    slot = i % 2

    @pl.when(i == 0)
    def _():
        _gather_rows(dest_ref, yb_hbm, ybuf, sem, 0, 0, tcm, DMA_UNROLL)

    @pl.when(i + 1 < pl.num_programs(0))
    def _():
        _gather_rows(dest_ref, yb_hbm, ybuf, sem, i + 1, 1 - slot, tcm, DMA_UNROLL)

    xn = x_ref[...] + mod_ref[0, 5:6, :] * _gathered_moe(r_ref, ybuf, sem, slot, tcm)
    ms = jnp.mean(xn * xn, axis=-1, keepdims=True)
    o_ref[...] = (xn * lax.rsqrt(ms + NORM_EPS)) * g_ref[...]


def _combine_final(x, route, mod, g_final, yb, dest, seq_len):
    t, d = x.shape
    tcm = COMBINE_TILE
    per_seq = seq_len // tcm
    return pl.pallas_call(
        _combine_final_body,
        grid_spec=pltpu.PrefetchScalarGridSpec(
            num_scalar_prefetch=1,
            grid=(t // tcm,),
            in_specs=[pl.BlockSpec((tcm, d), lambda i, *_: (i, 0)),
                      pl.BlockSpec((tcm, ROUTE_LANES), lambda i, *_: (i, 0)),
                      pl.BlockSpec((1, N_MOD, d), lambda i, *_: (i // per_seq, 0, 0)),
                      pl.BlockSpec((1, d), lambda i, *_: (0, 0)),
                      pl.BlockSpec(memory_space=pl.ANY)],
            out_specs=pl.BlockSpec((tcm, d), lambda i, *_: (i, 0)),
            scratch_shapes=[pltpu.VMEM((2, 2, tcm * SUBLANES, LANES), F32),
                            pltpu.SemaphoreType.DMA((2, 2))]),
        out_shape=jax.ShapeDtypeStruct((t, d), F32),
        compiler_params=_cparams(("arbitrary",)),
        name="moe_combine_final",
    )(dest, x, route, mod, g_final.reshape(1, d), yb)


def _combine_inproj_body(dest_ref, x_ref, r_ref, modp_ref, mod_ref, g_ref, w_ref, yb_hbm,
                         xo_ref, q_ref, k_ref, v_ref, u_ref, ybuf, sem):
    ts = x_ref.shape[1]
    per_seq = pl.num_programs(1)
    n_tiles = pl.num_programs(0) * per_seq
    i = pl.program_id(0) * per_seq + pl.program_id(1)
    slot = i % 2

    @pl.when(i == 0)
    def _():
        _gather_rows(dest_ref, yb_hbm, ybuf, sem, 0, 0, ts, DMA_UNROLL)

    moe = _gathered_moe(r_ref, ybuf, sem, slot, ts)
    _gather_rows(dest_ref, yb_hbm, ybuf, sem, jnp.minimum(i + 1, n_tiles - 1), 1 - slot, ts, True)
    xn = x_ref[0] + modp_ref[0, 5:6, :] * moe
    xo_ref[0] = xn
    _project_in(xn, mod_ref, g_ref, w_ref, q_ref, k_ref, v_ref, u_ref)

    @pl.when(i == n_tiles - 1)
    def _():
        for choice in range(2):
            _wait_rows(ybuf.at[1 - slot, choice], ts, sem.at[1 - slot, choice])


def _combine_input_projection(x, route, mod_prev, yb, dest, mod, g, w_in_bf, layer):
    b, s, d = x.shape
    din = w_in_bf.shape[2]
    dc = (din - 3 * D_ATTN) // 2
    ts = SEQ_TILE
    per_seq = s // ts
    tile = lambda n: pl.BlockSpec((1, ts, n), lambda i, j, *_: (i, j, 0))
    mods = pl.BlockSpec((1, N_MOD, d), lambda i, j, *_: (i, 0, 0))
    return pl.pallas_call(
        _combine_inproj_body,
        grid_spec=pltpu.PrefetchScalarGridSpec(
            num_scalar_prefetch=1,
            grid=(b, per_seq),
            in_specs=[tile(d),
                      pl.BlockSpec((ts, ROUTE_LANES), lambda i, j, *_: (i * per_seq + j, 0)),
                      mods, mods,
                      pl.BlockSpec((1, d), lambda i, j, *_: (0, 0)),
                      pl.BlockSpec((1, d, din), lambda i, j, *_: (layer, 0, 0)),
                      pl.BlockSpec(memory_space=pl.ANY)],
            out_specs=[tile(d), tile(D_ATTN), tile(D_ATTN), tile(D_ATTN), tile(dc)],
            scratch_shapes=[pltpu.VMEM((2, 2, ts * SUBLANES, LANES), F32),
                            pltpu.SemaphoreType.DMA((2, 2))]),
        out_shape=[jax.ShapeDtypeStruct((b, s, d), F32)]
        + [jax.ShapeDtypeStruct((b, s, D_ATTN), BF16)] * 3
        + [jax.ShapeDtypeStruct((b, s, dc), BF16)],
        compiler_params=_cparams(("arbitrary", "arbitrary")),
        name="combine_input_projection",
    )(dest, x, route, mod_prev, mod, g.reshape(1, d), w_in_bf, yb)


def _plan_blocks(counts, n_blocks):
    tm = MOE_BLOCK
    padded = ((counts + tm - 1) // tm) * tm
    pad_end = jnp.cumsum(padded)
    pad_start = pad_end - padded
    blk_row = jnp.arange(n_blocks, dtype=jnp.int32) * tm
    blk_eid = jnp.minimum(jnp.sum(pad_end[None, :] <= blk_row[:, None], axis=1), N_EXPERTS - 1)
    blk_valid = jnp.clip(pad_start[blk_eid] + counts[blk_eid] - blk_row, 0, tm)
    n_used = pad_end[-1:] // tm
    i32 = lambda a: a.astype(jnp.int32)
    return i32(pad_start), i32(blk_eid), i32(blk_valid), i32(n_used)


def kernel(x, c, w_mod, b_mod, g_norm1, w_in, conv_w, conv_b, conv_ln_g, conv_ln_b, w_out, g_norm2, w_router_group, b_router_group, w_router_expert, b_router_expert, w_exp_gate, w_exp_up, w_exp_down, g_final):
    b, s, d = x.shape
    depth = w_mod.shape[0]
    t = b * s
    assert d == SUBLANES * LANES
    assert s % (ATTN_BLOCK * max(DILATIONS)) == 0 and s % SEQ_TILE == 0
    assert t % DISPATCH_TILE == 0 and s % COMBINE_TILE == 0
    n_blocks = -(-2 * t // MOE_BLOCK) + N_EXPERTS
    n_rows = n_blocks * MOE_BLOCK

    mod_all = _modulation(c, w_mod, b_mod).reshape(depth, b, N_MOD, d)
    bias = _attn_bias_table()
    pad = ROUTE_LANES - N_GROUPS - N_EXPERTS
    w_router = jnp.concatenate(
        [w_router_group, w_router_expert, jnp.zeros((depth, d, pad), F32)], axis=-1)
    b_router = jnp.concatenate(
        [b_router_group, b_router_expert, jnp.zeros((depth, pad), F32)], axis=-1)
    b_router = b_router.reshape(depth, 1, ROUTE_LANES)
    wr_hi = w_router.astype(BF16)
    wr_lo = (w_router - wr_hi.astype(F32)).astype(BF16)
    w_in_bf = w_in.astype(BF16)
    w_out_bf = w_out.astype(BF16)

    pending = None
    for l in range(depth):
        mod = mod_all[l]
        if pending is None:
            q, k, v, u = _input_projection(x, mod, g_norm1[l], w_in_bf, l)
        else:
            x, q, k, v, u = _combine_input_projection(x, *pending, mod, g_norm1[l], w_in_bf, l)
        attn = _dilated_attention(q, k, v, bias)
        conv = _conformer_conv(u, conv_w[l], conv_b[l], conv_ln_g[l], conv_ln_b[l])
        x, h2, route, seen = _output_projection(attn, conv, x, mod, g_norm2[l], w_out_bf,
                                                wr_hi, wr_lo, b_router, l)
        route = route.reshape(t, ROUTE_LANES)
        counts = seen[0, N_GROUPS:N_GROUPS + N_EXPERTS].astype(jnp.int32)
        starts, blk_eid, blk_valid, n_used = _plan_blocks(counts, n_blocks)
        eid = route[:, REC_LANE0:REC_LANE1 + 1].astype(jnp.int32) - N_GROUPS
        ranks = route[:, REC_RANK0:REC_RANK1 + 1].astype(jnp.int32)
        onehot = eid[:, :, None] == jnp.arange(N_EXPERTS, dtype=jnp.int32)
        dest = (jnp.sum(jnp.where(onehot, starts, 0), axis=-1) + ranks).reshape(-1)
        xb = _dispatch(h2, dest, n_rows)
        yb = _experts(xb, blk_eid, blk_valid, n_used, w_exp_gate, w_exp_up, w_exp_down, l)
        pending = (route, mod, yb, dest)
    route, mod, yb, dest = pending
    return _combine_final(x.reshape(t, d), route, mod, g_final, yb, dest, s).reshape(b, s, d)
```

```python
import functools

import numpy as np
import jax
import jax.numpy as jnp
from jax import lax
from jax.experimental import pallas as pl
from jax.experimental.pallas import tpu as pltpu

F32 = jnp.float32
BF16 = jnp.bfloat16

SUBLANES = 8
LANES = 128
HEAD_DIM = 64
N_HEADS = 8
D_ATTN = N_HEADS * HEAD_DIM
CONV_WIDTH = 31
DILATIONS = (1, 4, 16)
ATTN_BLOCK = 128
ATTN_UNROLL = 10
DEINT = 4
N_GROUPS = 4
EXPERTS_PER_GROUP = 8
N_EXPERTS = N_GROUPS * EXPERTS_PER_GROUP
NORM_EPS = 1e-6
N_MOD = 6
MASK_VALUE = -1e30
ROUTE_LANES = 128
REC_LANE0, REC_LANE1, REC_W0, REC_W1, REC_RANK0, REC_RANK1 = range(6)

SEQ_TILE = 512
CONV_HALO = 32
CONV_ROWS = 64
MOE_BLOCK = 512
DISPATCH_TILE = 512
COMBINE_TILE = 512
DMA_UNROLL = 8
VMEM_LIMIT = 56 * 1024 * 1024


def _cparams(sem):
    return pltpu.CompilerParams(dimension_semantics=sem, vmem_limit_bytes=VMEM_LIMIT)


def _mod_body(c_ref, w_ref, b_ref, o_ref):
    c = c_ref[...]
    c_act = c * jax.nn.sigmoid(c)
    o_ref[0] = jnp.dot(c_act, w_ref[0], preferred_element_type=F32,
                       precision=lax.Precision.HIGHEST) + b_ref[0]


def _modulation(c, w_mod, b_mod):
    depth, d, dm = w_mod.shape
    b = c.shape[0]
    nt = dm // d
    return pl.pallas_call(
        _mod_body,
        grid=(depth, nt),
        in_specs=[pl.BlockSpec((b, d), lambda l, j: (0, 0)),
                  pl.BlockSpec((1, d, d), lambda l, j: (l, 0, j)),
                  pl.BlockSpec((1, 1, d), lambda l, j: (l, 0, j))],
        out_specs=pl.BlockSpec((1, b, d), lambda l, j: (l, 0, j)),
        out_shape=jax.ShapeDtypeStruct((depth, b, dm), F32),
        compiler_params=_cparams(("arbitrary", "arbitrary")),
        name="modulation",
    )(c, w_mod, b_mod.reshape(depth, 1, dm))


def _rms_modulate(x, g, shift, scale):
    ms = jnp.mean(x * x, axis=-1, keepdims=True)
    return (x * lax.rsqrt(ms + NORM_EPS)) * g * (1.0 + scale) + shift


def _project_in(x, mod_ref, g_ref, w_ref, q_ref, k_ref, v_ref, u_ref):
    h = _rms_modulate(x, g_ref[...], mod_ref[0, 0:1, :], mod_ref[0, 1:2, :])
    y = jnp.dot(h.astype(BF16), w_ref[0], preferred_element_type=F32)
    da = D_ATTN
    q_ref[0] = (y[:, 0:da] * (HEAD_DIM ** -0.5)).astype(q_ref.dtype)
    k_ref[0] = y[:, da:2 * da].astype(k_ref.dtype)
    v_ref[0] = y[:, 2 * da:3 * da].astype(v_ref.dtype)
    dc = (y.shape[1] - 3 * da) // 2
    a = y[:, 3 * da:3 * da + dc]
    b = y[:, 3 * da + dc:]
    u_ref[0] = (a * jax.nn.sigmoid(b)).astype(u_ref.dtype)


def _inproj_body(x_ref, mod_ref, g_ref, w_ref, q_ref, k_ref, v_ref, u_ref):
    _project_in(x_ref[0], mod_ref, g_ref, w_ref, q_ref, k_ref, v_ref, u_ref)


def _input_projection(x, mod, g, w_in_bf, layer):
    b, s, d = x.shape
    din = w_in_bf.shape[2]
    dc = (din - 3 * D_ATTN) // 2
    ts = SEQ_TILE
    tile = lambda n: pl.BlockSpec((1, ts, n), lambda i, j: (i, j, 0))
    return pl.pallas_call(
        _inproj_body,
        grid=(b, s // ts),
        in_specs=[tile(d),
                  pl.BlockSpec((1, N_MOD, d), lambda i, j: (i, 0, 0)),
                  pl.BlockSpec((1, d), lambda i, j: (0, 0)),
                  pl.BlockSpec((1, d, din), lambda i, j: (layer, 0, 0))],
        out_specs=[tile(D_ATTN), tile(D_ATTN), tile(D_ATTN), tile(dc)],
        out_shape=[jax.ShapeDtypeStruct((b, s, D_ATTN), BF16)] * 3
        + [jax.ShapeDtypeStruct((b, s, dc), BF16)],
        compiler_params=_cparams(("arbitrary", "arbitrary")),
        name="input_projection",
    )(x, mod, g.reshape(1, d), w_in_bf)


def _attn_bias_table():
    blk = ATTN_BLOCK
    slopes = jnp.asarray(2.0 ** (-8.0 * np.arange(1, N_HEADS + 1) / N_HEADS), dtype=F32)
    qi = np.arange(blk)[:, None] + blk
    ki = np.arange(2 * blk)[None, :]
    delta = qi - ki
    in_band = (delta >= 0) & (delta <= blk)
    tables = []
    for dil in DILATIONS:
        bias = -slopes[:, None, None] * jnp.asarray(delta * dil, dtype=F32)
        later = jnp.where(in_band[None], bias, MASK_VALUE)
        first = jnp.where((in_band & (ki >= blk))[None], bias, MASK_VALUE)
        tables.append(jnp.stack([later, first], axis=1))
    t = jnp.stack(tables, axis=1)
    return t.reshape(N_HEADS // 2, 2, len(DILATIONS), 2, blk, 2 * blk)


def _attn_body(q_ref, k_ref, v_ref, bias_ref, o_ref, xf, x4, acc_o, acc_l, acc_m, s_buf, p_buf):
    blk = ATTN_BLOCK
    pw = 2 * HEAD_DIM
    s_len = q_ref.shape[1]
    seg = s_len // DEINT
    left = lax.broadcasted_iota(jnp.int32, (blk, pw), 1) < HEAD_DIM
    trans_b = (((1,), (1,)), ((), ()))
    n_branch = len(DILATIONS)
    n_blocks = s_len // blk
    natural = (q_ref.at[0], k_ref.at[0], v_ref.at[0])
    deint = (x4.at[0], x4.at[1], x4.at[2])

    for t, ref in enumerate(natural):
        xf[...] = ref[...].astype(F32)
        for r in range(DEINT):
            x4[t, r * seg:(r + 1) * seg, :] = xf[pl.ds(r, seg, stride=DEINT), :]

    for br, dil in enumerate(DILATIONS):
        nb = s_len // (blk * dil)
        step = dil // DEINT if dil > DEINT else 1
        srcs = natural if dil == 1 else deint

        def starts(j, dil=dil, nb=nb, step=step):
            r = j // nb
            n = j % nb
            if dil == 1:
                base = 0
            else:
                base = (r % DEINT) * seg + r // DEINT
            qs = base + n * (blk * step)
            ks = base + jnp.maximum(n - 1, 0) * (blk * step)
            return qs, ks, jnp.where(n == 0, 1, 0)

        def span(start, step=step):
            return pl.ds(start, blk) if step == 1 else pl.ds(start, blk, stride=step)

        def both(ref, qs, ks):
            return jnp.concatenate([ref[span(ks), :], ref[span(qs), :]], axis=0).astype(BF16)

        def scores(j, slot, br=br, srcs=srcs):
            qs, ks, first = starts(j)
            q = srcs[0][span(qs), :].astype(BF16)
            kk = both(srcs[1], qs, ks)
            zero = jnp.zeros_like(q)
            q2 = jnp.concatenate([jnp.where(left, q, zero), jnp.where(left, zero, q)], axis=0)
            s = lax.dot_general(q2, kk, trans_b, preferred_element_type=F32)
            s_buf[slot] = s.reshape(2, blk, 2 * blk) + bias_ref[0, :, br, first]

        def softmax(j, slot, br=br):
            qs, _, _ = starts(j)
            maxes = []
            for hh in range(2):
                s = s_buf[slot, hh]
                m = jnp.max(s, axis=-1, keepdims=True)
                p_buf[slot, hh] = jnp.exp(s - m).astype(BF16)
                maxes.append(m)
            acc_m[br, span(qs), :] = jnp.where(left, maxes[0], maxes[1])

        def values(j, slot, br=br, srcs=srcs):
            qs, ks, _ = starts(j)
            vv = both(srcs[2], qs, ks)
            rhs = jnp.concatenate([vv, jnp.ones_like(vv)], axis=1)
            o = jnp.dot(p_buf[slot].reshape(2 * blk, 2 * blk), rhs, preferred_element_type=F32)
            acc_o[br, span(qs), :] = jnp.where(left, o[0:blk, 0:pw], o[blk:, 0:pw])
            acc_l[br, span(qs), :] = jnp.where(left, o[0:blk, pw:], o[blk:, pw:])

        nslot = ATTN_UNROLL
        scores(0, 0)
        softmax(0, 0)
        scores(1, 1)

        def steady(i, carry):
            for u in range(ATTN_UNROLL):
                j = 2 + ATTN_UNROLL * i + u
                values(j - 2, u % nslot)
                softmax(j - 1, (u + 1) % nslot)
                scores(j, (u + 2) % nslot)
            return carry

        trips = (n_blocks - 2) // ATTN_UNROLL
        lax.fori_loop(0, trips, steady, 0)
        for j in range(2 + ATTN_UNROLL * trips, n_blocks):
            values(j - 2, (j - 2) % nslot)
            softmax(j - 1, (j - 1) % nslot)
            scores(j, j % nslot)
        values(n_blocks - 2, (n_blocks - 2) % nslot)
        softmax(n_blocks - 1, (n_blocks - 1) % nslot)
        values(n_blocks - 1, (n_blocks - 1) % nslot)

    rows = 256

    def merge(c, carry):
        idx = pl.multiple_of(c * rows, rows)
        r = idx // seg
        nat = pl.ds(DEINT * (idx - r * seg) + r, rows, stride=DEINT)
        spans = [nat if dil == 1 else pl.ds(idx, rows) for dil in DILATIONS]
        ms = [acc_m[br, spans[br], :] for br in range(n_branch)]
        m_max = functools.reduce(jnp.maximum, ms)
        num = 0.0
        den = 0.0
        for br in range(n_branch):
            a = jnp.exp(ms[br] - m_max)
            num = num + a * acc_o[br, spans[br], :]
            den = den + a * acc_l[br, spans[br], :]
        o_ref[0, nat, :] = num / den
        return carry

    lax.fori_loop(0, s_len // rows, merge, 0, unroll=2)


def _dilated_attention(q, k, v, bias):
    b, s, da = q.shape
    pw = 2 * HEAD_DIM
    assert DILATIONS == (1, DEINT, DEINT * DEINT)
    seq = pl.BlockSpec((1, s, pw), lambda i, j: (i, 0, j))
    nbr = len(DILATIONS)
    stage = (ATTN_UNROLL, 2, ATTN_BLOCK, 2 * ATTN_BLOCK)
    return pl.pallas_call(
        _attn_body,
        grid=(b, da // pw),
        in_specs=[seq, seq, seq,
                  pl.BlockSpec((1,) + bias.shape[1:], lambda i, j: (j, 0, 0, 0, 0, 0))],
        out_specs=seq,
        out_shape=jax.ShapeDtypeStruct((b, s, da), F32),
        scratch_shapes=[pltpu.VMEM((s, pw), F32), pltpu.VMEM((3, s, pw), F32)]
        + [pltpu.VMEM((nbr, s, pw), F32)] * 3
        + [pltpu.VMEM(stage, F32), pltpu.VMEM(stage, BF16)],
        compiler_params=_cparams(("arbitrary", "arbitrary")),
        name="dilated_attention",
    )(q, k, v, bias)


def _conv_body(uc_ref, uh_ref, w_ref, cb_ref, lg_ref, lb_ref, o_ref, win, shifted):
    t = pl.program_id(1)
    tc = uc_ref.shape[1]
    win[0:CONV_HALO, :] = jnp.where(t > 0, uh_ref[0].astype(F32), 0.0)
    win[CONV_HALO:, :] = uc_ref[0].astype(F32)
    off = CONV_HALO - (CONV_WIDTH - 1)
    for b in range(SUBLANES):
        n_rows = tc + SUBLANES * ((CONV_WIDTH - 1 - b) // SUBLANES)
        shifted[b, 0:n_rows, :] = win[off + b:off + b + n_rows, :]
    for rb in range(tc // CONV_ROWS):
        r0 = rb * CONV_ROWS
        acc = jnp.zeros((CONV_ROWS, uc_ref.shape[2]), F32)
        for j in range(CONV_WIDTH):
            a, b = divmod(j, SUBLANES)
            rows = pl.ds(r0 + SUBLANES * a, CONV_ROWS)
            acc = acc + w_ref[j:j + 1, :] * shifted[b, rows, :]
        y = acc + cb_ref[...]
        mu = jnp.mean(y, axis=-1, keepdims=True)
        yc = y - mu
        var = jnp.mean(yc * yc, axis=-1, keepdims=True)
        z = yc * lax.rsqrt(var + NORM_EPS) * lg_ref[...] + lb_ref[...]
        o_ref[0, r0:r0 + CONV_ROWS, :] = (z * jax.nn.sigmoid(z)).astype(o_ref.dtype)


def _conformer_conv(u, conv_w, conv_b, ln_g, ln_b):
    b, s, dc = u.shape
    tc = SEQ_TILE
    per = tc // CONV_HALO
    row = pl.BlockSpec((1, dc), lambda i, j: (0, 0))
    return pl.pallas_call(
        _conv_body,
        grid=(b, s // tc),
        in_specs=[pl.BlockSpec((1, tc, dc), lambda i, j: (i, j, 0)),
                  pl.BlockSpec((1, CONV_HALO, dc), lambda i, j: (i, jnp.maximum(j * per - 1, 0), 0)),
                  pl.BlockSpec((CONV_WIDTH, dc), lambda i, j: (0, 0)),
                  row, row, row],
        out_specs=pl.BlockSpec((1, tc, dc), lambda i, j: (i, j, 0)),
        out_shape=jax.ShapeDtypeStruct((b, s, dc), BF16),
        scratch_shapes=[pltpu.VMEM((tc + CONV_HALO, dc), F32),
                        pltpu.VMEM((SUBLANES, tc + CONV_HALO - SUBLANES, dc), F32)],
        compiler_params=_cparams(("arbitrary", "arbitrary")),
        name="conformer_conv",
    )(u, u, conv_w, conv_b.reshape(1, dc), ln_g.reshape(1, dc), ln_b.reshape(1, dc))


def _route(logits, seen):
    rows = logits.shape[0]
    lane = lax.broadcasted_iota(jnp.int32, logits.shape, 1)
    big = jnp.int32(1 << 20)
    is_g = lane < N_GROUPS
    gl = jnp.where(is_g, logits, MASK_VALUE)
    gmax = jnp.max(gl, axis=-1, keepdims=True)
    grp = jnp.min(jnp.where(gl == gmax, lane, big), axis=-1, keepdims=True)
    gsum = jnp.sum(jnp.where(is_g, jnp.exp(gl - gmax), 0.0), axis=-1, keepdims=True)
    p_grp = 1.0 / gsum
    lo = N_GROUPS + grp * EXPERTS_PER_GROUP
    el = jnp.where((lane >= lo) & (lane < lo + EXPERTS_PER_GROUP), logits, MASK_VALUE)
    v1 = jnp.max(el, axis=-1, keepdims=True)
    i1 = jnp.min(jnp.where(el == v1, lane, big), axis=-1, keepdims=True)
    el2 = jnp.where(lane == i1, MASK_VALUE, el)
    v2 = jnp.max(el2, axis=-1, keepdims=True)
    i2 = jnp.min(jnp.where(el2 == v2, lane, big), axis=-1, keepdims=True)
    e = jnp.exp(v2 - v1)
    w1 = p_grp / (1.0 + e)
    w2 = p_grp * e / (1.0 + e)

    hit1 = lane == i1
    hit2 = lane == i2
    chosen = jnp.where(hit1 | hit2, 1.0, 0.0)
    ri = lax.broadcasted_iota(jnp.int32, (rows, rows), 0)
    ci = lax.broadcasted_iota(jnp.int32, (rows, rows), 1)
    earlier = jnp.where(ri > ci, 1.0, 0.0).astype(BF16)
    before = jnp.dot(earlier, chosen.astype(BF16), preferred_element_type=F32) + seen
    rank1 = jnp.sum(jnp.where(hit1, before, 0.0), axis=-1, keepdims=True)
    rank2 = jnp.sum(jnp.where(hit2, before, 0.0), axis=-1, keepdims=True)
    seen = seen + jnp.sum(chosen, axis=0, keepdims=True)

    rec = jnp.zeros(logits.shape, F32)
    for pos, val in ((REC_LANE0, i1.astype(F32)), (REC_LANE1, i2.astype(F32)), (REC_W0, w1),
                     (REC_W1, w2), (REC_RANK0, rank1), (REC_RANK1, rank2)):
        rec = jnp.where(lane == pos, val, rec)
    return rec, seen


def _split_bf16(x):
    hi = x.astype(BF16)
    lo = (x - hi.astype(F32)).astype(BF16)
    return hi, lo


def _outproj_body(a_ref, c_ref, x_ref, mod_ref, g_ref, w_ref, wrh_ref, wrl_ref, br_ref,
                  xo_ref, h_ref, r_ref, sel_ref, cnt_ref, seen):
    @pl.when((pl.program_id(0) == 0) & (pl.program_id(1) == 0))
    def _():
        seen[...] = jnp.zeros_like(seen)

    da = a_ref.shape[2]
    mix = jnp.dot(a_ref[0].astype(BF16), w_ref[0, 0:da, :], preferred_element_type=F32)
    mix = mix + jnp.dot(c_ref[0], w_ref[0, da:, :], preferred_element_type=F32)
    xn = x_ref[0] + mod_ref[0, 2:3, :] * mix
    xo_ref[0] = xn
    h = _rms_modulate(xn, g_ref[...], mod_ref[0, 3:4, :], mod_ref[0, 4:5, :])
    _store_token_tiles(h_ref, h)
    h_hi, h_lo = _split_bf16(h)
    logits = (jnp.dot(h_hi, wrh_ref[0], preferred_element_type=F32)
              + jnp.dot(h_lo, wrh_ref[0], preferred_element_type=F32)
              + jnp.dot(h_hi, wrl_ref[0], preferred_element_type=F32)) + br_ref[0]
    rec, new_seen = _route(logits, seen[...])
    r_ref[0] = rec
    sel_ref[...] = rec.T[0:SUBLANES, :]
    seen[...] = new_seen
    cnt_ref[...] = new_seen


def _output_projection(attn, conv, x, mod, g, w_out_bf, wr_hi, wr_lo, b_router, layer):
    b, s, d = x.shape
    ts = SEQ_TILE
    tile = lambda n: pl.BlockSpec((1, ts, n), lambda i, j: (i, j, 0))
    const = lambda r, c: pl.BlockSpec((r, c), lambda i, j: (0, 0))
    per_layer = lambda r, c: pl.BlockSpec((1, r, c), lambda i, j: (layer, 0, 0))
    return pl.pallas_call(
        _outproj_body,
        grid=(b, s // ts),
        in_specs=[tile(attn.shape[2]), tile(conv.shape[2]), tile(d),
                  pl.BlockSpec((1, N_MOD, d), lambda i, j: (i, 0, 0)),
                  const(1, d), per_layer(d, d), per_layer(d, ROUTE_LANES),
                  per_layer(d, ROUTE_LANES), per_layer(1, ROUTE_LANES)],
        out_specs=[tile(d), pl.BlockSpec((ts * SUBLANES, LANES), lambda i, j: (i * (s // ts) + j, 0)),
                   tile(ROUTE_LANES),
                   pl.BlockSpec((SUBLANES, ts), lambda i, j: (0, i * (s // ts) + j)),
                   const(1, ROUTE_LANES)],
        out_shape=[jax.ShapeDtypeStruct((b, s, d), F32),
                   jax.ShapeDtypeStruct((b * s * SUBLANES, LANES), F32),
                   jax.ShapeDtypeStruct((b, s, ROUTE_LANES), F32),
                   jax.ShapeDtypeStruct((SUBLANES, b * s), F32),
                   jax.ShapeDtypeStruct((1, ROUTE_LANES), F32)],
        scratch_shapes=[pltpu.VMEM((1, ROUTE_LANES), F32)],
        compiler_params=_cparams(("arbitrary", "arbitrary")),
        name="output_projection",
    )(attn, conv, x, mod, g.reshape(1, d), w_out_bf, wr_hi, wr_lo, b_router)


def _load_token_tiles(ref):
    n = ref.shape[0] // SUBLANES
    return jnp.concatenate([ref[pl.ds(a, n, stride=SUBLANES), :] for a in range(SUBLANES)], axis=1)


def _store_token_tiles(ref, val):
    n = val.shape[0]
    for a in range(SUBLANES):
        ref[pl.ds(a, n, stride=SUBLANES), :] = val[:, a * LANES:(a + 1) * LANES]


def _tile_of(ref, token):
    return ref.at[pl.ds(pl.multiple_of(token * SUBLANES, SUBLANES), SUBLANES)]


def _row_copy(src, src_token, dst, dst_token, sem):
    return pltpu.make_async_copy(_tile_of(src, src_token), _tile_of(dst, dst_token), sem)


def _wait_rows(like, n_tokens, sem):
    n = n_tokens * SUBLANES
    pltpu.make_async_copy(like.at[pl.ds(0, n)], like.at[pl.ds(0, n)], sem).wait()


def _dispatch_body(dest_ref, h_ref, xb_hbm, sem):
    td = h_ref.shape[0] // SUBLANES
    n_tok = dest_ref.shape[0] // 2
    base = pl.program_id(0) * td

    def issue(t, carry):
        for choice in range(2):
            _row_copy(h_ref, t, xb_hbm, dest_ref[choice * n_tok + base + t], sem).start()
        return carry

    lax.fori_loop(0, td, issue, 0, unroll=DMA_UNROLL)
    _wait_rows(xb_hbm, 2 * td, sem)


def _dispatch(h, dest, n_rows):
    t = h.shape[0] // SUBLANES
    td = DISPATCH_TILE
    return pl.pallas_call(
        _dispatch_body,
        grid_spec=pltpu.PrefetchScalarGridSpec(
            num_scalar_prefetch=1,
            grid=(t // td,),
            in_specs=[pl.BlockSpec((td * SUBLANES, LANES), lambda i, *_: (i, 0))],
            out_specs=pl.BlockSpec(memory_space=pl.ANY),
            scratch_shapes=[pltpu.SemaphoreType.DMA(())]),
        out_shape=jax.ShapeDtypeStruct((n_rows * SUBLANES, LANES), h.dtype),
        compiler_params=_cparams(("arbitrary",)),
        name="moe_dispatch",
    )(dest, h)


def _expert_body(eid_ref, valid_ref, nused_ref, x_ref, wg_ref, wu_ref, wd_ref, o_ref,
                 wg_bf, wu_bf, wd_bf):
    i = pl.program_id(0)

    @pl.when(i < nused_ref[0])
    def _():
        changed = (i == 0) | (eid_ref[i] != eid_ref[jnp.maximum(i - 1, 0)])

        @pl.when(changed)
        def _():
            wg_bf[...] = wg_ref[0, 0].astype(BF16)
            wu_bf[...] = wu_ref[0, 0].astype(BF16)
            wd_bf[...] = wd_ref[0, 0].astype(BF16)

        x = _load_token_tiles(x_ref)
        row = lax.broadcasted_iota(jnp.int32, x.shape, 0)
        x = jnp.where(row < valid_ref[i], x, 0.0).astype(BF16)
        g = jnp.dot(x, wg_bf[...], preferred_element_type=F32)
        u = jnp.dot(x, wu_bf[...], preferred_element_type=F32)
        h = (g * jax.nn.sigmoid(g)) * u
        _store_token_tiles(o_ref, jnp.dot(h.astype(BF16), wd_bf[...], preferred_element_type=F32))

    @pl.when(i >= nused_ref[0])
    def _():
        o_ref[...] = jnp.zeros_like(o_ref)


def _experts(xb, blk_eid, blk_valid, n_used, w_gate, w_up, w_down, layer):
    p = xb.shape[0] // SUBLANES
    d, de = w_gate.shape[2:]
    tm = MOE_BLOCK
    last = lambda i, nu: jnp.minimum(i, nu[0] - 1)
    rows = pl.BlockSpec((tm * SUBLANES, LANES), lambda i, eid, nv, nu: (last(i, nu), 0))
    weight = lambda r, c: pl.BlockSpec(
        (1, 1, r, c), lambda i, eid, nv, nu: (layer, eid[last(i, nu)], 0, 0))
    return pl.pallas_call(
        _expert_body,
        grid_spec=pltpu.PrefetchScalarGridSpec(
            num_scalar_prefetch=3,
            grid=(p // tm,),
            in_specs=[rows, weight(d, de), weight(d, de), weight(de, d)],
            out_specs=pl.BlockSpec((tm * SUBLANES, LANES), lambda i, eid, nv, nu: (i, 0)),
            scratch_shapes=[pltpu.VMEM((d, de), BF16), pltpu.VMEM((d, de), BF16),
                            pltpu.VMEM((de, d), BF16)]),
        out_shape=jax.ShapeDtypeStruct((p * SUBLANES, LANES), F32),
        compiler_params=_cparams(("arbitrary",)),
        name="moe_experts",
    )(blk_eid, blk_valid, n_used, xb, w_gate, w_up, w_down)


def _gather_rows(dest_ref, yb_hbm, ybuf, sem, tile, into, n_tokens, unroll):
    n_tok = dest_ref.shape[0] // 2

    def issue(t, carry):
        for choice in range(2):
            src = dest_ref[choice * n_tok + tile * n_tokens + t]
            _row_copy(yb_hbm, src, ybuf.at[into, choice], t, sem.at[into, choice]).start()
        return carry

    lax.fori_loop(0, n_tokens, issue, 0, unroll=unroll)


def _gathered_moe(r_ref, ybuf, sem, slot, n_tokens):
    for choice in range(2):
        _wait_rows(ybuf.at[slot, choice], n_tokens, sem.at[slot, choice])
    return (r_ref[:, REC_W0:REC_W0 + 1] * _load_token_tiles(ybuf.at[slot, 0])
            + r_ref[:, REC_W1:REC_W1 + 1] * _load_token_tiles(ybuf.at[slot, 1]))


def _combine_final_body(dest_ref, x_ref, r_ref, mod_ref, g_ref, yb_hbm, o_ref, ybuf, sem):
    tcm = x_ref.shape[0]
    i = pl.program_id(0)
    slot = i % 2

    @pl.when(i == 0)
    def _():
        _gather_rows(dest_ref, yb_hbm, ybuf, sem, 0, 0, tcm, DMA_UNROLL)

    @pl.when(i + 1 < pl.num_programs(0))
    def _():
        _gather_rows(dest_ref, yb_hbm, ybuf, sem, i + 1, 1 - slot, tcm, DMA_UNROLL)

    xn = x_ref[...] + mod_ref[0, 5:6, :] * _gathered_moe(r_ref, ybuf, sem, slot, tcm)
    ms = jnp.mean(xn * xn, axis=-1, keepdims=True)
    o_ref[...] = (xn * lax.rsqrt(ms + NORM_EPS)) * g_ref[...]


def _combine_final(x, route, mod, g_final, yb, dest, seq_len):
    t, d = x.shape
    tcm = COMBINE_TILE
    per_seq = seq_len // tcm
    return pl.pallas_call(
        _combine_final_body,
        grid_spec=pltpu.PrefetchScalarGridSpec(
            num_scalar_prefetch=1,
            grid=(t // tcm,),
            in_specs=[pl.BlockSpec((tcm, d), lambda i, *_: (i, 0)),
                      pl.BlockSpec((tcm, ROUTE_LANES), lambda i, *_: (i, 0)),
                      pl.BlockSpec((1, N_MOD, d), lambda i, *_: (i // per_seq, 0, 0)),
                      pl.BlockSpec((1, d), lambda i, *_: (0, 0)),
                      pl.BlockSpec(memory_space=pl.ANY)],
            out_specs=pl.BlockSpec((tcm, d), lambda i, *_: (i, 0)),
            scratch_shapes=[pltpu.VMEM((2, 2, tcm * SUBLANES, LANES), F32),
                            pltpu.SemaphoreType.DMA((2, 2))]),
        out_shape=jax.ShapeDtypeStruct((t, d), F32),
        compiler_params=_cparams(("arbitrary",)),
        name="moe_combine_final",
    )(dest, x, route, mod, g_final.reshape(1, d), yb)


def _combine_inproj_body(dest_ref, x_ref, r_ref, modp_ref, mod_ref, g_ref, w_ref, yb_hbm,
                         xo_ref, q_ref, k_ref, v_ref, u_ref, ybuf, sem):
    ts = x_ref.shape[1]
    per_seq = pl.num_programs(1)
    n_tiles = pl.num_programs(0) * per_seq
    i = pl.program_id(0) * per_seq + pl.program_id(1)
    slot = i % 2

    @pl.when(i == 0)
    def _():
        _gather_rows(dest_ref, yb_hbm, ybuf, sem, 0, 0, ts, DMA_UNROLL)

    moe = _gathered_moe(r_ref, ybuf, sem, slot, ts)
    _gather_rows(dest_ref, yb_hbm, ybuf, sem, jnp.minimum(i + 1, n_tiles - 1), 1 - slot, ts, True)
    xn = x_ref[0] + modp_ref[0, 5:6, :] * moe
    xo_ref[0] = xn
    _project_in(xn, mod_ref, g_ref, w_ref, q_ref, k_ref, v_ref, u_ref)

    @pl.when(i == n_tiles - 1)
    def _():
        for choice in range(2):
            _wait_rows(ybuf.at[1 - slot, choice], ts, sem.at[1 - slot, choice])


def _combine_input_projection(x, route, mod_prev, yb, dest, mod, g, w_in_bf, layer):
    b, s, d = x.shape
    din = w_in_bf.shape[2]
    dc = (din - 3 * D_ATTN) // 2
    ts = SEQ_TILE
    per_seq = s // ts
    tile = lambda n: pl.BlockSpec((1, ts, n), lambda i, j, *_: (i, j, 0))
    mods = pl.BlockSpec((1, N_MOD, d), lambda i, j, *_: (i, 0, 0))
    return pl.pallas_call(
        _combine_inproj_body,
        grid_spec=pltpu.PrefetchScalarGridSpec(
            num_scalar_prefetch=1,
            grid=(b, per_seq),
            in_specs=[tile(d),
                      pl.BlockSpec((ts, ROUTE_LANES), lambda i, j, *_: (i * per_seq + j, 0)),
                      mods, mods,
                      pl.BlockSpec((1, d), lambda i, j, *_: (0, 0)),
                      pl.BlockSpec((1, d, din), lambda i, j, *_: (layer, 0, 0)),
                      pl.BlockSpec(memory_space=pl.ANY)],
            out_specs=[tile(d), tile(D_ATTN), tile(D_ATTN), tile(D_ATTN), tile(dc)],
            scratch_shapes=[pltpu.VMEM((2, 2, ts * SUBLANES, LANES), F32),
                            pltpu.SemaphoreType.DMA((2, 2))]),
        out_shape=[jax.ShapeDtypeStruct((b, s, d), F32)]
        + [jax.ShapeDtypeStruct((b, s, D_ATTN), BF16)] * 3
        + [jax.ShapeDtypeStruct((b, s, dc), BF16)],
        compiler_params=_cparams(("arbitrary", "arbitrary")),
        name="combine_input_projection",
    )(dest, x, route, mod_prev, mod, g.reshape(1, d), w_in_bf, yb)


def _plan_blocks(counts, n_blocks):
    tm = MOE_BLOCK
    padded = ((counts + tm - 1) // tm) * tm
    pad_end = jnp.cumsum(padded)
    pad_start = pad_end - padded
    blk_row = jnp.arange(n_blocks, dtype=jnp.int32) * tm
    blk_eid = jnp.minimum(jnp.sum(pad_end[None, :] <= blk_row[:, None], axis=1), N_EXPERTS - 1)
    blk_valid = jnp.clip(pad_start[blk_eid] + counts[blk_eid] - blk_row, 0, tm)
    n_used = pad_end[-1:] // tm
    i32 = lambda a: a.astype(jnp.int32)
    return i32(pad_start), i32(blk_eid), i32(blk_valid), i32(n_used)


def kernel(x, c, w_mod, b_mod, g_norm1, w_in, conv_w, conv_b, conv_ln_g, conv_ln_b, w_out, g_norm2, w_router_group, b_router_group, w_router_expert, b_router_expert, w_exp_gate, w_exp_up, w_exp_down, g_final):
    b, s, d = x.shape
    depth = w_mod.shape[0]
    t = b * s
    assert d == SUBLANES * LANES
    assert s % (ATTN_BLOCK * max(DILATIONS)) == 0 and s % SEQ_TILE == 0
    assert t % DISPATCH_TILE == 0 and s % COMBINE_TILE == 0
    n_blocks = -(-2 * t // MOE_BLOCK) + N_EXPERTS
    n_rows = n_blocks * MOE_BLOCK

    mod_all = _modulation(c, w_mod, b_mod).reshape(depth, b, N_MOD, d)
    bias = _attn_bias_table()
    pad = ROUTE_LANES - N_GROUPS - N_EXPERTS
    w_router = jnp.concatenate(
        [w_router_group, w_router_expert, jnp.zeros((depth, d, pad), F32)], axis=-1)
    b_router = jnp.concatenate(
        [b_router_group, b_router_expert, jnp.zeros((depth, pad), F32)], axis=-1)
    b_router = b_router.reshape(depth, 1, ROUTE_LANES)
    wr_hi = w_router.astype(BF16)
    wr_lo = (w_router - wr_hi.astype(F32)).astype(BF16)
    w_in_bf = w_in.astype(BF16)
    w_out_bf = w_out.astype(BF16)

    pending = None
    for l in range(depth):
        mod = mod_all[l]
        if pending is None:
            q, k, v, u = _input_projection(x, mod, g_norm1[l], w_in_bf, l)
        else:
            x, q, k, v, u = _combine_input_projection(x, *pending, mod, g_norm1[l], w_in_bf, l)
        attn = _dilated_attention(q, k, v, bias)
        conv = _conformer_conv(u, conv_w[l], conv_b[l], conv_ln_g[l], conv_ln_b[l])
        x, h2, route, sel, seen = _output_projection(attn, conv, x, mod, g_norm2[l], w_out_bf,
                                                     wr_hi, wr_lo, b_router, l)
        route = route.reshape(t, ROUTE_LANES)
        counts = seen[0, N_GROUPS:N_GROUPS + N_EXPERTS].astype(jnp.int32)
        starts, blk_eid, blk_valid, n_used = _plan_blocks(counts, n_blocks)
        eid = sel[REC_LANE0:REC_LANE1 + 1].astype(jnp.int32) - N_GROUPS
        ranks = sel[REC_RANK0:REC_RANK1 + 1].astype(jnp.int32)
        onehot = eid[None] == jnp.arange(N_EXPERTS, dtype=jnp.int32)[:, None, None]
        dest = (jnp.sum(jnp.where(onehot, starts[:, None, None], 0), axis=0) + ranks).reshape(-1)
        xb = _dispatch(h2, dest, n_rows)
        yb = _experts(xb, blk_eid, blk_valid, n_used, w_exp_gate, w_exp_up, w_exp_down, l)
        pending = (route, mod, yb, dest)
    route, mod, yb, dest = pending
    return _combine_final(x.reshape(t, d), route, mod, g_final, yb, dest, s).reshape(b, s, d)
```

```python
import functools

import numpy as np
import jax
import jax.numpy as jnp
from jax import lax
from jax.experimental import pallas as pl
from jax.experimental.pallas import tpu as pltpu

F32 = jnp.float32
BF16 = jnp.bfloat16

SUBLANES = 8
LANES = 128
HEAD_DIM = 64
N_HEADS = 8
D_ATTN = N_HEADS * HEAD_DIM
CONV_WIDTH = 31
DILATIONS = (1, 4, 16)
ATTN_BLOCK = 128
ATTN_UNROLL = 10
DEINT = 4
N_GROUPS = 4
EXPERTS_PER_GROUP = 8
N_EXPERTS = N_GROUPS * EXPERTS_PER_GROUP
NORM_EPS = 1e-6
N_MOD = 6
MASK_VALUE = -1e30
ROUTE_LANES = 128
REC_LANE0, REC_LANE1, REC_W0, REC_W1, REC_RANK0, REC_RANK1 = range(6)

SEQ_TILE = 512
CONV_HALO = 32
CONV_ROWS = 64
MOE_BLOCK = 512
DISPATCH_TILE = 512
COMBINE_TILE = 512
DMA_UNROLL = 8
VMEM_LIMIT = 56 * 1024 * 1024


def _cparams(sem):
    return pltpu.CompilerParams(dimension_semantics=sem, vmem_limit_bytes=VMEM_LIMIT)


def _mod_body(c_ref, w_ref, b_ref, o_ref):
    c = c_ref[...]
    c_act = c * jax.nn.sigmoid(c)
    o_ref[0] = jnp.dot(c_act, w_ref[0], preferred_element_type=F32,
                       precision=lax.Precision.HIGHEST) + b_ref[0]


def _modulation(c, w_mod, b_mod):
    depth, d, dm = w_mod.shape
    b = c.shape[0]
    nt = dm // d
    return pl.pallas_call(
        _mod_body,
        grid=(depth, nt),
        in_specs=[pl.BlockSpec((b, d), lambda l, j: (0, 0)),
                  pl.BlockSpec((1, d, d), lambda l, j: (l, 0, j)),
                  pl.BlockSpec((1, 1, d), lambda l, j: (l, 0, j))],
        out_specs=pl.BlockSpec((1, b, d), lambda l, j: (l, 0, j)),
        out_shape=jax.ShapeDtypeStruct((depth, b, dm), F32),
        compiler_params=_cparams(("arbitrary", "arbitrary")),
        name="modulation",
    )(c, w_mod, b_mod.reshape(depth, 1, dm))


def _rms_modulate(x, g, shift, scale):
    ms = jnp.mean(x * x, axis=-1, keepdims=True)
    return (x * lax.rsqrt(ms + NORM_EPS)) * g * (1.0 + scale) + shift


def _project_in(x, mod_ref, g_ref, w_ref, q_ref, k_ref, v_ref, u_ref):
    h = _rms_modulate(x, g_ref[...], mod_ref[0, 0:1, :], mod_ref[0, 1:2, :])
    y = jnp.dot(h.astype(BF16), w_ref[0], preferred_element_type=F32)
    da = D_ATTN
    q_ref[0] = (y[:, 0:da] * (HEAD_DIM ** -0.5)).astype(q_ref.dtype)
    k_ref[0] = y[:, da:2 * da].astype(k_ref.dtype)
    v_ref[0] = y[:, 2 * da:3 * da].astype(v_ref.dtype)
    dc = (y.shape[1] - 3 * da) // 2
    a = y[:, 3 * da:3 * da + dc]
    b = y[:, 3 * da + dc:]
    u_ref[0] = (a * jax.nn.sigmoid(b)).astype(u_ref.dtype)


def _inproj_body(x_ref, mod_ref, g_ref, w_ref, q_ref, k_ref, v_ref, u_ref):
    _project_in(x_ref[0], mod_ref, g_ref, w_ref, q_ref, k_ref, v_ref, u_ref)


def _input_projection(x, mod, g, w_in_bf, layer):
    b, s, d = x.shape
    din = w_in_bf.shape[2]
    dc = (din - 3 * D_ATTN) // 2
    ts = SEQ_TILE
    tile = lambda n: pl.BlockSpec((1, ts, n), lambda i, j: (i, j, 0))
    return pl.pallas_call(
        _inproj_body,
        grid=(b, s // ts),
        in_specs=[tile(d),
                  pl.BlockSpec((1, N_MOD, d), lambda i, j: (i, 0, 0)),
                  pl.BlockSpec((1, d), lambda i, j: (0, 0)),
                  pl.BlockSpec((1, d, din), lambda i, j: (layer, 0, 0))],
        out_specs=[tile(D_ATTN), tile(D_ATTN), tile(D_ATTN), tile(dc)],
        out_shape=[jax.ShapeDtypeStruct((b, s, D_ATTN), BF16)] * 3
        + [jax.ShapeDtypeStruct((b, s, dc), BF16)],
        compiler_params=_cparams(("arbitrary", "arbitrary")),
        name="input_projection",
    )(x, mod, g.reshape(1, d), w_in_bf)


def _attn_bias_table():
    blk = ATTN_BLOCK
    slopes = jnp.asarray(2.0 ** (-8.0 * np.arange(1, N_HEADS + 1) / N_HEADS), dtype=F32)
    qi = np.arange(blk)[:, None] + blk
    ki = np.arange(2 * blk)[None, :]
    delta = qi - ki
    in_band = (delta >= 0) & (delta <= blk)
    tables = []
    for dil in DILATIONS:
        bias = -slopes[:, None, None] * jnp.asarray(delta * dil, dtype=F32)
        later = jnp.where(in_band[None], bias, MASK_VALUE)
        first = jnp.where((in_band & (ki >= blk))[None], bias, MASK_VALUE)
        tables.append(jnp.stack([later, first], axis=1))
    t = jnp.stack(tables, axis=1)
    return t.reshape(N_HEADS // 2, 2, len(DILATIONS), 2, blk, 2 * blk)


def _attn_body(q_ref, k_ref, v_ref, bias_ref, o_ref, xf, x4, acc_o, acc_l, acc_m, s_buf, p_buf):
    blk = ATTN_BLOCK
    pw = 2 * HEAD_DIM
    s_len = q_ref.shape[1]
    seg = s_len // DEINT
    left = lax.broadcasted_iota(jnp.int32, (blk, pw), 1) < HEAD_DIM
    trans_b = (((1,), (1,)), ((), ()))
    n_branch = len(DILATIONS)
    n_blocks = s_len // blk
    natural = (q_ref.at[0], k_ref.at[0], v_ref.at[0])
    deint = (x4.at[0], x4.at[1], x4.at[2])

    for t, ref in enumerate(natural):
        xf[...] = ref[...].astype(F32)
        for r in range(DEINT):
            x4[t, r * seg:(r + 1) * seg, :] = xf[pl.ds(r, seg, stride=DEINT), :]

    for br, dil in enumerate(DILATIONS):
        nb = s_len // (blk * dil)
        step = dil // DEINT if dil > DEINT else 1
        srcs = natural if dil == 1 else deint

        def starts(j, dil=dil, nb=nb, step=step):
            r = j // nb
            n = j % nb
            if dil == 1:
                base = 0
            else:
                base = (r % DEINT) * seg + r // DEINT
            qs = base + n * (blk * step)
            ks = base + jnp.maximum(n - 1, 0) * (blk * step)
            return qs, ks, jnp.where(n == 0, 1, 0)

        def span(start, step=step):
            return pl.ds(start, blk) if step == 1 else pl.ds(start, blk, stride=step)

        def both(ref, qs, ks):
            return jnp.concatenate([ref[span(ks), :], ref[span(qs), :]], axis=0).astype(BF16)

        def scores(j, slot, br=br, srcs=srcs):
            qs, ks, first = starts(j)
            q = srcs[0][span(qs), :].astype(BF16)
            kk = both(srcs[1], qs, ks)
            zero = jnp.zeros_like(q)
            q2 = jnp.concatenate([jnp.where(left, q, zero), jnp.where(left, zero, q)], axis=0)
            s = lax.dot_general(q2, kk, trans_b, preferred_element_type=F32)
            s_buf[slot] = s.reshape(2, blk, 2 * blk) + bias_ref[0, :, br, first]

        def softmax(j, slot, br=br):
            qs, _, _ = starts(j)
            maxes = []
            for hh in range(2):
                s = s_buf[slot, hh]
                m = jnp.max(s, axis=-1, keepdims=True)
                p_buf[slot, hh] = jnp.exp(s - m).astype(BF16)
                maxes.append(m)
            acc_m[br, span(qs), :] = jnp.where(left, maxes[0], maxes[1])

        def values(j, slot, br=br, srcs=srcs):
            qs, ks, _ = starts(j)
            vv = both(srcs[2], qs, ks)
            rhs = jnp.concatenate([vv, jnp.ones_like(vv)], axis=1)
            o = jnp.dot(p_buf[slot].reshape(2 * blk, 2 * blk), rhs, preferred_element_type=F32)
            acc_o[br, span(qs), :] = jnp.where(left, o[0:blk, 0:pw], o[blk:, 0:pw])
            acc_l[br, span(qs), :] = jnp.where(left, o[0:blk, pw:], o[blk:, pw:])

        nslot = ATTN_UNROLL
        scores(0, 0)
        softmax(0, 0)
        scores(1, 1)

        def steady(i, carry):
            for u in range(ATTN_UNROLL):
                j = 2 + ATTN_UNROLL * i + u
                values(j - 2, u % nslot)
                softmax(j - 1, (u + 1) % nslot)
                scores(j, (u + 2) % nslot)
            return carry

        trips = (n_blocks - 2) // ATTN_UNROLL
        lax.fori_loop(0, trips, steady, 0)
        for j in range(2 + ATTN_UNROLL * trips, n_blocks):
            values(j - 2, (j - 2) % nslot)
            softmax(j - 1, (j - 1) % nslot)
            scores(j, j % nslot)
        values(n_blocks - 2, (n_blocks - 2) % nslot)
        softmax(n_blocks - 1, (n_blocks - 1) % nslot)
        values(n_blocks - 1, (n_blocks - 1) % nslot)

    rows = 256

    def merge(c, carry):
        idx = pl.multiple_of(c * rows, rows)
        r = idx // seg
        nat = pl.ds(DEINT * (idx - r * seg) + r, rows, stride=DEINT)
        spans = [nat if dil == 1 else pl.ds(idx, rows) for dil in DILATIONS]
        ms = [acc_m[br, spans[br], :] for br in range(n_branch)]
        m_max = functools.reduce(jnp.maximum, ms)
        num = 0.0
        den = 0.0
        for br in range(n_branch):
            a = jnp.exp(ms[br] - m_max)
            num = num + a * acc_o[br, spans[br], :]
            den = den + a * acc_l[br, spans[br], :]
        o_ref[0, nat, :] = num / den
        return carry

    lax.fori_loop(0, s_len // rows, merge, 0, unroll=2)


def _dilated_attention(q, k, v, bias):
    b, s, da = q.shape
    pw = 2 * HEAD_DIM
    assert DILATIONS == (1, DEINT, DEINT * DEINT)
    seq = pl.BlockSpec((1, s, pw), lambda i, j: (i, 0, j))
    nbr = len(DILATIONS)
    stage = (ATTN_UNROLL, 2, ATTN_BLOCK, 2 * ATTN_BLOCK)
    return pl.pallas_call(
        _attn_body,
        grid=(b, da // pw),
        in_specs=[seq, seq, seq,
                  pl.BlockSpec((1,) + bias.shape[1:], lambda i, j: (j, 0, 0, 0, 0, 0))],
        out_specs=seq,
        out_shape=jax.ShapeDtypeStruct((b, s, da), F32),
        scratch_shapes=[pltpu.VMEM((s, pw), F32), pltpu.VMEM((3, s, pw), F32)]
        + [pltpu.VMEM((nbr, s, pw), F32)] * 3
        + [pltpu.VMEM(stage, F32), pltpu.VMEM(stage, BF16)],
        compiler_params=_cparams(("arbitrary", "arbitrary")),
        name="dilated_attention",
    )(q, k, v, bias)


def _conv_body(uc_ref, uh_ref, w_ref, cb_ref, lg_ref, lb_ref, o_ref, win, shifted):
    t = pl.program_id(1)
    tc = uc_ref.shape[1]
    win[0:CONV_HALO, :] = jnp.where(t > 0, uh_ref[0].astype(F32), 0.0)
    win[CONV_HALO:, :] = uc_ref[0].astype(F32)
    off = CONV_HALO - (CONV_WIDTH - 1)
    for b in range(SUBLANES):
        n_rows = tc + SUBLANES * ((CONV_WIDTH - 1 - b) // SUBLANES)
        shifted[b, 0:n_rows, :] = win[off + b:off + b + n_rows, :]
    for rb in range(tc // CONV_ROWS):
        r0 = rb * CONV_ROWS
        acc = jnp.zeros((CONV_ROWS, uc_ref.shape[2]), F32)
        for j in range(CONV_WIDTH):
            a, b = divmod(j, SUBLANES)
            rows = pl.ds(r0 + SUBLANES * a, CONV_ROWS)
            acc = acc + w_ref[j:j + 1, :] * shifted[b, rows, :]
        y = acc + cb_ref[...]
        mu = jnp.mean(y, axis=-1, keepdims=True)
        yc = y - mu
        var = jnp.mean(yc * yc, axis=-1, keepdims=True)
        z = yc * lax.rsqrt(var + NORM_EPS) * lg_ref[...] + lb_ref[...]
        o_ref[0, r0:r0 + CONV_ROWS, :] = (z * jax.nn.sigmoid(z)).astype(o_ref.dtype)


def _conformer_conv(u, conv_w, conv_b, ln_g, ln_b):
    b, s, dc = u.shape
    tc = SEQ_TILE
    per = tc // CONV_HALO
    row = pl.BlockSpec((1, dc), lambda i, j: (0, 0))
    return pl.pallas_call(
        _conv_body,
        grid=(b, s // tc),
        in_specs=[pl.BlockSpec((1, tc, dc), lambda i, j: (i, j, 0)),
                  pl.BlockSpec((1, CONV_HALO, dc), lambda i, j: (i, jnp.maximum(j * per - 1, 0), 0)),
                  pl.BlockSpec((CONV_WIDTH, dc), lambda i, j: (0, 0)),
                  row, row, row],
        out_specs=pl.BlockSpec((1, tc, dc), lambda i, j: (i, j, 0)),
        out_shape=jax.ShapeDtypeStruct((b, s, dc), BF16),
        scratch_shapes=[pltpu.VMEM((tc + CONV_HALO, dc), F32),
                        pltpu.VMEM((SUBLANES, tc + CONV_HALO - SUBLANES, dc), F32)],
        compiler_params=_cparams(("arbitrary", "arbitrary")),
        name="conformer_conv",
    )(u, u, conv_w, conv_b.reshape(1, dc), ln_g.reshape(1, dc), ln_b.reshape(1, dc))


def _route(logits, seen):
    rows = logits.shape[0]
    lane = lax.broadcasted_iota(jnp.int32, logits.shape, 1)
    big = jnp.int32(1 << 20)
    is_g = lane < N_GROUPS
    gl = jnp.where(is_g, logits, MASK_VALUE)
    gmax = jnp.max(gl, axis=-1, keepdims=True)
    grp = jnp.min(jnp.where(gl == gmax, lane, big), axis=-1, keepdims=True)
    gsum = jnp.sum(jnp.where(is_g, jnp.exp(gl - gmax), 0.0), axis=-1, keepdims=True)
    p_grp = 1.0 / gsum
    lo = N_GROUPS + grp * EXPERTS_PER_GROUP
    el = jnp.where((lane >= lo) & (lane < lo + EXPERTS_PER_GROUP), logits, MASK_VALUE)
    v1 = jnp.max(el, axis=-1, keepdims=True)
    i1 = jnp.min(jnp.where(el == v1, lane, big), axis=-1, keepdims=True)
    el2 = jnp.where(lane == i1, MASK_VALUE, el)
    v2 = jnp.max(el2, axis=-1, keepdims=True)
    i2 = jnp.min(jnp.where(el2 == v2, lane, big), axis=-1, keepdims=True)
    e = jnp.exp(v2 - v1)
    w1 = p_grp / (1.0 + e)
    w2 = p_grp * e / (1.0 + e)

    hit1 = lane == i1
    hit2 = lane == i2
    chosen = jnp.where(hit1 | hit2, 1.0, 0.0)
    ri = lax.broadcasted_iota(jnp.int32, (rows, rows), 0)
    ci = lax.broadcasted_iota(jnp.int32, (rows, rows), 1)
    earlier = jnp.where(ri > ci, 1.0, 0.0).astype(BF16)
    before = jnp.dot(earlier, chosen.astype(BF16), preferred_element_type=F32) + seen
    rank1 = jnp.sum(jnp.where(hit1, before, 0.0), axis=-1, keepdims=True)
    rank2 = jnp.sum(jnp.where(hit2, before, 0.0), axis=-1, keepdims=True)
    seen = seen + jnp.sum(chosen, axis=0, keepdims=True)

    rec = jnp.zeros(logits.shape, F32)
    for pos, val in ((REC_LANE0, i1.astype(F32)), (REC_LANE1, i2.astype(F32)), (REC_W0, w1),
                     (REC_W1, w2), (REC_RANK0, rank1), (REC_RANK1, rank2)):
        rec = jnp.where(lane == pos, val, rec)
    return rec, seen


def _split_bf16(x):
    hi = x.astype(BF16)
    lo = (x - hi.astype(F32)).astype(BF16)
    return hi, lo


def _outproj_body(a_ref, c_ref, x_ref, mod_ref, g_ref, w_ref, wrh_ref, wrl_ref, br_ref,
                  xo_ref, h_ref, r_ref, sel_ref, cnt_ref, seen):
    @pl.when((pl.program_id(0) == 0) & (pl.program_id(1) == 0))
    def _():
        seen[...] = jnp.zeros_like(seen)

    da = a_ref.shape[2]
    mix = jnp.dot(a_ref[0].astype(BF16), w_ref[0, 0:da, :], preferred_element_type=F32)
    mix = mix + jnp.dot(c_ref[0], w_ref[0, da:, :], preferred_element_type=F32)
    xn = x_ref[0] + mod_ref[0, 2:3, :] * mix
    xo_ref[0] = xn
    h = _rms_modulate(xn, g_ref[...], mod_ref[0, 3:4, :], mod_ref[0, 4:5, :])
    _store_token_tiles(h_ref, h)
    h_hi, h_lo = _split_bf16(h)
    logits = (jnp.dot(h_hi, wrh_ref[0], preferred_element_type=F32)
              + jnp.dot(h_lo, wrh_ref[0], preferred_element_type=F32)
              + jnp.dot(h_hi, wrl_ref[0], preferred_element_type=F32)) + br_ref[0]
    rec, new_seen = _route(logits, seen[...])
    r_ref[0] = rec
    sel_ref[...] = rec.T[0:SUBLANES, :]
    seen[...] = new_seen
    cnt_ref[...] = new_seen


def _output_projection(attn, conv, x, mod, g, w_out_bf, wr_hi, wr_lo, b_router, layer):
    b, s, d = x.shape
    ts = SEQ_TILE
    tile = lambda n: pl.BlockSpec((1, ts, n), lambda i, j: (i, j, 0))
    const = lambda r, c: pl.BlockSpec((r, c), lambda i, j: (0, 0))
    per_layer = lambda r, c: pl.BlockSpec((1, r, c), lambda i, j: (layer, 0, 0))
    return pl.pallas_call(
        _outproj_body,
        grid=(b, s // ts),
        in_specs=[tile(attn.shape[2]), tile(conv.shape[2]), tile(d),
                  pl.BlockSpec((1, N_MOD, d), lambda i, j: (i, 0, 0)),
                  const(1, d), per_layer(d, d), per_layer(d, ROUTE_LANES),
                  per_layer(d, ROUTE_LANES), per_layer(1, ROUTE_LANES)],
        out_specs=[tile(d), pl.BlockSpec((ts * SUBLANES, LANES), lambda i, j: (i * (s // ts) + j, 0)),
                   tile(ROUTE_LANES),
                   pl.BlockSpec((SUBLANES, ts), lambda i, j: (0, i * (s // ts) + j)),
                   const(1, ROUTE_LANES)],
        out_shape=[jax.ShapeDtypeStruct((b, s, d), F32),
                   jax.ShapeDtypeStruct((b * s * SUBLANES, LANES), F32),
                   jax.ShapeDtypeStruct((b, s, ROUTE_LANES), F32),
                   jax.ShapeDtypeStruct((SUBLANES, b * s), F32),
                   jax.ShapeDtypeStruct((1, ROUTE_LANES), F32)],
        scratch_shapes=[pltpu.VMEM((1, ROUTE_LANES), F32)],
        compiler_params=_cparams(("arbitrary", "arbitrary")),
        name="output_projection",
    )(attn, conv, x, mod, g.reshape(1, d), w_out_bf, wr_hi, wr_lo, b_router)


def _load_token_tiles(ref):
    n = ref.shape[0] // SUBLANES
    return jnp.concatenate([ref[pl.ds(a, n, stride=SUBLANES), :] for a in range(SUBLANES)], axis=1)


def _store_token_tiles(ref, val):
    n = val.shape[0]
    for a in range(SUBLANES):
        ref[pl.ds(a, n, stride=SUBLANES), :] = val[:, a * LANES:(a + 1) * LANES]


def _tile_of(ref, token):
    return ref.at[pl.ds(pl.multiple_of(token * SUBLANES, SUBLANES), SUBLANES)]


def _row_copy(src, src_token, dst, dst_token, sem):
    return pltpu.make_async_copy(_tile_of(src, src_token), _tile_of(dst, dst_token), sem)


def _wait_rows(like, n_tokens, sem):
    n = n_tokens * SUBLANES
    pltpu.make_async_copy(like.at[pl.ds(0, n)], like.at[pl.ds(0, n)], sem).wait()


def _dispatch_body(dest_ref, h_ref, xb_hbm, sem):
    td = h_ref.shape[0] // SUBLANES
    n_tok = dest_ref.shape[0] // 2
    base = pl.program_id(0) * td

    def issue(t, carry):
        for choice in range(2):
            _row_copy(h_ref, t, xb_hbm, dest_ref[choice * n_tok + base + t], sem).start()
        return carry

    lax.fori_loop(0, td, issue, 0, unroll=DMA_UNROLL)
    _wait_rows(xb_hbm, 2 * td, sem)


def _dispatch(h, dest, n_rows):
    t = h.shape[0] // SUBLANES
    td = DISPATCH_TILE
    return pl.pallas_call(
        _dispatch_body,
        grid_spec=pltpu.PrefetchScalarGridSpec(
            num_scalar_prefetch=1,
            grid=(t // td,),
            in_specs=[pl.BlockSpec((td * SUBLANES, LANES), lambda i, *_: (i, 0))],
            out_specs=pl.BlockSpec(memory_space=pl.ANY),
            scratch_shapes=[pltpu.SemaphoreType.DMA(())]),
        out_shape=jax.ShapeDtypeStruct((n_rows * SUBLANES, LANES), h.dtype),
        compiler_params=_cparams(("arbitrary",)),
        name="moe_dispatch",
    )(dest, h)


def _expert_body(eid_ref, valid_ref, nused_ref, x_ref, wg_ref, wu_ref, wd_ref, o_ref,
                 wg_bf, wu_bf, wd_bf):
    i = pl.program_id(0)

    @pl.when(i < nused_ref[0])
    def _():
        changed = (i == 0) | (eid_ref[i] != eid_ref[jnp.maximum(i - 1, 0)])

        @pl.when(changed)
        def _():
            wg_bf[...] = wg_ref[0, 0].astype(BF16)
            wu_bf[...] = wu_ref[0, 0].astype(BF16)
            wd_bf[...] = wd_ref[0, 0].astype(BF16)

        x = _load_token_tiles(x_ref)
        row = lax.broadcasted_iota(jnp.int32, x.shape, 0)
        x = jnp.where(row < valid_ref[i], x, 0.0).astype(BF16)
        g = jnp.dot(x, wg_bf[...], preferred_element_type=F32)
        u = jnp.dot(x, wu_bf[...], preferred_element_type=F32)
        h = (g * jax.nn.sigmoid(g)) * u
        _store_token_tiles(o_ref, jnp.dot(h.astype(BF16), wd_bf[...], preferred_element_type=F32))

    @pl.when(i >= nused_ref[0])
    def _():
        o_ref[...] = jnp.zeros_like(o_ref)


def _experts(xb, blk_eid, blk_valid, n_used, w_gate, w_up, w_down, layer):
    p = xb.shape[0] // SUBLANES
    d, de = w_gate.shape[2:]
    tm = MOE_BLOCK
    last = lambda i, nu: jnp.minimum(i, nu[0] - 1)
    rows = pl.BlockSpec((tm * SUBLANES, LANES), lambda i, eid, nv, nu: (last(i, nu), 0))
    weight = lambda r, c: pl.BlockSpec(
        (1, 1, r, c), lambda i, eid, nv, nu: (layer, eid[last(i, nu)], 0, 0))
    return pl.pallas_call(
        _expert_body,
        grid_spec=pltpu.PrefetchScalarGridSpec(
            num_scalar_prefetch=3,
            grid=(p // tm,),
            in_specs=[rows, weight(d, de), weight(d, de), weight(de, d)],
            out_specs=pl.BlockSpec((tm * SUBLANES, LANES), lambda i, eid, nv, nu: (i, 0)),
            scratch_shapes=[pltpu.VMEM((d, de), BF16), pltpu.VMEM((d, de), BF16),
                            pltpu.VMEM((de, d), BF16)]),
        out_shape=jax.ShapeDtypeStruct((p * SUBLANES, LANES), F32),
        compiler_params=_cparams(("arbitrary",)),
        name="moe_experts",
    )(blk_eid, blk_valid, n_used, xb, w_gate, w_up, w_down)


def _gather_rows(dest_ref, yb_hbm, ybuf, sem, tile, into, n_tokens, unroll):
    n_tok = dest_ref.shape[0] // 2

    def issue(t, carry):
        for choice in range(2):
            src = dest_ref[choice * n_tok + tile * n_tokens + t]
            _row_copy(yb_hbm, src, ybuf.at[into, choice], t, sem.at[into, choice]).start()
        return carry

    lax.fori_loop(0, n_tokens, issue, 0, unroll=unroll)


def _gathered_moe(r_ref, ybuf, sem, slot, n_tokens):
    for choice in range(2):
        _wait_rows(ybuf.at[slot, choice], n_tokens, sem.at[slot, choice])
    return (r_ref[:, REC_W0:REC_W0 + 1] * _load_token_tiles(ybuf.at[slot, 0])
            + r_ref[:, REC_W1:REC_W1 + 1] * _load_token_tiles(ybuf.at[slot, 1]))


def _combine_final_body(dest_ref, x_ref, r_ref, mod_ref, g_ref, yb_hbm, o_ref, ybuf, sem):
    tcm = x_ref.shape[0]
    i = pl.program_id(0)
    slot = i % 2

    @pl.when(i == 0)
    def _():
        _gather_rows(dest_ref, yb_hbm, ybuf, sem, 0, 0, tcm, DMA_UNROLL)

    @pl.when(i + 1 < pl.num_programs(0))
    def _():
        _gather_rows(dest_ref, yb_hbm, ybuf, sem, i + 1, 1 - slot, tcm, DMA_UNROLL)

    xn = x_ref[...] + mod_ref[0, 5:6, :] * _gathered_moe(r_ref, ybuf, sem, slot, tcm)
    ms = jnp.mean(xn * xn, axis=-1, keepdims=True)
    o_ref[...] = (xn * lax.rsqrt(ms + NORM_EPS)) * g_ref[...]


def _combine_final(x, route, mod, g_final, yb, dest, seq_len):
    t, d = x.shape
    tcm = COMBINE_TILE
    per_seq = seq_len // tcm
    return pl.pallas_call(
        _combine_final_body,
        grid_spec=pltpu.PrefetchScalarGridSpec(
            num_scalar_prefetch=1,
            grid=(t // tcm,),
            in_specs=[pl.BlockSpec((tcm, d), lambda i, *_: (i, 0)),
                      pl.BlockSpec((tcm, ROUTE_LANES), lambda i, *_: (i, 0)),
                      pl.BlockSpec((1, N_MOD, d), lambda i, *_: (i // per_seq, 0, 0)),
                      pl.BlockSpec((1, d), lambda i, *_: (0, 0)),
                      pl.BlockSpec(memory_space=pl.ANY)],
            out_specs=pl.BlockSpec((tcm, d), lambda i, *_: (i, 0)),
            scratch_shapes=[pltpu.VMEM((2, 2, tcm * SUBLANES, LANES), F32),
                            pltpu.SemaphoreType.DMA((2, 2))]),
        out_shape=jax.ShapeDtypeStruct((t, d), F32),
        compiler_params=_cparams(("arbitrary",)),
        name="moe_combine_final",
    )(dest, x, route, mod, g_final.reshape(1, d), yb)


def _combine_inproj_body(dest_ref, x_ref, r_ref, modp_ref, mod_ref, g_ref, w_ref, yb_hbm,
                         xo_ref, q_ref, k_ref, v_ref, u_ref, ybuf, sem):
    ts = x_ref.shape[1]
    per_seq = pl.num_programs(1)
    n_tiles = pl.num_programs(0) * per_seq
    i = pl.program_id(0) * per_seq + pl.program_id(1)
    slot = i % 2

    @pl.when(i == 0)
    def _():
        _gather_rows(dest_ref, yb_hbm, ybuf, sem, 0, 0, ts, DMA_UNROLL)

    moe = _gathered_moe(r_ref, ybuf, sem, slot, ts)
    _gather_rows(dest_ref, yb_hbm, ybuf, sem, jnp.minimum(i + 1, n_tiles - 1), 1 - slot, ts, True)
    xn = x_ref[0] + modp_ref[0, 5:6, :] * moe
    xo_ref[0] = xn
    _project_in(xn, mod_ref, g_ref, w_ref, q_ref, k_ref, v_ref, u_ref)

    @pl.when(i == n_tiles - 1)
    def _():
        for choice in range(2):
            _wait_rows(ybuf.at[1 - slot, choice], ts, sem.at[1 - slot, choice])


def _combine_input_projection(x, route, mod_prev, yb, dest, mod, g, w_in_bf, layer):
    b, s, d = x.shape
    din = w_in_bf.shape[2]
    dc = (din - 3 * D_ATTN) // 2
    ts = SEQ_TILE
    per_seq = s // ts
    tile = lambda n: pl.BlockSpec((1, ts, n), lambda i, j, *_: (i, j, 0))
    mods = pl.BlockSpec((1, N_MOD, d), lambda i, j, *_: (i, 0, 0))
    return pl.pallas_call(
        _combine_inproj_body,
        grid_spec=pltpu.PrefetchScalarGridSpec(
            num_scalar_prefetch=1,
            grid=(b, per_seq),
            in_specs=[tile(d),
                      pl.BlockSpec((ts, ROUTE_LANES), lambda i, j, *_: (i * per_seq + j, 0)),
                      mods, mods,
                      pl.BlockSpec((1, d), lambda i, j, *_: (0, 0)),
                      pl.BlockSpec((1, d, din), lambda i, j, *_: (layer, 0, 0)),
                      pl.BlockSpec(memory_space=pl.ANY)],
            out_specs=[tile(d), tile(D_ATTN), tile(D_ATTN), tile(D_ATTN), tile(dc)],
            scratch_shapes=[pltpu.VMEM((2, 2, ts * SUBLANES, LANES), F32),
                            pltpu.SemaphoreType.DMA((2, 2))]),
        out_shape=[jax.ShapeDtypeStruct((b, s, d), F32)]
        + [jax.ShapeDtypeStruct((b, s, D_ATTN), BF16)] * 3
        + [jax.ShapeDtypeStruct((b, s, dc), BF16)],
        compiler_params=_cparams(("arbitrary", "arbitrary")),
        name="combine_input_projection",
    )(dest, x, route, mod_prev, mod, g.reshape(1, d), w_in_bf, yb)


def _plan_blocks(counts, n_blocks):
    tm = MOE_BLOCK
    experts = jnp.arange(N_EXPERTS, dtype=jnp.int32)
    padded = ((counts + tm - 1) // tm) * tm
    pad_end = jnp.sum(jnp.where(experts[:, None] <= experts[None, :], padded[:, None], 0), axis=0)
    pad_start = pad_end - padded
    blk_row = jnp.arange(n_blocks, dtype=jnp.int32)[:, None] * tm
    inside = (pad_start[None, :] <= blk_row) & (blk_row < pad_end[None, :])
    blk_eid = jnp.where(jnp.any(inside, axis=1),
                        jnp.sum(jnp.where(inside, experts[None, :], 0), axis=1), N_EXPERTS - 1)
    rows_left = jnp.clip((pad_start + counts)[None, :] - blk_row, 0, tm)
    blk_valid = jnp.sum(jnp.where(inside, rows_left, 0), axis=1)
    n_used = jnp.sum(padded, keepdims=True) // tm
    i32 = lambda a: a.astype(jnp.int32)
    return i32(pad_start), i32(blk_eid), i32(blk_valid), i32(n_used)


def kernel(x, c, w_mod, b_mod, g_norm1, w_in, conv_w, conv_b, conv_ln_g, conv_ln_b, w_out, g_norm2, w_router_group, b_router_group, w_router_expert, b_router_expert, w_exp_gate, w_exp_up, w_exp_down, g_final):
    b, s, d = x.shape
    depth = w_mod.shape[0]
    t = b * s
    assert d == SUBLANES * LANES
    assert s % (ATTN_BLOCK * max(DILATIONS)) == 0 and s % SEQ_TILE == 0
    assert t % DISPATCH_TILE == 0 and s % COMBINE_TILE == 0
    n_blocks = -(-2 * t // MOE_BLOCK) + N_EXPERTS
    n_rows = n_blocks * MOE_BLOCK

    mod_all = _modulation(c, w_mod, b_mod).reshape(depth, b, N_MOD, d)
    bias = _attn_bias_table()
    pad = ROUTE_LANES - N_GROUPS - N_EXPERTS
    w_router = jnp.concatenate(
        [w_router_group, w_router_expert, jnp.zeros((depth, d, pad), F32)], axis=-1)
    b_router = jnp.concatenate(
        [b_router_group, b_router_expert, jnp.zeros((depth, pad), F32)], axis=-1)
    b_router = b_router.reshape(depth, 1, ROUTE_LANES)
    wr_hi = w_router.astype(BF16)
    wr_lo = (w_router - wr_hi.astype(F32)).astype(BF16)
    w_in_bf = w_in.astype(BF16)
    w_out_bf = w_out.astype(BF16)

    pending = None
    for l in range(depth):
        mod = mod_all[l]
        if pending is None:
            q, k, v, u = _input_projection(x, mod, g_norm1[l], w_in_bf, l)
        else:
            x, q, k, v, u = _combine_input_projection(x, *pending, mod, g_norm1[l], w_in_bf, l)
        attn = _dilated_attention(q, k, v, bias)
        conv = _conformer_conv(u, conv_w[l], conv_b[l], conv_ln_g[l], conv_ln_b[l])
        x, h2, route, sel, seen = _output_projection(attn, conv, x, mod, g_norm2[l], w_out_bf,
                                                     wr_hi, wr_lo, b_router, l)
        route = route.reshape(t, ROUTE_LANES)
        counts = seen[0, N_GROUPS:N_GROUPS + N_EXPERTS].astype(jnp.int32)
        starts, blk_eid, blk_valid, n_used = _plan_blocks(counts, n_blocks)
        eid = sel[REC_LANE0:REC_LANE1 + 1].astype(jnp.int32) - N_GROUPS
        ranks = sel[REC_RANK0:REC_RANK1 + 1].astype(jnp.int32)
        onehot = eid[None] == jnp.arange(N_EXPERTS, dtype=jnp.int32)[:, None, None]
        dest = (jnp.sum(jnp.where(onehot, starts[:, None, None], 0), axis=0) + ranks).reshape(-1)
        xb = _dispatch(h2, dest, n_rows)
        yb = _experts(xb, blk_eid, blk_valid, n_used, w_exp_gate, w_exp_up, w_exp_down, l)
        pending = (route, mod, yb, dest)
    route, mod, yb, dest = pending
    return _combine_final(x.reshape(t, d), route, mod, g_final, yb, dest, s).reshape(b, s, d)
```

```python
import functools

import numpy as np
import jax
import jax.numpy as jnp
from jax import lax
from jax.experimental import pallas as pl
from jax.experimental.pallas import tpu as pltpu

F32 = jnp.float32
BF16 = jnp.bfloat16

SUBLANES = 8
LANES = 128
HEAD_DIM = 64
N_HEADS = 8
D_ATTN = N_HEADS * HEAD_DIM
CONV_WIDTH = 31
DILATIONS = (1, 4, 16)
ATTN_BLOCK = 128
ATTN_UNROLL = 10
DEINT = 4
N_GROUPS = 4
EXPERTS_PER_GROUP = 8
N_EXPERTS = N_GROUPS * EXPERTS_PER_GROUP
NORM_EPS = 1e-6
N_MOD = 6
MASK_VALUE = -1e30
ROUTE_LANES = 128
REC_LANE0, REC_LANE1, REC_W0, REC_W1, REC_RANK0, REC_RANK1 = range(6)

SEQ_TILE = 512
CONV_HALO = 32
CONV_ROWS = 64
MOE_BLOCK = 512
CONV_TILE = 1024
DISPATCH_TILE = 2048
COMBINE_TILE = 1024
DMA_UNROLL = 8
VMEM_LIMIT = 56 * 1024 * 1024


def _cparams(sem):
    return pltpu.CompilerParams(dimension_semantics=sem, vmem_limit_bytes=VMEM_LIMIT)


def _mod_body(c_ref, w_ref, b_ref, o_ref):
    c = c_ref[...]
    c_act = c * jax.nn.sigmoid(c)
    o_ref[0] = jnp.dot(c_act, w_ref[0], preferred_element_type=F32,
                       precision=lax.Precision.HIGHEST) + b_ref[0]


def _modulation(c, w_mod, b_mod):
    depth, d, dm = w_mod.shape
    b = c.shape[0]
    nt = dm // d
    return pl.pallas_call(
        _mod_body,
        grid=(depth, nt),
        in_specs=[pl.BlockSpec((b, d), lambda l, j: (0, 0)),
                  pl.BlockSpec((1, d, d), lambda l, j: (l, 0, j)),
                  pl.BlockSpec((1, 1, d), lambda l, j: (l, 0, j))],
        out_specs=pl.BlockSpec((1, b, d), lambda l, j: (l, 0, j)),
        out_shape=jax.ShapeDtypeStruct((depth, b, dm), F32),
        compiler_params=_cparams(("arbitrary", "arbitrary")),
        name="modulation",
    )(c, w_mod, b_mod.reshape(depth, 1, dm))


def _rms_modulate(x, g, shift, scale):
    ms = jnp.mean(x * x, axis=-1, keepdims=True)
    return (x * lax.rsqrt(ms + NORM_EPS)) * g * (1.0 + scale) + shift


def _project_in(x, mod_ref, g_ref, w_ref, q_ref, k_ref, v_ref, u_ref):
    h = _rms_modulate(x, g_ref[...], mod_ref[0, 0:1, :], mod_ref[0, 1:2, :])
    y = jnp.dot(h.astype(BF16), w_ref[0], preferred_element_type=F32)
    da = D_ATTN
    q_ref[0] = (y[:, 0:da] * (HEAD_DIM ** -0.5)).astype(q_ref.dtype)
    k_ref[0] = y[:, da:2 * da].astype(k_ref.dtype)
    v_ref[0] = y[:, 2 * da:3 * da].astype(v_ref.dtype)
    dc = (y.shape[1] - 3 * da) // 2
    a = y[:, 3 * da:3 * da + dc]
    b = y[:, 3 * da + dc:]
    u_ref[0] = (a * jax.nn.sigmoid(b)).astype(u_ref.dtype)


def _inproj_body(x_ref, mod_ref, g_ref, w_ref, q_ref, k_ref, v_ref, u_ref):
    _project_in(x_ref[0], mod_ref, g_ref, w_ref, q_ref, k_ref, v_ref, u_ref)


def _input_projection(x, mod, g, w_in_bf, layer):
    b, s, d = x.shape
    din = w_in_bf.shape[2]
    dc = (din - 3 * D_ATTN) // 2
    ts = SEQ_TILE
    tile = lambda n: pl.BlockSpec((1, ts, n), lambda i, j: (i, j, 0))
    return pl.pallas_call(
        _inproj_body,
        grid=(b, s // ts),
        in_specs=[tile(d),
                  pl.BlockSpec((1, N_MOD, d), lambda i, j: (i, 0, 0)),
                  pl.BlockSpec((1, d), lambda i, j: (0, 0)),
                  pl.BlockSpec((1, d, din), lambda i, j: (layer, 0, 0))],
        out_specs=[tile(D_ATTN), tile(D_ATTN), tile(D_ATTN), tile(dc)],
        out_shape=[jax.ShapeDtypeStruct((b, s, D_ATTN), BF16)] * 3
        + [jax.ShapeDtypeStruct((b, s, dc), BF16)],
        compiler_params=_cparams(("arbitrary", "arbitrary")),
        name="input_projection",
    )(x, mod, g.reshape(1, d), w_in_bf)


def _attn_bias_table():
    blk = ATTN_BLOCK
    slopes = jnp.asarray(2.0 ** (-8.0 * np.arange(1, N_HEADS + 1) / N_HEADS), dtype=F32)
    qi = np.arange(blk)[:, None] + blk
    ki = np.arange(2 * blk)[None, :]
    delta = qi - ki
    in_band = (delta >= 0) & (delta <= blk)
    tables = []
    for dil in DILATIONS:
        bias = -slopes[:, None, None] * jnp.asarray(delta * dil, dtype=F32)
        later = jnp.where(in_band[None], bias, MASK_VALUE)
        first = jnp.where((in_band & (ki >= blk))[None], bias, MASK_VALUE)
        tables.append(jnp.stack([later, first], axis=1))
    t = jnp.stack(tables, axis=1)
    return t.reshape(N_HEADS // 2, 2, len(DILATIONS), 2, blk, 2 * blk)


def _attn_body(q_ref, k_ref, v_ref, bias_ref, o_ref, xf, x4, acc_o, acc_l, acc_m, s_buf, p_buf):
    blk = ATTN_BLOCK
    pw = 2 * HEAD_DIM
    s_len = q_ref.shape[1]
    seg = s_len // DEINT
    left = lax.broadcasted_iota(jnp.int32, (blk, pw), 1) < HEAD_DIM
    trans_b = (((1,), (1,)), ((), ()))
    n_branch = len(DILATIONS)
    n_blocks = s_len // blk
    natural = (q_ref.at[0], k_ref.at[0], v_ref.at[0])
    deint = (x4.at[0], x4.at[1], x4.at[2])

    for t, ref in enumerate(natural):
        xf[...] = ref[...].astype(F32)
        for r in range(DEINT):
            x4[t, r * seg:(r + 1) * seg, :] = xf[pl.ds(r, seg, stride=DEINT), :]

    for br, dil in enumerate(DILATIONS):
        nb = s_len // (blk * dil)
        step = dil // DEINT if dil > DEINT else 1
        srcs = natural if dil == 1 else deint

        def starts(j, dil=dil, nb=nb, step=step):
            r = j // nb
            n = j % nb
            if dil == 1:
                base = 0
            else:
                base = (r % DEINT) * seg + r // DEINT
            qs = base + n * (blk * step)
            ks = base + jnp.maximum(n - 1, 0) * (blk * step)
            return qs, ks, jnp.where(n == 0, 1, 0)

        def span(start, step=step):
            return pl.ds(start, blk) if step == 1 else pl.ds(start, blk, stride=step)

        def both(ref, qs, ks):
            return jnp.concatenate([ref[span(ks), :], ref[span(qs), :]], axis=0).astype(BF16)

        def scores(j, slot, br=br, srcs=srcs):
            qs, ks, first = starts(j)
            q = srcs[0][span(qs), :].astype(BF16)
            kk = both(srcs[1], qs, ks)
            zero = jnp.zeros_like(q)
            q2 = jnp.concatenate([jnp.where(left, q, zero), jnp.where(left, zero, q)], axis=0)
            s = lax.dot_general(q2, kk, trans_b, preferred_element_type=F32)
            s_buf[slot] = s.reshape(2, blk, 2 * blk) + bias_ref[0, :, br, first]

        def softmax(j, slot, br=br):
            qs, _, _ = starts(j)
            maxes = []
            for hh in range(2):
                s = s_buf[slot, hh]
                m = jnp.max(s, axis=-1, keepdims=True)
                p_buf[slot, hh] = jnp.exp(s - m).astype(BF16)
                maxes.append(m)
            acc_m[br, span(qs), :] = jnp.where(left, maxes[0], maxes[1])

        def values(j, slot, br=br, srcs=srcs):
            qs, ks, _ = starts(j)
            vv = both(srcs[2], qs, ks)
            rhs = jnp.concatenate([vv, jnp.ones_like(vv)], axis=1)
            o = jnp.dot(p_buf[slot].reshape(2 * blk, 2 * blk), rhs, preferred_element_type=F32)
            acc_o[br, span(qs), :] = jnp.where(left, o[0:blk, 0:pw], o[blk:, 0:pw])
            acc_l[br, span(qs), :] = jnp.where(left, o[0:blk, pw:], o[blk:, pw:])

        nslot = ATTN_UNROLL
        scores(0, 0)
        softmax(0, 0)
        scores(1, 1)

        def steady(i, carry):
            for u in range(ATTN_UNROLL):
                j = 2 + ATTN_UNROLL * i + u
                values(j - 2, u % nslot)
                softmax(j - 1, (u + 1) % nslot)
                scores(j, (u + 2) % nslot)
            return carry

        trips = (n_blocks - 2) // ATTN_UNROLL
        lax.fori_loop(0, trips, steady, 0)
        for j in range(2 + ATTN_UNROLL * trips, n_blocks):
            values(j - 2, (j - 2) % nslot)
            softmax(j - 1, (j - 1) % nslot)
            scores(j, j % nslot)
        values(n_blocks - 2, (n_blocks - 2) % nslot)
        softmax(n_blocks - 1, (n_blocks - 1) % nslot)
        values(n_blocks - 1, (n_blocks - 1) % nslot)

    rows = 256

    def merge(c, carry):
        idx = pl.multiple_of(c * rows, rows)
        r = idx // seg
        nat = pl.ds(DEINT * (idx - r * seg) + r, rows, stride=DEINT)
        spans = [nat if dil == 1 else pl.ds(idx, rows) for dil in DILATIONS]
        ms = [acc_m[br, spans[br], :] for br in range(n_branch)]
        m_max = functools.reduce(jnp.maximum, ms)
        num = 0.0
        den = 0.0
        for br in range(n_branch):
            a = jnp.exp(ms[br] - m_max)
            num = num + a * acc_o[br, spans[br], :]
            den = den + a * acc_l[br, spans[br], :]
        o_ref[0, nat, :] = num / den
        return carry

    lax.fori_loop(0, s_len // rows, merge, 0, unroll=2)


def _dilated_attention(q, k, v, bias):
    b, s, da = q.shape
    pw = 2 * HEAD_DIM
    assert DILATIONS == (1, DEINT, DEINT * DEINT)
    seq = pl.BlockSpec((1, s, pw), lambda i, j: (i, 0, j))
    nbr = len(DILATIONS)
    stage = (ATTN_UNROLL, 2, ATTN_BLOCK, 2 * ATTN_BLOCK)
    return pl.pallas_call(
        _attn_body,
        grid=(b, da // pw),
        in_specs=[seq, seq, seq,
                  pl.BlockSpec((1,) + bias.shape[1:], lambda i, j: (j, 0, 0, 0, 0, 0))],
        out_specs=seq,
        out_shape=jax.ShapeDtypeStruct((b, s, da), F32),
        scratch_shapes=[pltpu.VMEM((s, pw), F32), pltpu.VMEM((3, s, pw), F32)]
        + [pltpu.VMEM((nbr, s, pw), F32)] * 3
        + [pltpu.VMEM(stage, F32), pltpu.VMEM(stage, BF16)],
        compiler_params=_cparams(("arbitrary", "arbitrary")),
        name="dilated_attention",
    )(q, k, v, bias)


def _conv_body(uc_ref, uh_ref, w_ref, cb_ref, lg_ref, lb_ref, o_ref, win, shifted):
    t = pl.program_id(1)
    tc = uc_ref.shape[1]
    win[0:CONV_HALO, :] = jnp.where(t > 0, uh_ref[0].astype(F32), 0.0)
    win[CONV_HALO:, :] = uc_ref[0].astype(F32)
    off = CONV_HALO - (CONV_WIDTH - 1)
    for b in range(SUBLANES):
        n_rows = tc + SUBLANES * ((CONV_WIDTH - 1 - b) // SUBLANES)
        shifted[b, 0:n_rows, :] = win[off + b:off + b + n_rows, :]
    for rb in range(tc // CONV_ROWS):
        r0 = rb * CONV_ROWS
        acc = jnp.zeros((CONV_ROWS, uc_ref.shape[2]), F32)
        for j in range(CONV_WIDTH):
            a, b = divmod(j, SUBLANES)
            rows = pl.ds(r0 + SUBLANES * a, CONV_ROWS)
            acc = acc + w_ref[j:j + 1, :] * shifted[b, rows, :]
        y = acc + cb_ref[...]
        mu = jnp.mean(y, axis=-1, keepdims=True)
        yc = y - mu
        var = jnp.mean(yc * yc, axis=-1, keepdims=True)
        z = yc * lax.rsqrt(var + NORM_EPS) * lg_ref[...] + lb_ref[...]
        o_ref[0, r0:r0 + CONV_ROWS, :] = (z * jax.nn.sigmoid(z)).astype(o_ref.dtype)


def _conformer_conv(u, conv_w, conv_b, ln_g, ln_b):
    b, s, dc = u.shape
    tc = CONV_TILE
    per = tc // CONV_HALO
    row = pl.BlockSpec((1, dc), lambda i, j: (0, 0))
    return pl.pallas_call(
        _conv_body,
        grid=(b, s // tc),
        in_specs=[pl.BlockSpec((1, tc, dc), lambda i, j: (i, j, 0)),
                  pl.BlockSpec((1, CONV_HALO, dc), lambda i, j: (i, jnp.maximum(j * per - 1, 0), 0)),
                  pl.BlockSpec((CONV_WIDTH, dc), lambda i, j: (0, 0)),
                  row, row, row],
        out_specs=pl.BlockSpec((1, tc, dc), lambda i, j: (i, j, 0)),
        out_shape=jax.ShapeDtypeStruct((b, s, dc), BF16),
        scratch_shapes=[pltpu.VMEM((tc + CONV_HALO, dc), F32),
                        pltpu.VMEM((SUBLANES, tc + CONV_HALO - SUBLANES, dc), F32)],
        compiler_params=_cparams(("arbitrary", "arbitrary")),
        name="conformer_conv",
    )(u, u, conv_w, conv_b.reshape(1, dc), ln_g.reshape(1, dc), ln_b.reshape(1, dc))


def _route(logits, seen):
    rows = logits.shape[0]
    lane = lax.broadcasted_iota(jnp.int32, logits.shape, 1)
    big = jnp.int32(1 << 20)
    is_g = lane < N_GROUPS
    gl = jnp.where(is_g, logits, MASK_VALUE)
    gmax = jnp.max(gl, axis=-1, keepdims=True)
    grp = jnp.min(jnp.where(gl == gmax, lane, big), axis=-1, keepdims=True)
    gsum = jnp.sum(jnp.where(is_g, jnp.exp(gl - gmax), 0.0), axis=-1, keepdims=True)
    p_grp = 1.0 / gsum
    lo = N_GROUPS + grp * EXPERTS_PER_GROUP
    el = jnp.where((lane >= lo) & (lane < lo + EXPERTS_PER_GROUP), logits, MASK_VALUE)
    v1 = jnp.max(el, axis=-1, keepdims=True)
    i1 = jnp.min(jnp.where(el == v1, lane, big), axis=-1, keepdims=True)
    el2 = jnp.where(lane == i1, MASK_VALUE, el)
    v2 = jnp.max(el2, axis=-1, keepdims=True)
    i2 = jnp.min(jnp.where(el2 == v2, lane, big), axis=-1, keepdims=True)
    e = jnp.exp(v2 - v1)
    w1 = p_grp / (1.0 + e)
    w2 = p_grp * e / (1.0 + e)

    hit1 = lane == i1
    hit2 = lane == i2
    chosen = jnp.where(hit1 | hit2, 1.0, 0.0)
    ri = lax.broadcasted_iota(jnp.int32, (rows, rows), 0)
    ci = lax.broadcasted_iota(jnp.int32, (rows, rows), 1)
    earlier = jnp.where(ri > ci, 1.0, 0.0).astype(BF16)
    before = jnp.dot(earlier, chosen.astype(BF16), preferred_element_type=F32) + seen
    rank1 = jnp.sum(jnp.where(hit1, before, 0.0), axis=-1, keepdims=True)
    rank2 = jnp.sum(jnp.where(hit2, before, 0.0), axis=-1, keepdims=True)
    seen = seen + jnp.sum(chosen, axis=0, keepdims=True)

    rec = jnp.zeros(logits.shape, F32)
    for pos, val in ((REC_LANE0, i1.astype(F32)), (REC_LANE1, i2.astype(F32)), (REC_W0, w1),
                     (REC_W1, w2), (REC_RANK0, rank1), (REC_RANK1, rank2)):
        rec = jnp.where(lane == pos, val, rec)
    return rec, seen


def _split_bf16(x):
    hi = x.astype(BF16)
    lo = (x - hi.astype(F32)).astype(BF16)
    return hi, lo


def _outproj_body(a_ref, c_ref, x_ref, mod_ref, g_ref, w_ref, wrh_ref, wrl_ref, br_ref,
                  xo_ref, h_ref, r_ref, sel_ref, cnt_ref, seen):
    @pl.when((pl.program_id(0) == 0) & (pl.program_id(1) == 0))
    def _():
        seen[...] = jnp.zeros_like(seen)

    da = a_ref.shape[2]
    mix = jnp.dot(a_ref[0].astype(BF16), w_ref[0, 0:da, :], preferred_element_type=F32)
    mix = mix + jnp.dot(c_ref[0], w_ref[0, da:, :], preferred_element_type=F32)
    xn = x_ref[0] + mod_ref[0, 2:3, :] * mix
    xo_ref[0] = xn
    h = _rms_modulate(xn, g_ref[...], mod_ref[0, 3:4, :], mod_ref[0, 4:5, :])
    _store_token_tiles(h_ref, h)
    h_hi, h_lo = _split_bf16(h)
    logits = (jnp.dot(h_hi, wrh_ref[0], preferred_element_type=F32)
              + jnp.dot(h_lo, wrh_ref[0], preferred_element_type=F32)
              + jnp.dot(h_hi, wrl_ref[0], preferred_element_type=F32)) + br_ref[0]
    rec, new_seen = _route(logits, seen[...])
    r_ref[0] = rec
    sel_ref[...] = rec.T[0:SUBLANES, :]
    seen[...] = new_seen
    cnt_ref[...] = new_seen


def _output_projection(attn, conv, x, mod, g, w_out_bf, wr_hi, wr_lo, b_router, layer):
    b, s, d = x.shape
    ts = SEQ_TILE
    tile = lambda n: pl.BlockSpec((1, ts, n), lambda i, j: (i, j, 0))
    const = lambda r, c: pl.BlockSpec((r, c), lambda i, j: (0, 0))
    per_layer = lambda r, c: pl.BlockSpec((1, r, c), lambda i, j: (layer, 0, 0))
    return pl.pallas_call(
        _outproj_body,
        grid=(b, s // ts),
        in_specs=[tile(attn.shape[2]), tile(conv.shape[2]), tile(d),
                  pl.BlockSpec((1, N_MOD, d), lambda i, j: (i, 0, 0)),
                  const(1, d), per_layer(d, d), per_layer(d, ROUTE_LANES),
                  per_layer(d, ROUTE_LANES), per_layer(1, ROUTE_LANES)],
        out_specs=[tile(d), pl.BlockSpec((ts * SUBLANES, LANES), lambda i, j: (i * (s // ts) + j, 0)),
                   tile(ROUTE_LANES),
                   pl.BlockSpec((SUBLANES, ts), lambda i, j: (0, i * (s // ts) + j)),
                   const(1, ROUTE_LANES)],
        out_shape=[jax.ShapeDtypeStruct((b, s, d), F32),
                   jax.ShapeDtypeStruct((b * s * SUBLANES, LANES), F32),
                   jax.ShapeDtypeStruct((b, s, ROUTE_LANES), F32),
                   jax.ShapeDtypeStruct((SUBLANES, b * s), F32),
                   jax.ShapeDtypeStruct((1, ROUTE_LANES), F32)],
        scratch_shapes=[pltpu.VMEM((1, ROUTE_LANES), F32)],
        compiler_params=_cparams(("arbitrary", "arbitrary")),
        name="output_projection",
    )(attn, conv, x, mod, g.reshape(1, d), w_out_bf, wr_hi, wr_lo, b_router)


def _load_token_tiles(ref):
    n = ref.shape[0] // SUBLANES
    return jnp.concatenate([ref[pl.ds(a, n, stride=SUBLANES), :] for a in range(SUBLANES)], axis=1)


def _store_token_tiles(ref, val):
    n = val.shape[0]
    for a in range(SUBLANES):
        ref[pl.ds(a, n, stride=SUBLANES), :] = val[:, a * LANES:(a + 1) * LANES]


def _tile_of(ref, token):
    return ref.at[pl.ds(pl.multiple_of(token * SUBLANES, SUBLANES), SUBLANES)]


def _row_copy(src, src_token, dst, dst_token, sem):
    return pltpu.make_async_copy(_tile_of(src, src_token), _tile_of(dst, dst_token), sem)


def _wait_rows(like, n_tokens, sem):
    n = n_tokens * SUBLANES
    pltpu.make_async_copy(like.at[pl.ds(0, n)], like.at[pl.ds(0, n)], sem).wait()


def _dispatch_body(dest_ref, h_ref, xb_hbm, sem):
    td = h_ref.shape[0] // SUBLANES
    n_tok = dest_ref.shape[0] // 2
    base = pl.program_id(0) * td

    def issue(t, carry):
        for choice in range(2):
            _row_copy(h_ref, t, xb_hbm, dest_ref[choice * n_tok + base + t], sem).start()
        return carry

    lax.fori_loop(0, td, issue, 0, unroll=DMA_UNROLL)
    _wait_rows(xb_hbm, 2 * td, sem)


def _dispatch(h, dest, n_rows):
    t = h.shape[0] // SUBLANES
    td = DISPATCH_TILE
    return pl.pallas_call(
        _dispatch_body,
        grid_spec=pltpu.PrefetchScalarGridSpec(
            num_scalar_prefetch=1,
            grid=(t // td,),
            in_specs=[pl.BlockSpec((td * SUBLANES, LANES), lambda i, *_: (i, 0))],
            out_specs=pl.BlockSpec(memory_space=pl.ANY),
            scratch_shapes=[pltpu.SemaphoreType.DMA(())]),
        out_shape=jax.ShapeDtypeStruct((n_rows * SUBLANES, LANES), h.dtype),
        compiler_params=_cparams(("arbitrary",)),
        name="moe_dispatch",
    )(dest, h)


def _expert_body(eid_ref, valid_ref, nused_ref, x_ref, wg_ref, wu_ref, wd_ref, o_ref,
                 wg_bf, wu_bf, wd_bf):
    i = pl.program_id(0)

    @pl.when(i < nused_ref[0])
    def _():
        changed = (i == 0) | (eid_ref[i] != eid_ref[jnp.maximum(i - 1, 0)])

        @pl.when(changed)
        def _():
            wg_bf[...] = wg_ref[0, 0].astype(BF16)
            wu_bf[...] = wu_ref[0, 0].astype(BF16)
            wd_bf[...] = wd_ref[0, 0].astype(BF16)

        x = _load_token_tiles(x_ref)
        row = lax.broadcasted_iota(jnp.int32, x.shape, 0)
        x = jnp.where(row < valid_ref[i], x, 0.0).astype(BF16)
        g = jnp.dot(x, wg_bf[...], preferred_element_type=F32)
        u = jnp.dot(x, wu_bf[...], preferred_element_type=F32)
        h = (g * jax.nn.sigmoid(g)) * u
        _store_token_tiles(o_ref, jnp.dot(h.astype(BF16), wd_bf[...], preferred_element_type=F32))

    @pl.when(i >= nused_ref[0])
    def _():
        o_ref[...] = jnp.zeros_like(o_ref)


def _experts(xb, blk_eid, blk_valid, n_used, w_gate, w_up, w_down, layer):
    p = xb.shape[0] // SUBLANES
    d, de = w_gate.shape[2:]
    tm = MOE_BLOCK
    last = lambda i, nu: jnp.minimum(i, nu[0] - 1)
    rows = pl.BlockSpec((tm * SUBLANES, LANES), lambda i, eid, nv, nu: (last(i, nu), 0))
    weight = lambda r, c: pl.BlockSpec(
        (1, 1, r, c), lambda i, eid, nv, nu: (layer, eid[last(i, nu)], 0, 0))
    return pl.pallas_call(
        _expert_body,
        grid_spec=pltpu.PrefetchScalarGridSpec(
            num_scalar_prefetch=3,
            grid=(p // tm,),
            in_specs=[rows, weight(d, de), weight(d, de), weight(de, d)],
            out_specs=pl.BlockSpec((tm * SUBLANES, LANES), lambda i, eid, nv, nu: (i, 0)),
            scratch_shapes=[pltpu.VMEM((d, de), BF16), pltpu.VMEM((d, de), BF16),
                            pltpu.VMEM((de, d), BF16)]),
        out_shape=jax.ShapeDtypeStruct((p * SUBLANES, LANES), F32),
        compiler_params=_cparams(("arbitrary",)),
        name="moe_experts",
    )(blk_eid, blk_valid, n_used, xb, w_gate, w_up, w_down)


def _gather_rows(dest_ref, yb_hbm, ybuf, sem, tile, into, n_tokens, unroll):
    n_tok = dest_ref.shape[0] // 2

    def issue(t, carry):
        for choice in range(2):
            src = dest_ref[choice * n_tok + tile * n_tokens + t]
            _row_copy(yb_hbm, src, ybuf.at[into, choice], t, sem.at[into, choice]).start()
        return carry

    lax.fori_loop(0, n_tokens, issue, 0, unroll=unroll)


def _gathered_moe(r_ref, ybuf, sem, slot, n_tokens):
    for choice in range(2):
        _wait_rows(ybuf.at[slot, choice], n_tokens, sem.at[slot, choice])
    return (r_ref[:, REC_W0:REC_W0 + 1] * _load_token_tiles(ybuf.at[slot, 0])
            + r_ref[:, REC_W1:REC_W1 + 1] * _load_token_tiles(ybuf.at[slot, 1]))


def _combine_final_body(dest_ref, x_ref, r_ref, mod_ref, g_ref, yb_hbm, o_ref, ybuf, sem):
    tcm = x_ref.shape[0]
    i = pl.program_id(0)
    slot = i % 2

    @pl.when(i == 0)
    def _():
        _gather_rows(dest_ref, yb_hbm, ybuf, sem, 0, 0, tcm, DMA_UNROLL)

    @pl.when(i + 1 < pl.num_programs(0))
    def _():
        _gather_rows(dest_ref, yb_hbm, ybuf, sem, i + 1, 1 - slot, tcm, DMA_UNROLL)

    xn = x_ref[...] + mod_ref[0, 5:6, :] * _gathered_moe(r_ref, ybuf, sem, slot, tcm)
    ms = jnp.mean(xn * xn, axis=-1, keepdims=True)
    o_ref[...] = (xn * lax.rsqrt(ms + NORM_EPS)) * g_ref[...]


def _combine_final(x, route, mod, g_final, yb, dest, seq_len):
    t, d = x.shape
    tcm = COMBINE_TILE
    per_seq = seq_len // tcm
    return pl.pallas_call(
        _combine_final_body,
        grid_spec=pltpu.PrefetchScalarGridSpec(
            num_scalar_prefetch=1,
            grid=(t // tcm,),
            in_specs=[pl.BlockSpec((tcm, d), lambda i, *_: (i, 0)),
                      pl.BlockSpec((tcm, ROUTE_LANES), lambda i, *_: (i, 0)),
                      pl.BlockSpec((1, N_MOD, d), lambda i, *_: (i // per_seq, 0, 0)),
                      pl.BlockSpec((1, d), lambda i, *_: (0, 0)),
                      pl.BlockSpec(memory_space=pl.ANY)],
            out_specs=pl.BlockSpec((tcm, d), lambda i, *_: (i, 0)),
            scratch_shapes=[pltpu.VMEM((2, 2, tcm * SUBLANES, LANES), F32),
                            pltpu.SemaphoreType.DMA((2, 2))]),
        out_shape=jax.ShapeDtypeStruct((t, d), F32),
        compiler_params=_cparams(("arbitrary",)),
        name="moe_combine_final",
    )(dest, x, route, mod, g_final.reshape(1, d), yb)


def _combine_inproj_body(dest_ref, x_ref, r_ref, modp_ref, mod_ref, g_ref, w_ref, yb_hbm,
                         xo_ref, q_ref, k_ref, v_ref, u_ref, ybuf, sem):
    ts = x_ref.shape[1]
    per_seq = pl.num_programs(1)
    n_tiles = pl.num_programs(0) * per_seq
    i = pl.program_id(0) * per_seq + pl.program_id(1)
    slot = i % 2

    @pl.when(i == 0)
    def _():
        _gather_rows(dest_ref, yb_hbm, ybuf, sem, 0, 0, ts, DMA_UNROLL)

    moe = _gathered_moe(r_ref, ybuf, sem, slot, ts)
    _gather_rows(dest_ref, yb_hbm, ybuf, sem, jnp.minimum(i + 1, n_tiles - 1), 1 - slot, ts, True)
    xn = x_ref[0] + modp_ref[0, 5:6, :] * moe
    xo_ref[0] = xn
    _project_in(xn, mod_ref, g_ref, w_ref, q_ref, k_ref, v_ref, u_ref)

    @pl.when(i == n_tiles - 1)
    def _():
        for choice in range(2):
            _wait_rows(ybuf.at[1 - slot, choice], ts, sem.at[1 - slot, choice])


def _combine_input_projection(x, route, mod_prev, yb, dest, mod, g, w_in_bf, layer):
    b, s, d = x.shape
    din = w_in_bf.shape[2]
    dc = (din - 3 * D_ATTN) // 2
    ts = SEQ_TILE
    per_seq = s // ts
    tile = lambda n: pl.BlockSpec((1, ts, n), lambda i, j, *_: (i, j, 0))
    mods = pl.BlockSpec((1, N_MOD, d), lambda i, j, *_: (i, 0, 0))
    return pl.pallas_call(
        _combine_inproj_body,
        grid_spec=pltpu.PrefetchScalarGridSpec(
            num_scalar_prefetch=1,
            grid=(b, per_seq),
            in_specs=[tile(d),
                      pl.BlockSpec((ts, ROUTE_LANES), lambda i, j, *_: (i * per_seq + j, 0)),
                      mods, mods,
                      pl.BlockSpec((1, d), lambda i, j, *_: (0, 0)),
                      pl.BlockSpec((1, d, din), lambda i, j, *_: (layer, 0, 0)),
                      pl.BlockSpec(memory_space=pl.ANY)],
            out_specs=[tile(d), tile(D_ATTN), tile(D_ATTN), tile(D_ATTN), tile(dc)],
            scratch_shapes=[pltpu.VMEM((2, 2, ts * SUBLANES, LANES), F32),
                            pltpu.SemaphoreType.DMA((2, 2))]),
        out_shape=[jax.ShapeDtypeStruct((b, s, d), F32)]
        + [jax.ShapeDtypeStruct((b, s, D_ATTN), BF16)] * 3
        + [jax.ShapeDtypeStruct((b, s, dc), BF16)],
        compiler_params=_cparams(("arbitrary", "arbitrary")),
        name="combine_input_projection",
    )(dest, x, route, mod_prev, mod, g.reshape(1, d), w_in_bf, yb)


def _plan_blocks(counts, n_blocks):
    tm = MOE_BLOCK
    experts = jnp.arange(N_EXPERTS, dtype=jnp.int32)
    padded = ((counts + tm - 1) // tm) * tm
    pad_end = jnp.sum(jnp.where(experts[:, None] <= experts[None, :], padded[:, None], 0), axis=0)
    pad_start = pad_end - padded
    blk_row = jnp.arange(n_blocks, dtype=jnp.int32)[:, None] * tm
    inside = (pad_start[None, :] <= blk_row) & (blk_row < pad_end[None, :])
    blk_eid = jnp.where(jnp.any(inside, axis=1),
                        jnp.sum(jnp.where(inside, experts[None, :], 0), axis=1), N_EXPERTS - 1)
    rows_left = jnp.clip((pad_start + counts)[None, :] - blk_row, 0, tm)
    blk_valid = jnp.sum(jnp.where(inside, rows_left, 0), axis=1)
    n_used = jnp.sum(padded, keepdims=True) // tm
    i32 = lambda a: a.astype(jnp.int32)
    return i32(pad_start), i32(blk_eid), i32(blk_valid), i32(n_used)


def kernel(x, c, w_mod, b_mod, g_norm1, w_in, conv_w, conv_b, conv_ln_g, conv_ln_b, w_out, g_norm2, w_router_group, b_router_group, w_router_expert, b_router_expert, w_exp_gate, w_exp_up, w_exp_down, g_final):
    b, s, d = x.shape
    depth = w_mod.shape[0]
    t = b * s
    assert d == SUBLANES * LANES
    assert s % (ATTN_BLOCK * max(DILATIONS)) == 0 and s % SEQ_TILE == 0
    assert t % DISPATCH_TILE == 0 and s % COMBINE_TILE == 0 and s % CONV_TILE == 0
    n_blocks = -(-2 * t // MOE_BLOCK) + N_EXPERTS
    n_rows = n_blocks * MOE_BLOCK

    mod_all = _modulation(c, w_mod, b_mod).reshape(depth, b, N_MOD, d)
    bias = _attn_bias_table()
    pad = ROUTE_LANES - N_GROUPS - N_EXPERTS
    w_router = jnp.concatenate(
        [w_router_group, w_router_expert, jnp.zeros((depth, d, pad), F32)], axis=-1)
    b_router = jnp.concatenate(
        [b_router_group, b_router_expert, jnp.zeros((depth, pad), F32)], axis=-1)
    b_router = b_router.reshape(depth, 1, ROUTE_LANES)
    wr_hi = w_router.astype(BF16)
    wr_lo = (w_router - wr_hi.astype(F32)).astype(BF16)
    w_in_bf = w_in.astype(BF16)
    w_out_bf = w_out.astype(BF16)

    pending = None
    for l in range(depth):
        mod = mod_all[l]
        if pending is None:
            q, k, v, u = _input_projection(x, mod, g_norm1[l], w_in_bf, l)
        else:
            x, q, k, v, u = _combine_input_projection(x, *pending, mod, g_norm1[l], w_in_bf, l)
        attn = _dilated_attention(q, k, v, bias)
        conv = _conformer_conv(u, conv_w[l], conv_b[l], conv_ln_g[l], conv_ln_b[l])
        x, h2, route, sel, seen = _output_projection(attn, conv, x, mod, g_norm2[l], w_out_bf,
                                                     wr_hi, wr_lo, b_router, l)
        route = route.reshape(t, ROUTE_LANES)
        counts = seen[0, N_GROUPS:N_GROUPS + N_EXPERTS].astype(jnp.int32)
        starts, blk_eid, blk_valid, n_used = _plan_blocks(counts, n_blocks)
        eid = sel[REC_LANE0:REC_LANE1 + 1].astype(jnp.int32) - N_GROUPS
        ranks = sel[REC_RANK0:REC_RANK1 + 1].astype(jnp.int32)
        onehot = eid[None] == jnp.arange(N_EXPERTS, dtype=jnp.int32)[:, None, None]
        dest = (jnp.sum(jnp.where(onehot, starts[:, None, None], 0), axis=0) + ranks).reshape(-1)
        xb = _dispatch(h2, dest, n_rows)
        yb = _experts(xb, blk_eid, blk_valid, n_used, w_exp_gate, w_exp_up, w_exp_down, l)
        pending = (route, mod, yb, dest)
    route, mod, yb, dest = pending
    return _combine_final(x.reshape(t, d), route, mod, g_final, yb, dest, s).reshape(b, s, d)
```

```python
import functools

import numpy as np
import jax
import jax.numpy as jnp
from jax import lax
from jax.experimental import pallas as pl
from jax.experimental.pallas import tpu as pltpu

F32 = jnp.float32
BF16 = jnp.bfloat16

SUBLANES = 8
LANES = 128
HEAD_DIM = 64
N_HEADS = 8
D_ATTN = N_HEADS * HEAD_DIM
CONV_WIDTH = 31
DILATIONS = (1, 4, 16)
ATTN_BLOCK = 128
ATTN_UNROLL = 10
DEINT = 4
N_GROUPS = 4
EXPERTS_PER_GROUP = 8
N_EXPERTS = N_GROUPS * EXPERTS_PER_GROUP
NORM_EPS = 1e-6
N_MOD = 6
MASK_VALUE = -1e30
ROUTE_LANES = 128
REC_LANE0, REC_LANE1, REC_W0, REC_W1, REC_RANK0, REC_RANK1 = range(6)

SEQ_TILE = 512
CONV_HALO = 32
CONV_ROWS = 64
MOE_BLOCK = 512
DISPATCH_TILE = 4096
COMBINE_TILE = 512
DMA_UNROLL = 8
VMEM_LIMIT = 56 * 1024 * 1024


def _cparams(sem):
    return pltpu.CompilerParams(dimension_semantics=sem, vmem_limit_bytes=VMEM_LIMIT)


def _mod_body(c_ref, w_ref, b_ref, o_ref):
    c = c_ref[...]
    c_act = c * jax.nn.sigmoid(c)
    o_ref[0] = jnp.dot(c_act, w_ref[0], preferred_element_type=F32,
                       precision=lax.Precision.HIGHEST) + b_ref[0]


def _modulation(c, w_mod, b_mod):
    depth, d, dm = w_mod.shape
    b = c.shape[0]
    nt = dm // d
    return pl.pallas_call(
        _mod_body,
        grid=(depth, nt),
        in_specs=[pl.BlockSpec((b, d), lambda l, j: (0, 0)),
                  pl.BlockSpec((1, d, d), lambda l, j: (l, 0, j)),
                  pl.BlockSpec((1, 1, d), lambda l, j: (l, 0, j))],
        out_specs=pl.BlockSpec((1, b, d), lambda l, j: (l, 0, j)),
        out_shape=jax.ShapeDtypeStruct((depth, b, dm), F32),
        compiler_params=_cparams(("arbitrary", "arbitrary")),
        name="modulation",
    )(c, w_mod, b_mod.reshape(depth, 1, dm))


def _rms_modulate(x, g, shift, scale):
    ms = jnp.mean(x * x, axis=-1, keepdims=True)
    return (x * lax.rsqrt(ms + NORM_EPS)) * g * (1.0 + scale) + shift


def _project_in(x, mod_ref, g_ref, w_ref, q_ref, k_ref, v_ref, u_ref):
    h = _rms_modulate(x, g_ref[...], mod_ref[0, 0:1, :], mod_ref[0, 1:2, :])
    y = jnp.dot(h.astype(BF16), w_ref[0], preferred_element_type=F32)
    da = D_ATTN
    q_ref[0] = (y[:, 0:da] * (HEAD_DIM ** -0.5)).astype(q_ref.dtype)
    k_ref[0] = y[:, da:2 * da].astype(k_ref.dtype)
    v_ref[0] = y[:, 2 * da:3 * da].astype(v_ref.dtype)
    dc = (y.shape[1] - 3 * da) // 2
    a = y[:, 3 * da:3 * da + dc]
    b = y[:, 3 * da + dc:]
    u_ref[0] = (a * jax.nn.sigmoid(b)).astype(u_ref.dtype)


def _inproj_body(x_ref, mod_ref, g_ref, w_ref, q_ref, k_ref, v_ref, u_ref):
    _project_in(x_ref[0], mod_ref, g_ref, w_ref, q_ref, k_ref, v_ref, u_ref)


def _input_projection(x, mod, g, w_in_bf, layer):
    b, s, d = x.shape
    din = w_in_bf.shape[2]
    dc = (din - 3 * D_ATTN) // 2
    ts = SEQ_TILE
    tile = lambda n: pl.BlockSpec((1, ts, n), lambda i, j: (i, j, 0))
    return pl.pallas_call(
        _inproj_body,
        grid=(b, s // ts),
        in_specs=[tile(d),
                  pl.BlockSpec((1, N_MOD, d), lambda i, j: (i, 0, 0)),
                  pl.BlockSpec((1, d), lambda i, j: (0, 0)),
                  pl.BlockSpec((1, d, din), lambda i, j: (layer, 0, 0))],
        out_specs=[tile(D_ATTN), tile(D_ATTN), tile(D_ATTN), tile(dc)],
        out_shape=[jax.ShapeDtypeStruct((b, s, D_ATTN), BF16)] * 3
        + [jax.ShapeDtypeStruct((b, s, dc), BF16)],
        compiler_params=_cparams(("arbitrary", "arbitrary")),
        name="input_projection",
    )(x, mod, g.reshape(1, d), w_in_bf)


def _attn_bias_table():
    blk = ATTN_BLOCK
    slopes = jnp.asarray(2.0 ** (-8.0 * np.arange(1, N_HEADS + 1) / N_HEADS), dtype=F32)
    qi = np.arange(blk)[:, None] + blk
    ki = np.arange(2 * blk)[None, :]
    delta = qi - ki
    in_band = (delta >= 0) & (delta <= blk)
    tables = []
    for dil in DILATIONS:
        bias = -slopes[:, None, None] * jnp.asarray(delta * dil, dtype=F32)
        later = jnp.where(in_band[None], bias, MASK_VALUE)
        first = jnp.where((in_band & (ki >= blk))[None], bias, MASK_VALUE)
        tables.append(jnp.stack([later, first], axis=1))
    t = jnp.stack(tables, axis=1)
    return t.reshape(N_HEADS // 2, 2, len(DILATIONS), 2, blk, 2 * blk)


def _attn_body(q_ref, k_ref, v_ref, bias_ref, o_ref, xf, x4, acc_o, acc_l, acc_m, s_buf, p_buf):
    blk = ATTN_BLOCK
    pw = 2 * HEAD_DIM
    s_len = q_ref.shape[1]
    seg = s_len // DEINT
    left = lax.broadcasted_iota(jnp.int32, (blk, pw), 1) < HEAD_DIM
    trans_b = (((1,), (1,)), ((), ()))
    n_branch = len(DILATIONS)
    n_blocks = s_len // blk
    natural = (q_ref.at[0], k_ref.at[0], v_ref.at[0])
    deint = (x4.at[0], x4.at[1], x4.at[2])

    for t, ref in enumerate(natural):
        xf[...] = ref[...].astype(F32)
        for r in range(DEINT):
            x4[t, r * seg:(r + 1) * seg, :] = xf[pl.ds(r, seg, stride=DEINT), :]

    for br, dil in enumerate(DILATIONS):
        nb = s_len // (blk * dil)
        step = dil // DEINT if dil > DEINT else 1
        srcs = natural if dil == 1 else deint

        def starts(j, dil=dil, nb=nb, step=step):
            r = j // nb
            n = j % nb
            if dil == 1:
                base = 0
            else:
                base = (r % DEINT) * seg + r // DEINT
            qs = base + n * (blk * step)
            ks = base + jnp.maximum(n - 1, 0) * (blk * step)
            return qs, ks, jnp.where(n == 0, 1, 0)

        def span(start, step=step):
            return pl.ds(start, blk) if step == 1 else pl.ds(start, blk, stride=step)

        def both(ref, qs, ks):
            return jnp.concatenate([ref[span(ks), :], ref[span(qs), :]], axis=0).astype(BF16)

        def scores(j, slot, br=br, srcs=srcs):
            qs, ks, first = starts(j)
            q = srcs[0][span(qs), :].astype(BF16)
            kk = both(srcs[1], qs, ks)
            zero = jnp.zeros_like(q)
            q2 = jnp.concatenate([jnp.where(left, q, zero), jnp.where(left, zero, q)], axis=0)
            s = lax.dot_general(q2, kk, trans_b, preferred_element_type=F32)
            s_buf[slot] = s.reshape(2, blk, 2 * blk) + bias_ref[0, :, br, first]

        def softmax(j, slot, br=br):
            qs, _, _ = starts(j)
            maxes = []
            for hh in range(2):
                s = s_buf[slot, hh]
                m = jnp.max(s, axis=-1, keepdims=True)
                p_buf[slot, hh] = jnp.exp(s - m).astype(BF16)
                maxes.append(m)
            acc_m[br, span(qs), :] = jnp.where(left, maxes[0], maxes[1])

        def values(j, slot, br=br, srcs=srcs):
            qs, ks, _ = starts(j)
            vv = both(srcs[2], qs, ks)
            rhs = jnp.concatenate([vv, jnp.ones_like(vv)], axis=1)
            o = jnp.dot(p_buf[slot].reshape(2 * blk, 2 * blk), rhs, preferred_element_type=F32)
            acc_o[br, span(qs), :] = jnp.where(left, o[0:blk, 0:pw], o[blk:, 0:pw])
            acc_l[br, span(qs), :] = jnp.where(left, o[0:blk, pw:], o[blk:, pw:])

        nslot = ATTN_UNROLL
        scores(0, 0)
        softmax(0, 0)
        scores(1, 1)

        def steady(i, carry):
            for u in range(ATTN_UNROLL):
                j = 2 + ATTN_UNROLL * i + u
                values(j - 2, u % nslot)
                softmax(j - 1, (u + 1) % nslot)
                scores(j, (u + 2) % nslot)
            return carry

        trips = (n_blocks - 2) // ATTN_UNROLL
        lax.fori_loop(0, trips, steady, 0)
        for j in range(2 + ATTN_UNROLL * trips, n_blocks):
            values(j - 2, (j - 2) % nslot)
            softmax(j - 1, (j - 1) % nslot)
            scores(j, j % nslot)
        values(n_blocks - 2, (n_blocks - 2) % nslot)
        softmax(n_blocks - 1, (n_blocks - 1) % nslot)
        values(n_blocks - 1, (n_blocks - 1) % nslot)

    rows = 256

    def merge(c, carry):
        idx = pl.multiple_of(c * rows, rows)
        r = idx // seg
        nat = pl.ds(DEINT * (idx - r * seg) + r, rows, stride=DEINT)
        spans = [nat if dil == 1 else pl.ds(idx, rows) for dil in DILATIONS]
        ms = [acc_m[br, spans[br], :] for br in range(n_branch)]
        m_max = functools.reduce(jnp.maximum, ms)
        num = 0.0
        den = 0.0
        for br in range(n_branch):
            a = jnp.exp(ms[br] - m_max)
            num = num + a * acc_o[br, spans[br], :]
            den = den + a * acc_l[br, spans[br], :]
        o_ref[0, nat, :] = num / den
        return carry

    lax.fori_loop(0, s_len // rows, merge, 0, unroll=2)


def _dilated_attention(q, k, v, bias):
    b, s, da = q.shape
    pw = 2 * HEAD_DIM
    assert DILATIONS == (1, DEINT, DEINT * DEINT)
    seq = pl.BlockSpec((1, s, pw), lambda i, j: (i, 0, j))
    nbr = len(DILATIONS)
    stage = (ATTN_UNROLL, 2, ATTN_BLOCK, 2 * ATTN_BLOCK)
    return pl.pallas_call(
        _attn_body,
        grid=(b, da // pw),
        in_specs=[seq, seq, seq,
                  pl.BlockSpec((1,) + bias.shape[1:], lambda i, j: (j, 0, 0, 0, 0, 0))],
        out_specs=seq,
        out_shape=jax.ShapeDtypeStruct((b, s, da), F32),
        scratch_shapes=[pltpu.VMEM((s, pw), F32), pltpu.VMEM((3, s, pw), F32)]
        + [pltpu.VMEM((nbr, s, pw), F32)] * 3
        + [pltpu.VMEM(stage, F32), pltpu.VMEM(stage, BF16)],
        compiler_params=_cparams(("arbitrary", "arbitrary")),
        name="dilated_attention",
    )(q, k, v, bias)


def _conv_body(uc_ref, uh_ref, w_ref, cb_ref, lg_ref, lb_ref, o_ref, win, shifted):
    t = pl.program_id(1)
    tc = uc_ref.shape[1]
    win[0:CONV_HALO, :] = jnp.where(t > 0, uh_ref[0].astype(F32), 0.0)
    win[CONV_HALO:, :] = uc_ref[0].astype(F32)
    off = CONV_HALO - (CONV_WIDTH - 1)
    for b in range(SUBLANES):
        n_rows = tc + SUBLANES * ((CONV_WIDTH - 1 - b) // SUBLANES)
        shifted[b, 0:n_rows, :] = win[off + b:off + b + n_rows, :]
    for rb in range(tc // CONV_ROWS):
        r0 = rb * CONV_ROWS
        acc = jnp.zeros((CONV_ROWS, uc_ref.shape[2]), F32)
        for j in range(CONV_WIDTH):
            a, b = divmod(j, SUBLANES)
            rows = pl.ds(r0 + SUBLANES * a, CONV_ROWS)
            acc = acc + w_ref[j:j + 1, :] * shifted[b, rows, :]
        y = acc + cb_ref[...]
        mu = jnp.mean(y, axis=-1, keepdims=True)
        yc = y - mu
        var = jnp.mean(yc * yc, axis=-1, keepdims=True)
        z = yc * lax.rsqrt(var + NORM_EPS) * lg_ref[...] + lb_ref[...]
        o_ref[0, r0:r0 + CONV_ROWS, :] = (z * jax.nn.sigmoid(z)).astype(o_ref.dtype)


def _conformer_conv(u, conv_w, conv_b, ln_g, ln_b):
    b, s, dc = u.shape
    tc = SEQ_TILE
    per = tc // CONV_HALO
    row = pl.BlockSpec((1, dc), lambda i, j: (0, 0))
    return pl.pallas_call(
        _conv_body,
        grid=(b, s // tc),
        in_specs=[pl.BlockSpec((1, tc, dc), lambda i, j: (i, j, 0)),
                  pl.BlockSpec((1, CONV_HALO, dc), lambda i, j: (i, jnp.maximum(j * per - 1, 0), 0)),
                  pl.BlockSpec((CONV_WIDTH, dc), lambda i, j: (0, 0)),
                  row, row, row],
        out_specs=pl.BlockSpec((1, tc, dc), lambda i, j: (i, j, 0)),
        out_shape=jax.ShapeDtypeStruct((b, s, dc), BF16),
        scratch_shapes=[pltpu.VMEM((tc + CONV_HALO, dc), F32),
                        pltpu.VMEM((SUBLANES, tc + CONV_HALO - SUBLANES, dc), F32)],
        compiler_params=_cparams(("arbitrary", "arbitrary")),
        name="conformer_conv",
    )(u, u, conv_w, conv_b.reshape(1, dc), ln_g.reshape(1, dc), ln_b.reshape(1, dc))


def _route(logits, seen):
    rows = logits.shape[0]
    lane = lax.broadcasted_iota(jnp.int32, logits.shape, 1)
    big = jnp.int32(1 << 20)
    is_g = lane < N_GROUPS
    gl = jnp.where(is_g, logits, MASK_VALUE)
    gmax = jnp.max(gl, axis=-1, keepdims=True)
    grp = jnp.min(jnp.where(gl == gmax, lane, big), axis=-1, keepdims=True)
    gsum = jnp.sum(jnp.where(is_g, jnp.exp(gl - gmax), 0.0), axis=-1, keepdims=True)
    p_grp = 1.0 / gsum
    lo = N_GROUPS + grp * EXPERTS_PER_GROUP
    el = jnp.where((lane >= lo) & (lane < lo + EXPERTS_PER_GROUP), logits, MASK_VALUE)
    v1 = jnp.max(el, axis=-1, keepdims=True)
    i1 = jnp.min(jnp.where(el == v1, lane, big), axis=-1, keepdims=True)
    el2 = jnp.where(lane == i1, MASK_VALUE, el)
    v2 = jnp.max(el2, axis=-1, keepdims=True)
    i2 = jnp.min(jnp.where(el2 == v2, lane, big), axis=-1, keepdims=True)
    e = jnp.exp(v2 - v1)
    w1 = p_grp / (1.0 + e)
    w2 = p_grp * e / (1.0 + e)

    hit1 = lane == i1
    hit2 = lane == i2
    chosen = jnp.where(hit1 | hit2, 1.0, 0.0)
    ri = lax.broadcasted_iota(jnp.int32, (rows, rows), 0)
    ci = lax.broadcasted_iota(jnp.int32, (rows, rows), 1)
    earlier = jnp.where(ri > ci, 1.0, 0.0).astype(BF16)
    before = jnp.dot(earlier, chosen.astype(BF16), preferred_element_type=F32) + seen
    rank1 = jnp.sum(jnp.where(hit1, before, 0.0), axis=-1, keepdims=True)
    rank2 = jnp.sum(jnp.where(hit2, before, 0.0), axis=-1, keepdims=True)
    seen = seen + jnp.sum(chosen, axis=0, keepdims=True)

    rec = jnp.zeros(logits.shape, F32)
    for pos, val in ((REC_LANE0, i1.astype(F32)), (REC_LANE1, i2.astype(F32)), (REC_W0, w1),
                     (REC_W1, w2), (REC_RANK0, rank1), (REC_RANK1, rank2)):
        rec = jnp.where(lane == pos, val, rec)
    return rec, seen


def _split_bf16(x):
    hi = x.astype(BF16)
    lo = (x - hi.astype(F32)).astype(BF16)
    return hi, lo


def _outproj_body(a_ref, c_ref, x_ref, mod_ref, g_ref, w_ref, wrh_ref, wrl_ref, br_ref,
                  xo_ref, h_ref, r_ref, sel_ref, cnt_ref, seen):
    @pl.when((pl.program_id(0) == 0) & (pl.program_id(1) == 0))
    def _():
        seen[...] = jnp.zeros_like(seen)

    da = a_ref.shape[2]
    mix = jnp.dot(a_ref[0].astype(BF16), w_ref[0, 0:da, :], preferred_element_type=F32)
    mix = mix + jnp.dot(c_ref[0], w_ref[0, da:, :], preferred_element_type=F32)
    xn = x_ref[0] + mod_ref[0, 2:3, :] * mix
    xo_ref[0] = xn
    h = _rms_modulate(xn, g_ref[...], mod_ref[0, 3:4, :], mod_ref[0, 4:5, :])
    _store_token_tiles(h_ref, h)
    h_hi, h_lo = _split_bf16(h)
    logits = (jnp.dot(h_hi, wrh_ref[0], preferred_element_type=F32)
              + jnp.dot(h_lo, wrh_ref[0], preferred_element_type=F32)
              + jnp.dot(h_hi, wrl_ref[0], preferred_element_type=F32)) + br_ref[0]
    rec, new_seen = _route(logits, seen[...])
    r_ref[0] = rec
    sel_ref[...] = rec.T[0:SUBLANES, :]
    seen[...] = new_seen
    cnt_ref[...] = new_seen


def _output_projection(attn, conv, x, mod, g, w_out_bf, wr_hi, wr_lo, b_router, layer):
    b, s, d = x.shape
    ts = SEQ_TILE
    tile = lambda n: pl.BlockSpec((1, ts, n), lambda i, j: (i, j, 0))
    const = lambda r, c: pl.BlockSpec((r, c), lambda i, j: (0, 0))
    per_layer = lambda r, c: pl.BlockSpec((1, r, c), lambda i, j: (layer, 0, 0))
    return pl.pallas_call(
        _outproj_body,
        grid=(b, s // ts),
        in_specs=[tile(attn.shape[2]), tile(conv.shape[2]), tile(d),
                  pl.BlockSpec((1, N_MOD, d), lambda i, j: (i, 0, 0)),
                  const(1, d), per_layer(d, d), per_layer(d, ROUTE_LANES),
                  per_layer(d, ROUTE_LANES), per_layer(1, ROUTE_LANES)],
        out_specs=[tile(d), pl.BlockSpec((ts * SUBLANES, LANES), lambda i, j: (i * (s // ts) + j, 0)),
                   tile(ROUTE_LANES),
                   pl.BlockSpec((SUBLANES, ts), lambda i, j: (0, i * (s // ts) + j)),
                   const(1, ROUTE_LANES)],
        out_shape=[jax.ShapeDtypeStruct((b, s, d), F32),
                   jax.ShapeDtypeStruct((b * s * SUBLANES, LANES), F32),
                   jax.ShapeDtypeStruct((b, s, ROUTE_LANES), F32),
                   jax.ShapeDtypeStruct((SUBLANES, b * s), F32),
                   jax.ShapeDtypeStruct((1, ROUTE_LANES), F32)],
        scratch_shapes=[pltpu.VMEM((1, ROUTE_LANES), F32)],
        compiler_params=_cparams(("arbitrary", "arbitrary")),
        name="output_projection",
    )(attn, conv, x, mod, g.reshape(1, d), w_out_bf, wr_hi, wr_lo, b_router)


def _load_token_tiles(ref):
    n = ref.shape[0] // SUBLANES
    return jnp.concatenate([ref[pl.ds(a, n, stride=SUBLANES), :] for a in range(SUBLANES)], axis=1)


def _store_token_tiles(ref, val):
    n = val.shape[0]
    for a in range(SUBLANES):
        ref[pl.ds(a, n, stride=SUBLANES), :] = val[:, a * LANES:(a + 1) * LANES]


def _tile_of(ref, token):
    return ref.at[pl.ds(pl.multiple_of(token * SUBLANES, SUBLANES), SUBLANES)]


def _row_copy(src, src_token, dst, dst_token, sem):
    return pltpu.make_async_copy(_tile_of(src, src_token), _tile_of(dst, dst_token), sem)


def _wait_rows(like, n_tokens, sem):
    n = n_tokens * SUBLANES
    pltpu.make_async_copy(like.at[pl.ds(0, n)], like.at[pl.ds(0, n)], sem).wait()


def _dispatch_body(dest_ref, h_ref, xb_hbm, sem):
    td = h_ref.shape[0] // SUBLANES
    n_tok = dest_ref.shape[0] // 2
    base = pl.program_id(0) * td

    def issue(t, carry):
        for choice in range(2):
            _row_copy(h_ref, t, xb_hbm, dest_ref[choice * n_tok + base + t], sem).start()
        return carry

    lax.fori_loop(0, td, issue, 0, unroll=DMA_UNROLL)
    _wait_rows(xb_hbm, 2 * td, sem)


def _dispatch(h, dest, n_rows):
    t = h.shape[0] // SUBLANES
    td = min(DISPATCH_TILE, t)
    assert t % td == 0
    return pl.pallas_call(
        _dispatch_body,
        grid_spec=pltpu.PrefetchScalarGridSpec(
            num_scalar_prefetch=1,
            grid=(t // td,),
            in_specs=[pl.BlockSpec((td * SUBLANES, LANES), lambda i, *_: (i, 0))],
            out_specs=pl.BlockSpec(memory_space=pl.ANY),
            scratch_shapes=[pltpu.SemaphoreType.DMA(())]),
        out_shape=jax.ShapeDtypeStruct((n_rows * SUBLANES, LANES), h.dtype),
        compiler_params=_cparams(("arbitrary",)),
        name="moe_dispatch",
    )(dest, h)


def _expert_body(eid_ref, valid_ref, nused_ref, x_ref, wg_ref, wu_ref, wd_ref, o_ref,
                 wg_bf, wu_bf, wd_bf):
    i = pl.program_id(0)

    @pl.when(i < nused_ref[0])
    def _():
        changed = (i == 0) | (eid_ref[i] != eid_ref[jnp.maximum(i - 1, 0)])

        @pl.when(changed)
        def _():
            wg_bf[...] = wg_ref[0, 0].astype(BF16)
            wu_bf[...] = wu_ref[0, 0].astype(BF16)
            wd_bf[...] = wd_ref[0, 0].astype(BF16)

        x = _load_token_tiles(x_ref)
        row = lax.broadcasted_iota(jnp.int32, x.shape, 0)
        x = jnp.where(row < valid_ref[i], x, 0.0).astype(BF16)
        g = jnp.dot(x, wg_bf[...], preferred_element_type=F32)
        u = jnp.dot(x, wu_bf[...], preferred_element_type=F32)
        h = (g * jax.nn.sigmoid(g)) * u
        _store_token_tiles(o_ref, jnp.dot(h.astype(BF16), wd_bf[...], preferred_element_type=F32))

    @pl.when(i >= nused_ref[0])
    def _():
        o_ref[...] = jnp.zeros_like(o_ref)


def _experts(xb, blk_eid, blk_valid, n_used, w_gate, w_up, w_down, layer):
    p = xb.shape[0] // SUBLANES
    d, de = w_gate.shape[2:]
    tm = MOE_BLOCK
    last = lambda i, nu: jnp.minimum(i, nu[0] - 1)
    rows = pl.BlockSpec((tm * SUBLANES, LANES), lambda i, eid, nv, nu: (last(i, nu), 0))
    weight = lambda r, c: pl.BlockSpec(
        (1, 1, r, c), lambda i, eid, nv, nu: (layer, eid[last(i, nu)], 0, 0))
    return pl.pallas_call(
        _expert_body,
        grid_spec=pltpu.PrefetchScalarGridSpec(
            num_scalar_prefetch=3,
            grid=(p // tm,),
            in_specs=[rows, weight(d, de), weight(d, de), weight(de, d)],
            out_specs=pl.BlockSpec((tm * SUBLANES, LANES), lambda i, eid, nv, nu: (i, 0)),
            scratch_shapes=[pltpu.VMEM((d, de), BF16), pltpu.VMEM((d, de), BF16),
                            pltpu.VMEM((de, d), BF16)]),
        out_shape=jax.ShapeDtypeStruct((p * SUBLANES, LANES), F32),
        compiler_params=_cparams(("arbitrary",)),
        name="moe_experts",
    )(blk_eid, blk_valid, n_used, xb, w_gate, w_up, w_down)


def _gather_rows(dest_ref, yb_hbm, ybuf, sem, tile, into, n_tokens, unroll):
    n_tok = dest_ref.shape[0] // 2

    def issue(t, carry):
        for choice in range(2):
            src = dest_ref[choice * n_tok + tile * n_tokens + t]
            _row_copy(yb_hbm, src, ybuf.at[into, choice], t, sem.at[into, choice]).start()
        return carry

    lax.fori_loop(0, n_tokens, issue, 0, unroll=unroll)


def _gathered_moe(r_ref, ybuf, sem, slot, n_tokens):
    for choice in range(2):
        _wait_rows(ybuf.at[slot, choice], n_tokens, sem.at[slot, choice])
    return (r_ref[:, REC_W0:REC_W0 + 1] * _load_token_tiles(ybuf.at[slot, 0])
            + r_ref[:, REC_W1:REC_W1 + 1] * _load_token_tiles(ybuf.at[slot, 1]))


def _combine_final_body(dest_ref, x_ref, r_ref, mod_ref, g_ref, yb_hbm, o_ref, ybuf, sem):
    tcm = x_ref.shape[0]
    i = pl.program_id(0)
    slot = i % 2

    @pl.when(i == 0)
    def _():
        _gather_rows(dest_ref, yb_hbm, ybuf, sem, 0, 0, tcm, DMA_UNROLL)

    @pl.when(i + 1 < pl.num_programs(0))
    def _():
        _gather_rows(dest_ref, yb_hbm, ybuf, sem, i + 1, 1 - slot, tcm, DMA_UNROLL)

    xn = x_ref[...] + mod_ref[0, 5:6, :] * _gathered_moe(r_ref, ybuf, sem, slot, tcm)
    ms = jnp.mean(xn * xn, axis=-1, keepdims=True)
    o_ref[...] = (xn * lax.rsqrt(ms + NORM_EPS)) * g_ref[...]


def _combine_final(x, route, mod, g_final, yb, dest, seq_len):
    t, d = x.shape
    tcm = COMBINE_TILE
    per_seq = seq_len // tcm
    return pl.pallas_call(
        _combine_final_body,
        grid_spec=pltpu.PrefetchScalarGridSpec(
            num_scalar_prefetch=1,
            grid=(t // tcm,),
            in_specs=[pl.BlockSpec((tcm, d), lambda i, *_: (i, 0)),
                      pl.BlockSpec((tcm, ROUTE_LANES), lambda i, *_: (i, 0)),
                      pl.BlockSpec((1, N_MOD, d), lambda i, *_: (i // per_seq, 0, 0)),
                      pl.BlockSpec((1, d), lambda i, *_: (0, 0)),
                      pl.BlockSpec(memory_space=pl.ANY)],
            out_specs=pl.BlockSpec((tcm, d), lambda i, *_: (i, 0)),
            scratch_shapes=[pltpu.VMEM((2, 2, tcm * SUBLANES, LANES), F32),
                            pltpu.SemaphoreType.DMA((2, 2))]),
        out_shape=jax.ShapeDtypeStruct((t, d), F32),
        compiler_params=_cparams(("arbitrary",)),
        name="moe_combine_final",
    )(dest, x, route, mod, g_final.reshape(1, d), yb)


def _combine_inproj_body(dest_ref, x_ref, r_ref, modp_ref, mod_ref, g_ref, w_ref, yb_hbm,
                         xo_ref, q_ref, k_ref, v_ref, u_ref, ybuf, sem):
    ts = x_ref.shape[1]
    per_seq = pl.num_programs(1)
    n_tiles = pl.num_programs(0) * per_seq
    i = pl.program_id(0) * per_seq + pl.program_id(1)
    slot = i % 2

    @pl.when(i == 0)
    def _():
        _gather_rows(dest_ref, yb_hbm, ybuf, sem, 0, 0, ts, DMA_UNROLL)

    moe = _gathered_moe(r_ref, ybuf, sem, slot, ts)
    _gather_rows(dest_ref, yb_hbm, ybuf, sem, jnp.minimum(i + 1, n_tiles - 1), 1 - slot, ts, True)
    xn = x_ref[0] + modp_ref[0, 5:6, :] * moe
    xo_ref[0] = xn
    _project_in(xn, mod_ref, g_ref, w_ref, q_ref, k_ref, v_ref, u_ref)

    @pl.when(i == n_tiles - 1)
    def _():
        for choice in range(2):
            _wait_rows(ybuf.at[1 - slot, choice], ts, sem.at[1 - slot, choice])


def _combine_input_projection(x, route, mod_prev, yb, dest, mod, g, w_in_bf, layer):
    b, s, d = x.shape
    din = w_in_bf.shape[2]
    dc = (din - 3 * D_ATTN) // 2
    ts = SEQ_TILE
    per_seq = s // ts
    tile = lambda n: pl.BlockSpec((1, ts, n), lambda i, j, *_: (i, j, 0))
    mods = pl.BlockSpec((1, N_MOD, d), lambda i, j, *_: (i, 0, 0))
    return pl.pallas_call(
        _combine_inproj_body,
        grid_spec=pltpu.PrefetchScalarGridSpec(
            num_scalar_prefetch=1,
            grid=(b, per_seq),
            in_specs=[tile(d),
                      pl.BlockSpec((ts, ROUTE_LANES), lambda i, j, *_: (i * per_seq + j, 0)),
                      mods, mods,
                      pl.BlockSpec((1, d), lambda i, j, *_: (0, 0)),
                      pl.BlockSpec((1, d, din), lambda i, j, *_: (layer, 0, 0)),
                      pl.BlockSpec(memory_space=pl.ANY)],
            out_specs=[tile(d), tile(D_ATTN), tile(D_ATTN), tile(D_ATTN), tile(dc)],
            scratch_shapes=[pltpu.VMEM((2, 2, ts * SUBLANES, LANES), F32),
                            pltpu.SemaphoreType.DMA((2, 2))]),
        out_shape=[jax.ShapeDtypeStruct((b, s, d), F32)]
        + [jax.ShapeDtypeStruct((b, s, D_ATTN), BF16)] * 3
        + [jax.ShapeDtypeStruct((b, s, dc), BF16)],
        compiler_params=_cparams(("arbitrary", "arbitrary")),
        name="combine_input_projection",
    )(dest, x, route, mod_prev, mod, g.reshape(1, d), w_in_bf, yb)


def _plan_blocks(counts, n_blocks):
    tm = MOE_BLOCK
    experts = jnp.arange(N_EXPERTS, dtype=jnp.int32)
    padded = ((counts + tm - 1) // tm) * tm
    pad_end = jnp.sum(jnp.where(experts[:, None] <= experts[None, :], padded[:, None], 0), axis=0)
    pad_start = pad_end - padded
    blk_row = jnp.arange(n_blocks, dtype=jnp.int32)[:, None] * tm
    inside = (pad_start[None, :] <= blk_row) & (blk_row < pad_end[None, :])
    blk_eid = jnp.where(jnp.any(inside, axis=1),
                        jnp.sum(jnp.where(inside, experts[None, :], 0), axis=1), N_EXPERTS - 1)
    rows_left = jnp.clip((pad_start + counts)[None, :] - blk_row, 0, tm)
    blk_valid = jnp.sum(jnp.where(inside, rows_left, 0), axis=1)
    n_used = jnp.sum(padded, keepdims=True) // tm
    i32 = lambda a: a.astype(jnp.int32)
    return i32(pad_start), i32(blk_eid), i32(blk_valid), i32(n_used)


def kernel(x, c, w_mod, b_mod, g_norm1, w_in, conv_w, conv_b, conv_ln_g, conv_ln_b, w_out, g_norm2, w_router_group, b_router_group, w_router_expert, b_router_expert, w_exp_gate, w_exp_up, w_exp_down, g_final):
    b, s, d = x.shape
    depth = w_mod.shape[0]
    t = b * s
    assert d == SUBLANES * LANES
    assert s % (ATTN_BLOCK * max(DILATIONS)) == 0 and s % SEQ_TILE == 0
    assert s % COMBINE_TILE == 0
    n_blocks = -(-2 * t // MOE_BLOCK) + N_EXPERTS
    n_rows = n_blocks * MOE_BLOCK

    mod_all = _modulation(c, w_mod, b_mod).reshape(depth, b, N_MOD, d)
    bias = _attn_bias_table()
    pad = ROUTE_LANES - N_GROUPS - N_EXPERTS
    w_router = jnp.concatenate(
        [w_router_group, w_router_expert, jnp.zeros((depth, d, pad), F32)], axis=-1)
    b_router = jnp.concatenate(
        [b_router_group, b_router_expert, jnp.zeros((depth, pad), F32)], axis=-1)
    b_router = b_router.reshape(depth, 1, ROUTE_LANES)
    wr_hi = w_router.astype(BF16)
    wr_lo = (w_router - wr_hi.astype(F32)).astype(BF16)
    w_in_bf = w_in.astype(BF16)
    w_out_bf = w_out.astype(BF16)

    pending = None
    for l in range(depth):
        mod = mod_all[l]
        if pending is None:
            q, k, v, u = _input_projection(x, mod, g_norm1[l], w_in_bf, l)
        else:
            x, q, k, v, u = _combine_input_projection(x, *pending, mod, g_norm1[l], w_in_bf, l)
        attn = _dilated_attention(q, k, v, bias)
        conv = _conformer_conv(u, conv_w[l], conv_b[l], conv_ln_g[l], conv_ln_b[l])
        x, h2, route, sel, seen = _output_projection(attn, conv, x, mod, g_norm2[l], w_out_bf,
                                                     wr_hi, wr_lo, b_router, l)
        route = route.reshape(t, ROUTE_LANES)
        counts = seen[0, N_GROUPS:N_GROUPS + N_EXPERTS].astype(jnp.int32)
        starts, blk_eid, blk_valid, n_used = _plan_blocks(counts, n_blocks)
        eid = sel[REC_LANE0:REC_LANE1 + 1].astype(jnp.int32) - N_GROUPS
        ranks = sel[REC_RANK0:REC_RANK1 + 1].astype(jnp.int32)
        onehot = eid[None] == jnp.arange(N_EXPERTS, dtype=jnp.int32)[:, None, None]
        dest = (jnp.sum(jnp.where(onehot, starts[:, None, None], 0), axis=0) + ranks).reshape(-1)
        xb = _dispatch(h2, dest, n_rows)
        yb = _experts(xb, blk_eid, blk_valid, n_used, w_exp_gate, w_exp_up, w_exp_down, l)
        pending = (route, mod, yb, dest)
    route, mod, yb, dest = pending
    return _combine_final(x.reshape(t, d), route, mod, g_final, yb, dest, s).reshape(b, s, d)
```

```python
import functools

import numpy as np
import jax
import jax.numpy as jnp
from jax import lax
from jax.experimental import pallas as pl
from jax.experimental.pallas import tpu as pltpu

F32 = jnp.float32
BF16 = jnp.bfloat16

SUBLANES = 8
LANES = 128
HEAD_DIM = 64
N_HEADS = 8
D_ATTN = N_HEADS * HEAD_DIM
CONV_WIDTH = 31
DILATIONS = (1, 4, 16)
ATTN_BLOCK = 128
ATTN_UNROLL = 15
DEINT = 4
N_GROUPS = 4
EXPERTS_PER_GROUP = 8
N_EXPERTS = N_GROUPS * EXPERTS_PER_GROUP
NORM_EPS = 1e-6
N_MOD = 6
MASK_VALUE = -1e30
ROUTE_LANES = 128
REC_LANE0, REC_LANE1, REC_W0, REC_W1, REC_RANK0, REC_RANK1 = range(6)

SEQ_TILE = 512
CONV_HALO = 32
CONV_ROWS = 64
MOE_BLOCK = 512
DISPATCH_TILE = 4096
COMBINE_TILE = 512
DMA_UNROLL = 8
VMEM_LIMIT = 56 * 1024 * 1024


def _cparams(sem):
    return pltpu.CompilerParams(dimension_semantics=sem, vmem_limit_bytes=VMEM_LIMIT)


def _mod_body(c_ref, w_ref, b_ref, o_ref):
    c = c_ref[...]
    c_act = c * jax.nn.sigmoid(c)
    o_ref[0] = jnp.dot(c_act, w_ref[0], preferred_element_type=F32,
                       precision=lax.Precision.HIGHEST) + b_ref[0]


def _modulation(c, w_mod, b_mod):
    depth, d, dm = w_mod.shape
    b = c.shape[0]
    nt = dm // d
    return pl.pallas_call(
        _mod_body,
        grid=(depth, nt),
        in_specs=[pl.BlockSpec((b, d), lambda l, j: (0, 0)),
                  pl.BlockSpec((1, d, d), lambda l, j: (l, 0, j)),
                  pl.BlockSpec((1, 1, d), lambda l, j: (l, 0, j))],
        out_specs=pl.BlockSpec((1, b, d), lambda l, j: (l, 0, j)),
        out_shape=jax.ShapeDtypeStruct((depth, b, dm), F32),
        compiler_params=_cparams(("arbitrary", "arbitrary")),
        name="modulation",
    )(c, w_mod, b_mod.reshape(depth, 1, dm))


def _rms_modulate(x, g, shift, scale):
    ms = jnp.mean(x * x, axis=-1, keepdims=True)
    return (x * lax.rsqrt(ms + NORM_EPS)) * g * (1.0 + scale) + shift


def _project_in(x, mod_ref, g_ref, w_ref, q_ref, k_ref, v_ref, u_ref):
    h = _rms_modulate(x, g_ref[...], mod_ref[0, 0:1, :], mod_ref[0, 1:2, :])
    y = jnp.dot(h.astype(BF16), w_ref[0], preferred_element_type=F32)
    da = D_ATTN
    q_ref[0] = (y[:, 0:da] * (HEAD_DIM ** -0.5)).astype(q_ref.dtype)
    k_ref[0] = y[:, da:2 * da].astype(k_ref.dtype)
    v_ref[0] = y[:, 2 * da:3 * da].astype(v_ref.dtype)
    dc = (y.shape[1] - 3 * da) // 2
    a = y[:, 3 * da:3 * da + dc]
    b = y[:, 3 * da + dc:]
    u_ref[0] = (a * jax.nn.sigmoid(b)).astype(u_ref.dtype)


def _inproj_body(x_ref, mod_ref, g_ref, w_ref, q_ref, k_ref, v_ref, u_ref):
    _project_in(x_ref[0], mod_ref, g_ref, w_ref, q_ref, k_ref, v_ref, u_ref)


def _input_projection(x, mod, g, w_in_bf, layer):
    b, s, d = x.shape
    din = w_in_bf.shape[2]
    dc = (din - 3 * D_ATTN) // 2
    ts = SEQ_TILE
    tile = lambda n: pl.BlockSpec((1, ts, n), lambda i, j: (i, j, 0))
    return pl.pallas_call(
        _inproj_body,
        grid=(b, s // ts),
        in_specs=[tile(d),
                  pl.BlockSpec((1, N_MOD, d), lambda i, j: (i, 0, 0)),
                  pl.BlockSpec((1, d), lambda i, j: (0, 0)),
                  pl.BlockSpec((1, d, din), lambda i, j: (layer, 0, 0))],
        out_specs=[tile(D_ATTN), tile(D_ATTN), tile(D_ATTN), tile(dc)],
        out_shape=[jax.ShapeDtypeStruct((b, s, D_ATTN), BF16)] * 3
        + [jax.ShapeDtypeStruct((b, s, dc), BF16)],
        compiler_params=_cparams(("arbitrary", "arbitrary")),
        name="input_projection",
    )(x, mod, g.reshape(1, d), w_in_bf)


def _attn_bias_table():
    blk = ATTN_BLOCK
    slopes = jnp.asarray(2.0 ** (-8.0 * np.arange(1, N_HEADS + 1) / N_HEADS), dtype=F32)
    qi = np.arange(blk)[:, None] + blk
    ki = np.arange(2 * blk)[None, :]
    delta = qi - ki
    in_band = (delta >= 0) & (delta <= blk)
    tables = []
    for dil in DILATIONS:
        bias = -slopes[:, None, None] * jnp.asarray(delta * dil, dtype=F32)
        later = jnp.where(in_band[None], bias, MASK_VALUE)
        first = jnp.where((in_band & (ki >= blk))[None], bias, MASK_VALUE)
        tables.append(jnp.stack([later, first], axis=1))
    t = jnp.stack(tables, axis=1)
    return t.reshape(N_HEADS // 2, 2, len(DILATIONS), 2, blk, 2 * blk)


def _attn_body(q_ref, k_ref, v_ref, bias_ref, o_ref, xf, x4, acc_o, acc_l, acc_m, s_buf, p_buf):
    blk = ATTN_BLOCK
    pw = 2 * HEAD_DIM
    s_len = q_ref.shape[1]
    seg = s_len // DEINT
    left = lax.broadcasted_iota(jnp.int32, (blk, pw), 1) < HEAD_DIM
    trans_b = (((1,), (1,)), ((), ()))
    n_branch = len(DILATIONS)
    n_blocks = s_len // blk
    natural = (q_ref.at[0], k_ref.at[0], v_ref.at[0])
    deint = (x4.at[0], x4.at[1], x4.at[2])

    for t, ref in enumerate(natural):
        xf[...] = ref[...].astype(F32)
        for r in range(DEINT):
            x4[t, r * seg:(r + 1) * seg, :] = xf[pl.ds(r, seg, stride=DEINT), :]

    for br, dil in enumerate(DILATIONS):
        nb = s_len // (blk * dil)
        step = dil // DEINT if dil > DEINT else 1
        srcs = natural if dil == 1 else deint

        def starts(j, dil=dil, nb=nb, step=step):
            r = j // nb
            n = j % nb
            if dil == 1:
                base = 0
            else:
                base = (r % DEINT) * seg + r // DEINT
            qs = base + n * (blk * step)
            ks = base + jnp.maximum(n - 1, 0) * (blk * step)
            return qs, ks, jnp.where(n == 0, 1, 0)

        def span(start, step=step):
            return pl.ds(start, blk) if step == 1 else pl.ds(start, blk, stride=step)

        def both(ref, qs, ks):
            return jnp.concatenate([ref[span(ks), :], ref[span(qs), :]], axis=0).astype(BF16)

        def scores(j, slot, br=br, srcs=srcs):
            qs, ks, first = starts(j)
            q = srcs[0][span(qs), :].astype(BF16)
            kk = both(srcs[1], qs, ks)
            zero = jnp.zeros_like(q)
            q2 = jnp.concatenate([jnp.where(left, q, zero), jnp.where(left, zero, q)], axis=0)
            s = lax.dot_general(q2, kk, trans_b, preferred_element_type=F32)
            s_buf[slot] = s.reshape(2, blk, 2 * blk) + bias_ref[0, :, br, first]

        def softmax(j, slot, br=br):
            qs, _, _ = starts(j)
            maxes = []
            for hh in range(2):
                s = s_buf[slot, hh]
                m = jnp.max(s, axis=-1, keepdims=True)
                p_buf[slot, hh] = jnp.exp(s - m).astype(BF16)
                maxes.append(m)
            acc_m[br, span(qs), :] = jnp.where(left, maxes[0], maxes[1])

        def values(j, slot, br=br, srcs=srcs):
            qs, ks, _ = starts(j)
            vv = both(srcs[2], qs, ks)
            rhs = jnp.concatenate([vv, jnp.ones_like(vv)], axis=1)
            o = jnp.dot(p_buf[slot].reshape(2 * blk, 2 * blk), rhs, preferred_element_type=F32)
            acc_o[br, span(qs), :] = jnp.where(left, o[0:blk, 0:pw], o[blk:, 0:pw])
            acc_l[br, span(qs), :] = jnp.where(left, o[0:blk, pw:], o[blk:, pw:])

        nslot = ATTN_UNROLL
        scores(0, 0)
        softmax(0, 0)
        scores(1, 1)

        def steady(i, carry):
            for u in range(ATTN_UNROLL):
                j = 2 + ATTN_UNROLL * i + u
                values(j - 2, u % nslot)
                softmax(j - 1, (u + 1) % nslot)
                scores(j, (u + 2) % nslot)
            return carry

        trips = (n_blocks - 2) // ATTN_UNROLL
        lax.fori_loop(0, trips, steady, 0)
        for j in range(2 + ATTN_UNROLL * trips, n_blocks):
            values(j - 2, (j - 2) % nslot)
            softmax(j - 1, (j - 1) % nslot)
            scores(j, j % nslot)
        values(n_blocks - 2, (n_blocks - 2) % nslot)
        softmax(n_blocks - 1, (n_blocks - 1) % nslot)
        values(n_blocks - 1, (n_blocks - 1) % nslot)

    rows = 256

    def merge(c, carry):
        idx = pl.multiple_of(c * rows, rows)
        r = idx // seg
        nat = pl.ds(DEINT * (idx - r * seg) + r, rows, stride=DEINT)
        spans = [nat if dil == 1 else pl.ds(idx, rows) for dil in DILATIONS]
        ms = [acc_m[br, spans[br], :] for br in range(n_branch)]
        m_max = functools.reduce(jnp.maximum, ms)
        num = 0.0
        den = 0.0
        for br in range(n_branch):
            a = jnp.exp(ms[br] - m_max)
            num = num + a * acc_o[br, spans[br], :]
            den = den + a * acc_l[br, spans[br], :]
        o_ref[0, nat, :] = num / den
        return carry

    lax.fori_loop(0, s_len // rows, merge, 0, unroll=2)


def _dilated_attention(q, k, v, bias):
    b, s, da = q.shape
    pw = 2 * HEAD_DIM
    assert DILATIONS == (1, DEINT, DEINT * DEINT)
    seq = pl.BlockSpec((1, s, pw), lambda i, j: (i, 0, j))
    nbr = len(DILATIONS)
    stage = (ATTN_UNROLL, 2, ATTN_BLOCK, 2 * ATTN_BLOCK)
    return pl.pallas_call(
        _attn_body,
        grid=(b, da // pw),
        in_specs=[seq, seq, seq,
                  pl.BlockSpec((1,) + bias.shape[1:], lambda i, j: (j, 0, 0, 0, 0, 0))],
        out_specs=seq,
        out_shape=jax.ShapeDtypeStruct((b, s, da), F32),
        scratch_shapes=[pltpu.VMEM((s, pw), F32), pltpu.VMEM((3, s, pw), F32)]
        + [pltpu.VMEM((nbr, s, pw), F32)] * 3
        + [pltpu.VMEM(stage, F32), pltpu.VMEM(stage, BF16)],
        compiler_params=_cparams(("arbitrary", "arbitrary")),
        name="dilated_attention",
    )(q, k, v, bias)


def _conv_body(uc_ref, uh_ref, w_ref, cb_ref, lg_ref, lb_ref, o_ref, win, shifted):
    t = pl.program_id(1)
    tc = uc_ref.shape[1]
    win[0:CONV_HALO, :] = jnp.where(t > 0, uh_ref[0].astype(F32), 0.0)
    win[CONV_HALO:, :] = uc_ref[0].astype(F32)
    off = CONV_HALO - (CONV_WIDTH - 1)
    for b in range(SUBLANES):
        n_rows = tc + SUBLANES * ((CONV_WIDTH - 1 - b) // SUBLANES)
        shifted[b, 0:n_rows, :] = win[off + b:off + b + n_rows, :]
    for rb in range(tc // CONV_ROWS):
        r0 = rb * CONV_ROWS
        acc = jnp.zeros((CONV_ROWS, uc_ref.shape[2]), F32)
        for j in range(CONV_WIDTH):
            a, b = divmod(j, SUBLANES)
            rows = pl.ds(r0 + SUBLANES * a, CONV_ROWS)
            acc = acc + w_ref[j:j + 1, :] * shifted[b, rows, :]
        y = acc + cb_ref[...]
        mu = jnp.mean(y, axis=-1, keepdims=True)
        yc = y - mu
        var = jnp.mean(yc * yc, axis=-1, keepdims=True)
        z = yc * lax.rsqrt(var + NORM_EPS) * lg_ref[...] + lb_ref[...]
        o_ref[0, r0:r0 + CONV_ROWS, :] = (z * jax.nn.sigmoid(z)).astype(o_ref.dtype)


def _conformer_conv(u, conv_w, conv_b, ln_g, ln_b):
    b, s, dc = u.shape
    tc = SEQ_TILE
    per = tc // CONV_HALO
    row = pl.BlockSpec((1, dc), lambda i, j: (0, 0))
    return pl.pallas_call(
        _conv_body,
        grid=(b, s // tc),
        in_specs=[pl.BlockSpec((1, tc, dc), lambda i, j: (i, j, 0)),
                  pl.BlockSpec((1, CONV_HALO, dc), lambda i, j: (i, jnp.maximum(j * per - 1, 0), 0)),
                  pl.BlockSpec((CONV_WIDTH, dc), lambda i, j: (0, 0)),
                  row, row, row],
        out_specs=pl.BlockSpec((1, tc, dc), lambda i, j: (i, j, 0)),
        out_shape=jax.ShapeDtypeStruct((b, s, dc), BF16),
        scratch_shapes=[pltpu.VMEM((tc + CONV_HALO, dc), F32),
                        pltpu.VMEM((SUBLANES, tc + CONV_HALO - SUBLANES, dc), F32)],
        compiler_params=_cparams(("arbitrary", "arbitrary")),
        name="conformer_conv",
    )(u, u, conv_w, conv_b.reshape(1, dc), ln_g.reshape(1, dc), ln_b.reshape(1, dc))


def _route(logits, seen):
    rows = logits.shape[0]
    lane = lax.broadcasted_iota(jnp.int32, logits.shape, 1)
    big = jnp.int32(1 << 20)
    is_g = lane < N_GROUPS
    gl = jnp.where(is_g, logits, MASK_VALUE)
    gmax = jnp.max(gl, axis=-1, keepdims=True)
    grp = jnp.min(jnp.where(gl == gmax, lane, big), axis=-1, keepdims=True)
    gsum = jnp.sum(jnp.where(is_g, jnp.exp(gl - gmax), 0.0), axis=-1, keepdims=True)
    p_grp = 1.0 / gsum
    lo = N_GROUPS + grp * EXPERTS_PER_GROUP
    el = jnp.where((lane >= lo) & (lane < lo + EXPERTS_PER_GROUP), logits, MASK_VALUE)
    v1 = jnp.max(el, axis=-1, keepdims=True)
    i1 = jnp.min(jnp.where(el == v1, lane, big), axis=-1, keepdims=True)
    el2 = jnp.where(lane == i1, MASK_VALUE, el)
    v2 = jnp.max(el2, axis=-1, keepdims=True)
    i2 = jnp.min(jnp.where(el2 == v2, lane, big), axis=-1, keepdims=True)
    e = jnp.exp(v2 - v1)
    w1 = p_grp / (1.0 + e)
    w2 = p_grp * e / (1.0 + e)

    hit1 = lane == i1
    hit2 = lane == i2
    chosen = jnp.where(hit1 | hit2, 1.0, 0.0)
    ri = lax.broadcasted_iota(jnp.int32, (rows, rows), 0)
    ci = lax.broadcasted_iota(jnp.int32, (rows, rows), 1)
    earlier = jnp.where(ri > ci, 1.0, 0.0).astype(BF16)
    before = jnp.dot(earlier, chosen.astype(BF16), preferred_element_type=F32) + seen
    rank1 = jnp.sum(jnp.where(hit1, before, 0.0), axis=-1, keepdims=True)
    rank2 = jnp.sum(jnp.where(hit2, before, 0.0), axis=-1, keepdims=True)
    seen = seen + jnp.sum(chosen, axis=0, keepdims=True)

    rec = jnp.zeros(logits.shape, F32)
    for pos, val in ((REC_LANE0, i1.astype(F32)), (REC_LANE1, i2.astype(F32)), (REC_W0, w1),
                     (REC_W1, w2), (REC_RANK0, rank1), (REC_RANK1, rank2)):
        rec = jnp.where(lane == pos, val, rec)
    return rec, seen


def _split_bf16(x):
    hi = x.astype(BF16)
    lo = (x - hi.astype(F32)).astype(BF16)
    return hi, lo


def _outproj_body(a_ref, c_ref, x_ref, mod_ref, g_ref, w_ref, wrh_ref, wrl_ref, br_ref,
                  xo_ref, h_ref, r_ref, sel_ref, cnt_ref, seen):
    @pl.when((pl.program_id(0) == 0) & (pl.program_id(1) == 0))
    def _():
        seen[...] = jnp.zeros_like(seen)

    da = a_ref.shape[2]
    mix = jnp.dot(a_ref[0].astype(BF16), w_ref[0, 0:da, :], preferred_element_type=F32)
    mix = mix + jnp.dot(c_ref[0], w_ref[0, da:, :], preferred_element_type=F32)
    xn = x_ref[0] + mod_ref[0, 2:3, :] * mix
    xo_ref[0] = xn
    h = _rms_modulate(xn, g_ref[...], mod_ref[0, 3:4, :], mod_ref[0, 4:5, :])
    _store_token_tiles(h_ref, h)
    h_hi, h_lo = _split_bf16(h)
    logits = (jnp.dot(h_hi, wrh_ref[0], preferred_element_type=F32)
              + jnp.dot(h_lo, wrh_ref[0], preferred_element_type=F32)
              + jnp.dot(h_hi, wrl_ref[0], preferred_element_type=F32)) + br_ref[0]
    rec, new_seen = _route(logits, seen[...])
    r_ref[0] = rec
    sel_ref[...] = rec.T[0:SUBLANES, :]
    seen[...] = new_seen
    cnt_ref[...] = new_seen


def _output_projection(attn, conv, x, mod, g, w_out_bf, wr_hi, wr_lo, b_router, layer):
    b, s, d = x.shape
    ts = SEQ_TILE
    tile = lambda n: pl.BlockSpec((1, ts, n), lambda i, j: (i, j, 0))
    const = lambda r, c: pl.BlockSpec((r, c), lambda i, j: (0, 0))
    per_layer = lambda r, c: pl.BlockSpec((1, r, c), lambda i, j: (layer, 0, 0))
    return pl.pallas_call(
        _outproj_body,
        grid=(b, s // ts),
        in_specs=[tile(attn.shape[2]), tile(conv.shape[2]), tile(d),
                  pl.BlockSpec((1, N_MOD, d), lambda i, j: (i, 0, 0)),
                  const(1, d), per_layer(d, d), per_layer(d, ROUTE_LANES),
                  per_layer(d, ROUTE_LANES), per_layer(1, ROUTE_LANES)],
        out_specs=[tile(d), pl.BlockSpec((ts * SUBLANES, LANES), lambda i, j: (i * (s // ts) + j, 0)),
                   tile(ROUTE_LANES),
                   pl.BlockSpec((SUBLANES, ts), lambda i, j: (0, i * (s // ts) + j)),
                   const(1, ROUTE_LANES)],
        out_shape=[jax.ShapeDtypeStruct((b, s, d), F32),
                   jax.ShapeDtypeStruct((b * s * SUBLANES, LANES), F32),
                   jax.ShapeDtypeStruct((b, s, ROUTE_LANES), F32),
                   jax.ShapeDtypeStruct((SUBLANES, b * s), F32),
                   jax.ShapeDtypeStruct((1, ROUTE_LANES), F32)],
        scratch_shapes=[pltpu.VMEM((1, ROUTE_LANES), F32)],
        compiler_params=_cparams(("arbitrary", "arbitrary")),
        name="output_projection",
    )(attn, conv, x, mod, g.reshape(1, d), w_out_bf, wr_hi, wr_lo, b_router)


def _load_token_tiles(ref):
    n = ref.shape[0] // SUBLANES
    return jnp.concatenate([ref[pl.ds(a, n, stride=SUBLANES), :] for a in range(SUBLANES)], axis=1)


def _store_token_tiles(ref, val):
    n = val.shape[0]
    for a in range(SUBLANES):
        ref[pl.ds(a, n, stride=SUBLANES), :] = val[:, a * LANES:(a + 1) * LANES]


def _tile_of(ref, token):
    return ref.at[pl.ds(pl.multiple_of(token * SUBLANES, SUBLANES), SUBLANES)]


def _row_copy(src, src_token, dst, dst_token, sem):
    return pltpu.make_async_copy(_tile_of(src, src_token), _tile_of(dst, dst_token), sem)


def _wait_rows(like, n_tokens, sem):
    n = n_tokens * SUBLANES
    pltpu.make_async_copy(like.at[pl.ds(0, n)], like.at[pl.ds(0, n)], sem).wait()


def _dispatch_body(dest_ref, h_ref, xb_hbm, sem):
    td = h_ref.shape[0] // SUBLANES
    n_tok = dest_ref.shape[0] // 2
    base = pl.program_id(0) * td

    def issue(t, carry):
        for choice in range(2):
            _row_copy(h_ref, t, xb_hbm, dest_ref[choice * n_tok + base + t], sem).start()
        return carry

    lax.fori_loop(0, td, issue, 0, unroll=DMA_UNROLL)
    _wait_rows(xb_hbm, 2 * td, sem)


def _dispatch(h, dest, n_rows):
    t = h.shape[0] // SUBLANES
    td = min(DISPATCH_TILE, t)
    assert t % td == 0
    return pl.pallas_call(
        _dispatch_body,
        grid_spec=pltpu.PrefetchScalarGridSpec(
            num_scalar_prefetch=1,
            grid=(t // td,),
            in_specs=[pl.BlockSpec((td * SUBLANES, LANES), lambda i, *_: (i, 0))],
            out_specs=pl.BlockSpec(memory_space=pl.ANY),
            scratch_shapes=[pltpu.SemaphoreType.DMA(())]),
        out_shape=jax.ShapeDtypeStruct((n_rows * SUBLANES, LANES), h.dtype),
        compiler_params=_cparams(("arbitrary",)),
        name="moe_dispatch",
    )(dest, h)


def _expert_body(eid_ref, valid_ref, nused_ref, x_ref, wg_ref, wu_ref, wd_ref, o_ref,
                 wg_bf, wu_bf, wd_bf):
    i = pl.program_id(0)

    @pl.when(i < nused_ref[0])
    def _():
        changed = (i == 0) | (eid_ref[i] != eid_ref[jnp.maximum(i - 1, 0)])

        @pl.when(changed)
        def _():
            wg_bf[...] = wg_ref[0, 0].astype(BF16)
            wu_bf[...] = wu_ref[0, 0].astype(BF16)
            wd_bf[...] = wd_ref[0, 0].astype(BF16)

        x = _load_token_tiles(x_ref)
        row = lax.broadcasted_iota(jnp.int32, x.shape, 0)
        x = jnp.where(row < valid_ref[i], x, 0.0).astype(BF16)
        g = jnp.dot(x, wg_bf[...], preferred_element_type=F32)
        u = jnp.dot(x, wu_bf[...], preferred_element_type=F32)
        h = (g * jax.nn.sigmoid(g)) * u
        _store_token_tiles(o_ref, jnp.dot(h.astype(BF16), wd_bf[...], preferred_element_type=F32))

    @pl.when(i >= nused_ref[0])
    def _():
        o_ref[...] = jnp.zeros_like(o_ref)


def _experts(xb, blk_eid, blk_valid, n_used, w_gate, w_up, w_down, layer):
    p = xb.shape[0] // SUBLANES
    d, de = w_gate.shape[2:]
    tm = MOE_BLOCK
    last = lambda i, nu: jnp.minimum(i, nu[0] - 1)
    rows = pl.BlockSpec((tm * SUBLANES, LANES), lambda i, eid, nv, nu: (last(i, nu), 0))
    weight = lambda r, c: pl.BlockSpec(
        (1, 1, r, c), lambda i, eid, nv, nu: (layer, eid[last(i, nu)], 0, 0))
    return pl.pallas_call(
        _expert_body,
        grid_spec=pltpu.PrefetchScalarGridSpec(
            num_scalar_prefetch=3,
            grid=(p // tm,),
            in_specs=[rows, weight(d, de), weight(d, de), weight(de, d)],
            out_specs=pl.BlockSpec((tm * SUBLANES, LANES), lambda i, eid, nv, nu: (i, 0)),
            scratch_shapes=[pltpu.VMEM((d, de), BF16), pltpu.VMEM((d, de), BF16),
                            pltpu.VMEM((de, d), BF16)]),
        out_shape=jax.ShapeDtypeStruct((p * SUBLANES, LANES), F32),
        compiler_params=_cparams(("arbitrary",)),
        name="moe_experts",
    )(blk_eid, blk_valid, n_used, xb, w_gate, w_up, w_down)


def _gather_rows(dest_ref, yb_hbm, ybuf, sem, tile, into, n_tokens, unroll):
    n_tok = dest_ref.shape[0] // 2

    def issue(t, carry):
        for choice in range(2):
            src = dest_ref[choice * n_tok + tile * n_tokens + t]
            _row_copy(yb_hbm, src, ybuf.at[into, choice], t, sem.at[into, choice]).start()
        return carry

    lax.fori_loop(0, n_tokens, issue, 0, unroll=unroll)


def _gathered_moe(r_ref, ybuf, sem, slot, n_tokens):
    for choice in range(2):
        _wait_rows(ybuf.at[slot, choice], n_tokens, sem.at[slot, choice])
    return (r_ref[:, REC_W0:REC_W0 + 1] * _load_token_tiles(ybuf.at[slot, 0])
            + r_ref[:, REC_W1:REC_W1 + 1] * _load_token_tiles(ybuf.at[slot, 1]))


def _combine_final_body(dest_ref, x_ref, r_ref, mod_ref, g_ref, yb_hbm, o_ref, ybuf, sem):
    tcm = x_ref.shape[0]
    i = pl.program_id(0)
    slot = i % 2

    @pl.when(i == 0)
    def _():
        _gather_rows(dest_ref, yb_hbm, ybuf, sem, 0, 0, tcm, DMA_UNROLL)

    @pl.when(i + 1 < pl.num_programs(0))
    def _():
        _gather_rows(dest_ref, yb_hbm, ybuf, sem, i + 1, 1 - slot, tcm, DMA_UNROLL)

    xn = x_ref[...] + mod_ref[0, 5:6, :] * _gathered_moe(r_ref, ybuf, sem, slot, tcm)
    ms = jnp.mean(xn * xn, axis=-1, keepdims=True)
    o_ref[...] = (xn * lax.rsqrt(ms + NORM_EPS)) * g_ref[...]


def _combine_final(x, route, mod, g_final, yb, dest, seq_len):
    t, d = x.shape
    tcm = COMBINE_TILE
    per_seq = seq_len // tcm
    return pl.pallas_call(
        _combine_final_body,
        grid_spec=pltpu.PrefetchScalarGridSpec(
            num_scalar_prefetch=1,
            grid=(t // tcm,),
            in_specs=[pl.BlockSpec((tcm, d), lambda i, *_: (i, 0)),
                      pl.BlockSpec((tcm, ROUTE_LANES), lambda i, *_: (i, 0)),
                      pl.BlockSpec((1, N_MOD, d), lambda i, *_: (i // per_seq, 0, 0)),
                      pl.BlockSpec((1, d), lambda i, *_: (0, 0)),
                      pl.BlockSpec(memory_space=pl.ANY)],
            out_specs=pl.BlockSpec((tcm, d), lambda i, *_: (i, 0)),
            scratch_shapes=[pltpu.VMEM((2, 2, tcm * SUBLANES, LANES), F32),
                            pltpu.SemaphoreType.DMA((2, 2))]),
        out_shape=jax.ShapeDtypeStruct((t, d), F32),
        compiler_params=_cparams(("arbitrary",)),
        name="moe_combine_final",
    )(dest, x, route, mod, g_final.reshape(1, d), yb)


def _combine_inproj_body(dest_ref, x_ref, r_ref, modp_ref, mod_ref, g_ref, w_ref, yb_hbm,
                         xo_ref, q_ref, k_ref, v_ref, u_ref, ybuf, sem):
    ts = x_ref.shape[1]
    per_seq = pl.num_programs(1)
    n_tiles = pl.num_programs(0) * per_seq
    i = pl.program_id(0) * per_seq + pl.program_id(1)
    slot = i % 2

    @pl.when(i == 0)
    def _():
        _gather_rows(dest_ref, yb_hbm, ybuf, sem, 0, 0, ts, DMA_UNROLL)

    moe = _gathered_moe(r_ref, ybuf, sem, slot, ts)
    _gather_rows(dest_ref, yb_hbm, ybuf, sem, jnp.minimum(i + 1, n_tiles - 1), 1 - slot, ts, True)
    xn = x_ref[0] + modp_ref[0, 5:6, :] * moe
    xo_ref[0] = xn
    _project_in(xn, mod_ref, g_ref, w_ref, q_ref, k_ref, v_ref, u_ref)

    @pl.when(i == n_tiles - 1)
    def _():
        for choice in range(2):
            _wait_rows(ybuf.at[1 - slot, choice], ts, sem.at[1 - slot, choice])


def _combine_input_projection(x, route, mod_prev, yb, dest, mod, g, w_in_bf, layer):
    b, s, d = x.shape
    din = w_in_bf.shape[2]
    dc = (din - 3 * D_ATTN) // 2
    ts = SEQ_TILE
    per_seq = s // ts
    tile = lambda n: pl.BlockSpec((1, ts, n), lambda i, j, *_: (i, j, 0))
    mods = pl.BlockSpec((1, N_MOD, d), lambda i, j, *_: (i, 0, 0))
    return pl.pallas_call(
        _combine_inproj_body,
        grid_spec=pltpu.PrefetchScalarGridSpec(
            num_scalar_prefetch=1,
            grid=(b, per_seq),
            in_specs=[tile(d),
                      pl.BlockSpec((ts, ROUTE_LANES), lambda i, j, *_: (i * per_seq + j, 0)),
                      mods, mods,
                      pl.BlockSpec((1, d), lambda i, j, *_: (0, 0)),
                      pl.BlockSpec((1, d, din), lambda i, j, *_: (layer, 0, 0)),
                      pl.BlockSpec(memory_space=pl.ANY)],
            out_specs=[tile(d), tile(D_ATTN), tile(D_ATTN), tile(D_ATTN), tile(dc)],
            scratch_shapes=[pltpu.VMEM((2, 2, ts * SUBLANES, LANES), F32),
                            pltpu.SemaphoreType.DMA((2, 2))]),
        out_shape=[jax.ShapeDtypeStruct((b, s, d), F32)]
        + [jax.ShapeDtypeStruct((b, s, D_ATTN), BF16)] * 3
        + [jax.ShapeDtypeStruct((b, s, dc), BF16)],
        compiler_params=_cparams(("arbitrary", "arbitrary")),
        name="combine_input_projection",
    )(dest, x, route, mod_prev, mod, g.reshape(1, d), w_in_bf, yb)


def _plan_blocks(counts, n_blocks):
    tm = MOE_BLOCK
    experts = jnp.arange(N_EXPERTS, dtype=jnp.int32)
    padded = ((counts + tm - 1) // tm) * tm
    pad_end = jnp.sum(jnp.where(experts[:, None] <= experts[None, :], padded[:, None], 0), axis=0)
    pad_start = pad_end - padded
    blk_row = jnp.arange(n_blocks, dtype=jnp.int32)[:, None] * tm
    inside = (pad_start[None, :] <= blk_row) & (blk_row < pad_end[None, :])
    blk_eid = jnp.where(jnp.any(inside, axis=1),
                        jnp.sum(jnp.where(inside, experts[None, :], 0), axis=1), N_EXPERTS - 1)
    rows_left = jnp.clip((pad_start + counts)[None, :] - blk_row, 0, tm)
    blk_valid = jnp.sum(jnp.where(inside, rows_left, 0), axis=1)
    n_used = jnp.sum(padded, keepdims=True) // tm
    i32 = lambda a: a.astype(jnp.int32)
    return i32(pad_start), i32(blk_eid), i32(blk_valid), i32(n_used)


def kernel(x, c, w_mod, b_mod, g_norm1, w_in, conv_w, conv_b, conv_ln_g, conv_ln_b, w_out, g_norm2, w_router_group, b_router_group, w_router_expert, b_router_expert, w_exp_gate, w_exp_up, w_exp_down, g_final):
    b, s, d = x.shape
    depth = w_mod.shape[0]
    t = b * s
    assert d == SUBLANES * LANES
    assert s % (ATTN_BLOCK * max(DILATIONS)) == 0 and s % SEQ_TILE == 0
    assert s % COMBINE_TILE == 0
    n_blocks = -(-2 * t // MOE_BLOCK) + N_EXPERTS
    n_rows = n_blocks * MOE_BLOCK

    mod_all = _modulation(c, w_mod, b_mod).reshape(depth, b, N_MOD, d)
    bias = _attn_bias_table()
    pad = ROUTE_LANES - N_GROUPS - N_EXPERTS
    w_router = jnp.concatenate(
        [w_router_group, w_router_expert, jnp.zeros((depth, d, pad), F32)], axis=-1)
    b_router = jnp.concatenate(
        [b_router_group, b_router_expert, jnp.zeros((depth, pad), F32)], axis=-1)
    b_router = b_router.reshape(depth, 1, ROUTE_LANES)
    wr_hi = w_router.astype(BF16)
    wr_lo = (w_router - wr_hi.astype(F32)).astype(BF16)
    w_in_bf = w_in.astype(BF16)
    w_out_bf = w_out.astype(BF16)

    pending = None
    for l in range(depth):
        mod = mod_all[l]
        if pending is None:
            q, k, v, u = _input_projection(x, mod, g_norm1[l], w_in_bf, l)
        else:
            x, q, k, v, u = _combine_input_projection(x, *pending, mod, g_norm1[l], w_in_bf, l)
        attn = _dilated_attention(q, k, v, bias)
        conv = _conformer_conv(u, conv_w[l], conv_b[l], conv_ln_g[l], conv_ln_b[l])
        x, h2, route, sel, seen = _output_projection(attn, conv, x, mod, g_norm2[l], w_out_bf,
                                                     wr_hi, wr_lo, b_router, l)
        route = route.reshape(t, ROUTE_LANES)
        counts = seen[0, N_GROUPS:N_GROUPS + N_EXPERTS].astype(jnp.int32)
        starts, blk_eid, blk_valid, n_used = _plan_blocks(counts, n_blocks)
        eid = sel[REC_LANE0:REC_LANE1 + 1].astype(jnp.int32) - N_GROUPS
        ranks = sel[REC_RANK0:REC_RANK1 + 1].astype(jnp.int32)
        onehot = eid[None] == jnp.arange(N_EXPERTS, dtype=jnp.int32)[:, None, None]
        dest = (jnp.sum(jnp.where(onehot, starts[:, None, None], 0), axis=0) + ranks).reshape(-1)
        xb = _dispatch(h2, dest, n_rows)
        yb = _experts(xb, blk_eid, blk_valid, n_used, w_exp_gate, w_exp_up, w_exp_down, l)
        pending = (route, mod, yb, dest)
    route, mod, yb, dest = pending
    return _combine_final(x.reshape(t, d), route, mod, g_final, yb, dest, s).reshape(b, s, d)
```

```python
import functools

import numpy as np
import jax
import jax.numpy as jnp
from jax import lax
from jax.experimental import pallas as pl
from jax.experimental.pallas import tpu as pltpu

F32 = jnp.float32
BF16 = jnp.bfloat16

SUBLANES = 8
LANES = 128
HEAD_DIM = 64
N_HEADS = 8
D_ATTN = N_HEADS * HEAD_DIM
CONV_WIDTH = 31
DILATIONS = (1, 4, 16)
ATTN_BLOCK = 128
ATTN_UNROLL = 15
DEINT = 4
N_GROUPS = 4
EXPERTS_PER_GROUP = 8
N_EXPERTS = N_GROUPS * EXPERTS_PER_GROUP
NORM_EPS = 1e-6
N_MOD = 6
MASK_VALUE = -1e30
ROUTE_LANES = 128
REC_LANE0, REC_LANE1, REC_W0, REC_W1, REC_RANK0, REC_RANK1 = range(6)

SEQ_TILE = 512
CONV_HALO = 32
CONV_ROWS = 64
MOE_BLOCK = 512
DISPATCH_TILE = 4096
COMBINE_TILE = 512
DMA_UNROLL = 8
VMEM_LIMIT = 56 * 1024 * 1024


def _cparams(sem):
    return pltpu.CompilerParams(dimension_semantics=sem, vmem_limit_bytes=VMEM_LIMIT)


def _mod_body(c_ref, w_ref, b_ref, o_ref):
    c = c_ref[...]
    c_act = c * jax.nn.sigmoid(c)
    o_ref[0] = jnp.dot(c_act, w_ref[0], preferred_element_type=F32,
                       precision=lax.Precision.HIGHEST) + b_ref[0]


def _modulation(c, w_mod, b_mod):
    depth, d, dm = w_mod.shape
    b = c.shape[0]
    nt = dm // d
    return pl.pallas_call(
        _mod_body,
        grid=(depth, nt),
        in_specs=[pl.BlockSpec((b, d), lambda l, j: (0, 0)),
                  pl.BlockSpec((1, d, d), lambda l, j: (l, 0, j)),
                  pl.BlockSpec((1, 1, d), lambda l, j: (l, 0, j))],
        out_specs=pl.BlockSpec((1, b, d), lambda l, j: (l, 0, j)),
        out_shape=jax.ShapeDtypeStruct((depth, b, dm), F32),
        compiler_params=_cparams(("arbitrary", "arbitrary")),
        name="modulation",
    )(c, w_mod, b_mod.reshape(depth, 1, dm))


def _rms_modulate(x, g, shift, scale):
    ms = jnp.mean(x * x, axis=-1, keepdims=True)
    return (x * lax.rsqrt(ms + NORM_EPS)) * g * (1.0 + scale) + shift


def _project_in(x, mod_ref, g_ref, w_ref, q_ref, k_ref, v_ref, u_ref):
    h = _rms_modulate(x, g_ref[...], mod_ref[0, 0:1, :], mod_ref[0, 1:2, :])
    y = jnp.dot(h.astype(BF16), w_ref[0], preferred_element_type=F32)
    da = D_ATTN
    q_ref[0] = (y[:, 0:da] * (HEAD_DIM ** -0.5)).astype(q_ref.dtype)
    k_ref[0] = y[:, da:2 * da].astype(k_ref.dtype)
    v_ref[0] = y[:, 2 * da:3 * da].astype(v_ref.dtype)
    dc = (y.shape[1] - 3 * da) // 2
    a = y[:, 3 * da:3 * da + dc]
    b = y[:, 3 * da + dc:]
    u_ref[0] = (a * jax.nn.sigmoid(b)).astype(u_ref.dtype)


def _inproj_body(x_ref, mod_ref, g_ref, w_ref, q_ref, k_ref, v_ref, u_ref):
    _project_in(x_ref[0], mod_ref, g_ref, w_ref, q_ref, k_ref, v_ref, u_ref)


def _input_projection(x, mod, g, w_in_bf, layer):
    b, s, d = x.shape
    din = w_in_bf.shape[2]
    dc = (din - 3 * D_ATTN) // 2
    ts = SEQ_TILE
    tile = lambda n: pl.BlockSpec((1, ts, n), lambda i, j: (i, j, 0))
    return pl.pallas_call(
        _inproj_body,
        grid=(b, s // ts),
        in_specs=[tile(d),
                  pl.BlockSpec((1, N_MOD, d), lambda i, j: (i, 0, 0)),
                  pl.BlockSpec((1, d), lambda i, j: (0, 0)),
                  pl.BlockSpec((1, d, din), lambda i, j: (layer, 0, 0))],
        out_specs=[tile(D_ATTN), tile(D_ATTN), tile(D_ATTN), tile(dc)],
        out_shape=[jax.ShapeDtypeStruct((b, s, D_ATTN), BF16)] * 3
        + [jax.ShapeDtypeStruct((b, s, dc), BF16)],
        compiler_params=_cparams(("arbitrary", "arbitrary")),
        name="input_projection",
    )(x, mod, g.reshape(1, d), w_in_bf)


def _attn_bias_table():
    blk = ATTN_BLOCK
    slopes = jnp.asarray(2.0 ** (-8.0 * np.arange(1, N_HEADS + 1) / N_HEADS), dtype=F32)
    qi = np.arange(blk)[:, None] + blk
    ki = np.arange(2 * blk)[None, :]
    delta = qi - ki
    in_band = (delta >= 0) & (delta <= blk)
    tables = []
    for dil in DILATIONS:
        bias = -slopes[:, None, None] * jnp.asarray(delta * dil, dtype=F32)
        later = jnp.where(in_band[None], bias, MASK_VALUE)
        first = jnp.where((in_band & (ki >= blk))[None], bias, MASK_VALUE)
        tables.append(jnp.stack([later, first], axis=1))
    t = jnp.stack(tables, axis=1)
    return t.reshape(N_HEADS // 2, 2, len(DILATIONS), 2, blk, 2 * blk)


def _attn_body(q_ref, k_ref, v_ref, bias_ref, o_ref, xf, x4, acc_o, acc_l, acc_m, s_buf, p_buf):
    blk = ATTN_BLOCK
    pw = 2 * HEAD_DIM
    s_len = q_ref.shape[1]
    seg = s_len // DEINT
    left = lax.broadcasted_iota(jnp.int32, (blk, pw), 1) < HEAD_DIM
    trans_b = (((1,), (1,)), ((), ()))
    n_branch = len(DILATIONS)
    n_blocks = s_len // blk
    natural = (q_ref.at[0], k_ref.at[0], v_ref.at[0])
    deint = (x4.at[0], x4.at[1], x4.at[2])

    for t, ref in enumerate(natural):
        xf[...] = ref[...].astype(F32)
        for r in range(DEINT):
            x4[t, r * seg:(r + 1) * seg, :] = xf[pl.ds(r, seg, stride=DEINT), :]

    for br, dil in enumerate(DILATIONS):
        nb = s_len // (blk * dil)
        step = dil // DEINT if dil > DEINT else 1
        srcs = natural if dil == 1 else deint

        def starts(j, dil=dil, nb=nb, step=step):
            r = j // nb
            n = j % nb
            if dil == 1:
                base = 0
            else:
                base = (r % DEINT) * seg + r // DEINT
            qs = base + n * (blk * step)
            ks = base + jnp.maximum(n - 1, 0) * (blk * step)
            return qs, ks, jnp.where(n == 0, 1, 0)

        def span(start, step=step):
            return pl.ds(start, blk) if step == 1 else pl.ds(start, blk, stride=step)

        def both(ref, qs, ks):
            return jnp.concatenate([ref[span(ks), :], ref[span(qs), :]], axis=0).astype(BF16)

        def scores(j, slot, br=br, srcs=srcs):
            qs, ks, first = starts(j)
            q = srcs[0][span(qs), :].astype(BF16)
            kk = both(srcs[1], qs, ks)
            zero = jnp.zeros_like(q)
            q2 = jnp.concatenate([jnp.where(left, q, zero), jnp.where(left, zero, q)], axis=0)
            s = lax.dot_general(q2, kk, trans_b, preferred_element_type=F32)
            s_buf[slot] = s.reshape(2, blk, 2 * blk) + bias_ref[0, :, br, first]

        def softmax(j, slot, br=br):
            qs, _, _ = starts(j)
            maxes = []
            for hh in range(2):
                s = s_buf[slot, hh]
                m = jnp.max(s, axis=-1, keepdims=True)
                p_buf[slot, hh] = jnp.exp(s - m).astype(BF16)
                maxes.append(m)
            acc_m[br, span(qs), :] = jnp.where(left, maxes[0], maxes[1])

        def values(j, slot, br=br, srcs=srcs):
            qs, ks, _ = starts(j)
            vv = both(srcs[2], qs, ks)
            rhs = jnp.concatenate([vv, jnp.ones_like(vv)], axis=1)
            o = jnp.dot(p_buf[slot].reshape(2 * blk, 2 * blk), rhs, preferred_element_type=F32)
            acc_o[br, span(qs), :] = jnp.where(left, o[0:blk, 0:pw], o[blk:, 0:pw])
            acc_l[br, span(qs), :] = jnp.where(left, o[0:blk, pw:], o[blk:, pw:])

        nslot = ATTN_UNROLL
        scores(0, 0)
        softmax(0, 0)
        scores(1, 1)

        def steady(i, carry):
            for u in range(ATTN_UNROLL):
                j = 2 + ATTN_UNROLL * i + u
                values(j - 2, u % nslot)
                softmax(j - 1, (u + 1) % nslot)
                scores(j, (u + 2) % nslot)
            return carry

        trips = (n_blocks - 2) // ATTN_UNROLL
        lax.fori_loop(0, trips, steady, 0)
        for j in range(2 + ATTN_UNROLL * trips, n_blocks):
            values(j - 2, (j - 2) % nslot)
            softmax(j - 1, (j - 1) % nslot)
            scores(j, j % nslot)
        values(n_blocks - 2, (n_blocks - 2) % nslot)
        softmax(n_blocks - 1, (n_blocks - 1) % nslot)
        values(n_blocks - 1, (n_blocks - 1) % nslot)

    rows = 256

    def merge(c, carry):
        idx = pl.multiple_of(c * rows, rows)
        r = idx // seg
        nat = pl.ds(DEINT * (idx - r * seg) + r, rows, stride=DEINT)
        spans = [nat if dil == 1 else pl.ds(idx, rows) for dil in DILATIONS]
        ms = [acc_m[br, spans[br], :] for br in range(n_branch)]
        m_max = functools.reduce(jnp.maximum, ms)
        num = 0.0
        den = 0.0
        for br in range(n_branch):
            a = jnp.exp(ms[br] - m_max)
            num = num + a * acc_o[br, spans[br], :]
            den = den + a * acc_l[br, spans[br], :]
        o_ref[0, nat, :] = num / den
        return carry

    lax.fori_loop(0, s_len // rows, merge, 0, unroll=2)


def _dilated_attention(q, k, v, bias):
    b, s, da = q.shape
    pw = 2 * HEAD_DIM
    assert DILATIONS == (1, DEINT, DEINT * DEINT)
    seq = pl.BlockSpec((1, s, pw), lambda i, j: (i, 0, j))
    nbr = len(DILATIONS)
    stage = (ATTN_UNROLL, 2, ATTN_BLOCK, 2 * ATTN_BLOCK)
    return pl.pallas_call(
        _attn_body,
        grid=(b, da // pw),
        in_specs=[seq, seq, seq,
                  pl.BlockSpec((1,) + bias.shape[1:], lambda i, j: (j, 0, 0, 0, 0, 0))],
        out_specs=seq,
        out_shape=jax.ShapeDtypeStruct((b, s, da), F32),
        scratch_shapes=[pltpu.VMEM((s, pw), F32), pltpu.VMEM((3, s, pw), F32)]
        + [pltpu.VMEM((nbr, s, pw), F32)] * 3
        + [pltpu.VMEM(stage, F32), pltpu.VMEM(stage, BF16)],
        compiler_params=_cparams(("arbitrary", "arbitrary")),
        name="dilated_attention",
    )(q, k, v, bias)


def _conv_body(uc_ref, uh_ref, w_ref, cb_ref, lg_ref, lb_ref, o_ref, win, shifted):
    t = pl.program_id(1)
    tc = uc_ref.shape[1]
    win[0:CONV_HALO, :] = jnp.where(t > 0, uh_ref[0].astype(F32), 0.0)
    win[CONV_HALO:, :] = uc_ref[0].astype(F32)
    off = CONV_HALO - (CONV_WIDTH - 1)
    for b in range(SUBLANES):
        n_rows = tc + SUBLANES * ((CONV_WIDTH - 1 - b) // SUBLANES)
        shifted[b, 0:n_rows, :] = win[off + b:off + b + n_rows, :]
    for rb in range(tc // CONV_ROWS):
        r0 = rb * CONV_ROWS
        acc = jnp.zeros((CONV_ROWS, uc_ref.shape[2]), F32)
        for j in range(CONV_WIDTH):
            a, b = divmod(j, SUBLANES)
            rows = pl.ds(r0 + SUBLANES * a, CONV_ROWS)
            acc = acc + w_ref[j:j + 1, :] * shifted[b, rows, :]
        y = acc + cb_ref[...]
        mu = jnp.mean(y, axis=-1, keepdims=True)
        yc = y - mu
        var = jnp.mean(yc * yc, axis=-1, keepdims=True)
        z = yc * lax.rsqrt(var + NORM_EPS) * lg_ref[...] + lb_ref[...]
        o_ref[0, r0:r0 + CONV_ROWS, :] = (z * jax.nn.sigmoid(z)).astype(o_ref.dtype)


def _conformer_conv(u, conv_w, conv_b, ln_g, ln_b):
    b, s, dc = u.shape
    tc = SEQ_TILE
    per = tc // CONV_HALO
    row = pl.BlockSpec((1, dc), lambda i, j: (0, 0))
    return pl.pallas_call(
        _conv_body,
        grid=(b, s // tc),
        in_specs=[pl.BlockSpec((1, tc, dc), lambda i, j: (i, j, 0)),
                  pl.BlockSpec((1, CONV_HALO, dc), lambda i, j: (i, jnp.maximum(j * per - 1, 0), 0)),
                  pl.BlockSpec((CONV_WIDTH, dc), lambda i, j: (0, 0)),
                  row, row, row],
        out_specs=pl.BlockSpec((1, tc, dc), lambda i, j: (i, j, 0)),
        out_shape=jax.ShapeDtypeStruct((b, s, dc), BF16),
        scratch_shapes=[pltpu.VMEM((tc + CONV_HALO, dc), F32),
                        pltpu.VMEM((SUBLANES, tc + CONV_HALO - SUBLANES, dc), F32)],
        compiler_params=_cparams(("arbitrary", "arbitrary")),
        name="conformer_conv",
    )(u, u, conv_w, conv_b.reshape(1, dc), ln_g.reshape(1, dc), ln_b.reshape(1, dc))


def _route(logits, seen):
    rows = logits.shape[0]
    lane = lax.broadcasted_iota(jnp.int32, logits.shape, 1)
    big = jnp.int32(1 << 20)
    is_g = lane < N_GROUPS
    gl = jnp.where(is_g, logits, MASK_VALUE)
    gmax = jnp.max(gl, axis=-1, keepdims=True)
    grp = jnp.min(jnp.where(gl == gmax, lane, big), axis=-1, keepdims=True)
    gsum = jnp.sum(jnp.where(is_g, jnp.exp(gl - gmax), 0.0), axis=-1, keepdims=True)
    p_grp = 1.0 / gsum
    lo = N_GROUPS + grp * EXPERTS_PER_GROUP
    el = jnp.where((lane >= lo) & (lane < lo + EXPERTS_PER_GROUP), logits, MASK_VALUE)
    v1 = jnp.max(el, axis=-1, keepdims=True)
    i1 = jnp.min(jnp.where(el == v1, lane, big), axis=-1, keepdims=True)
    el2 = jnp.where(lane == i1, MASK_VALUE, el)
    v2 = jnp.max(el2, axis=-1, keepdims=True)
    i2 = jnp.min(jnp.where(el2 == v2, lane, big), axis=-1, keepdims=True)
    e = jnp.exp(v2 - v1)
    w1 = p_grp / (1.0 + e)
    w2 = p_grp * e / (1.0 + e)

    hit1 = lane == i1
    hit2 = lane == i2
    chosen = jnp.where(hit1 | hit2, 1.0, 0.0)
    ri = lax.broadcasted_iota(jnp.int32, (rows, rows), 0)
    ci = lax.broadcasted_iota(jnp.int32, (rows, rows), 1)
    earlier = jnp.where(ri > ci, 1.0, 0.0).astype(BF16)
    before = jnp.dot(earlier, chosen.astype(BF16), preferred_element_type=F32) + seen
    rank1 = jnp.sum(jnp.where(hit1, before, 0.0), axis=-1, keepdims=True)
    rank2 = jnp.sum(jnp.where(hit2, before, 0.0), axis=-1, keepdims=True)
    seen = seen + jnp.sum(chosen, axis=0, keepdims=True)

    rec = jnp.zeros(logits.shape, F32)
    for pos, val in ((REC_LANE0, i1.astype(F32)), (REC_LANE1, i2.astype(F32)), (REC_W0, w1),
                     (REC_W1, w2), (REC_RANK0, rank1), (REC_RANK1, rank2)):
        rec = jnp.where(lane == pos, val, rec)
    return rec, seen


def _split_bf16(x):
    hi = x.astype(BF16)
    lo = (x - hi.astype(F32)).astype(BF16)
    return hi, lo


def _outproj_body(a_ref, c_ref, x_ref, mod_ref, g_ref, w_ref, wrh_ref, wrl_ref, br_ref,
                  xo_ref, h_ref, r_ref, sel_ref, cnt_ref, seen):
    @pl.when((pl.program_id(0) == 0) & (pl.program_id(1) == 0))
    def _():
        seen[...] = jnp.zeros_like(seen)

    da = a_ref.shape[2]
    mix = jnp.dot(a_ref[0].astype(BF16), w_ref[0, 0:da, :], preferred_element_type=F32)
    mix = mix + jnp.dot(c_ref[0], w_ref[0, da:, :], preferred_element_type=F32)
    xn = x_ref[0] + mod_ref[0, 2:3, :] * mix
    xo_ref[0] = xn
    h = _rms_modulate(xn, g_ref[...], mod_ref[0, 3:4, :], mod_ref[0, 4:5, :])
    _store_token_tiles(h_ref, h)
    h_hi, h_lo = _split_bf16(h)
    logits = (jnp.dot(h_hi, wrh_ref[0], preferred_element_type=F32)
              + jnp.dot(h_lo, wrh_ref[0], preferred_element_type=F32)
              + jnp.dot(h_hi, wrl_ref[0], preferred_element_type=F32)) + br_ref[0]
    rec, new_seen = _route(logits, seen[...])
    r_ref[0] = rec
    sel_ref[...] = rec.T[0:SUBLANES, :]
    seen[...] = new_seen
    cnt_ref[...] = new_seen


def _output_projection(attn, conv, x, mod, g, w_out_bf, wr_hi, wr_lo, b_router, layer):
    b, s, d = x.shape
    ts = SEQ_TILE
    tile = lambda n: pl.BlockSpec((1, ts, n), lambda i, j: (i, j, 0))
    const = lambda r, c: pl.BlockSpec((r, c), lambda i, j: (0, 0))
    per_layer = lambda r, c: pl.BlockSpec((1, r, c), lambda i, j: (layer, 0, 0))
    return pl.pallas_call(
        _outproj_body,
        grid=(b, s // ts),
        in_specs=[tile(attn.shape[2]), tile(conv.shape[2]), tile(d),
                  pl.BlockSpec((1, N_MOD, d), lambda i, j: (i, 0, 0)),
                  const(1, d), per_layer(d, d), per_layer(d, ROUTE_LANES),
                  per_layer(d, ROUTE_LANES), per_layer(1, ROUTE_LANES)],
        out_specs=[tile(d), pl.BlockSpec((ts * SUBLANES, LANES), lambda i, j: (i * (s // ts) + j, 0)),
                   tile(ROUTE_LANES),
                   pl.BlockSpec((SUBLANES, ts), lambda i, j: (0, i * (s // ts) + j)),
                   const(1, ROUTE_LANES)],
        out_shape=[jax.ShapeDtypeStruct((b, s, d), F32),
                   jax.ShapeDtypeStruct((b * s * SUBLANES, LANES), F32),
                   jax.ShapeDtypeStruct((b, s, ROUTE_LANES), F32),
                   jax.ShapeDtypeStruct((SUBLANES, b * s), F32),
                   jax.ShapeDtypeStruct((1, ROUTE_LANES), F32)],
        scratch_shapes=[pltpu.VMEM((1, ROUTE_LANES), F32)],
        compiler_params=_cparams(("arbitrary", "arbitrary")),
        name="output_projection",
    )(attn, conv, x, mod, g.reshape(1, d), w_out_bf, wr_hi, wr_lo, b_router)


def _load_token_tiles(ref):
    n = ref.shape[0] // SUBLANES
    return jnp.concatenate([ref[pl.ds(a, n, stride=SUBLANES), :] for a in range(SUBLANES)], axis=1)


def _store_token_tiles(ref, val):
    n = val.shape[0]
    for a in range(SUBLANES):
        ref[pl.ds(a, n, stride=SUBLANES), :] = val[:, a * LANES:(a + 1) * LANES]


def _tile_of(ref, token):
    return ref.at[pl.ds(pl.multiple_of(token * SUBLANES, SUBLANES), SUBLANES)]


def _row_copy(src, src_token, dst, dst_token, sem):
    return pltpu.make_async_copy(_tile_of(src, src_token), _tile_of(dst, dst_token), sem)


def _wait_rows(like, n_tokens, sem):
    n = n_tokens * SUBLANES
    pltpu.make_async_copy(like.at[pl.ds(0, n)], like.at[pl.ds(0, n)], sem).wait()


def _dispatch_body(dest_ref, h_ref, xb_hbm, sem):
    td = h_ref.shape[0] // SUBLANES
    n_tok = dest_ref.shape[0] // 2
    base = pl.program_id(0) * td

    def issue(t, carry):
        for choice in range(2):
            _row_copy(h_ref, t, xb_hbm, dest_ref[choice * n_tok + base + t], sem).start(
                priority=choice)
        return carry

    lax.fori_loop(0, td, issue, 0, unroll=DMA_UNROLL)
    _wait_rows(xb_hbm, 2 * td, sem)


def _dispatch(h, dest, n_rows):
    t = h.shape[0] // SUBLANES
    td = min(DISPATCH_TILE, t)
    assert t % td == 0
    return pl.pallas_call(
        _dispatch_body,
        grid_spec=pltpu.PrefetchScalarGridSpec(
            num_scalar_prefetch=1,
            grid=(t // td,),
            in_specs=[pl.BlockSpec((td * SUBLANES, LANES), lambda i, *_: (i, 0))],
            out_specs=pl.BlockSpec(memory_space=pl.ANY),
            scratch_shapes=[pltpu.SemaphoreType.DMA(())]),
        out_shape=jax.ShapeDtypeStruct((n_rows * SUBLANES, LANES), h.dtype),
        compiler_params=_cparams(("arbitrary",)),
        name="moe_dispatch",
    )(dest, h)


def _expert_body(eid_ref, valid_ref, nused_ref, x_ref, wg_ref, wu_ref, wd_ref, o_ref,
                 wg_bf, wu_bf, wd_bf):
    i = pl.program_id(0)

    @pl.when(i < nused_ref[0])
    def _():
        changed = (i == 0) | (eid_ref[i] != eid_ref[jnp.maximum(i - 1, 0)])

        @pl.when(changed)
        def _():
            wg_bf[...] = wg_ref[0, 0].astype(BF16)
            wu_bf[...] = wu_ref[0, 0].astype(BF16)
            wd_bf[...] = wd_ref[0, 0].astype(BF16)

        x = _load_token_tiles(x_ref)
        row = lax.broadcasted_iota(jnp.int32, x.shape, 0)
        x = jnp.where(row < valid_ref[i], x, 0.0).astype(BF16)
        g = jnp.dot(x, wg_bf[...], preferred_element_type=F32)
        u = jnp.dot(x, wu_bf[...], preferred_element_type=F32)
        h = (g * jax.nn.sigmoid(g)) * u
        _store_token_tiles(o_ref, jnp.dot(h.astype(BF16), wd_bf[...], preferred_element_type=F32))

    @pl.when(i >= nused_ref[0])
    def _():
        o_ref[...] = jnp.zeros_like(o_ref)


def _experts(xb, blk_eid, blk_valid, n_used, w_gate, w_up, w_down, layer):
    p = xb.shape[0] // SUBLANES
    d, de = w_gate.shape[2:]
    tm = MOE_BLOCK
    last = lambda i, nu: jnp.minimum(i, nu[0] - 1)
    rows = pl.BlockSpec((tm * SUBLANES, LANES), lambda i, eid, nv, nu: (last(i, nu), 0))
    weight = lambda r, c: pl.BlockSpec(
        (1, 1, r, c), lambda i, eid, nv, nu: (layer, eid[last(i, nu)], 0, 0))
    return pl.pallas_call(
        _expert_body,
        grid_spec=pltpu.PrefetchScalarGridSpec(
            num_scalar_prefetch=3,
            grid=(p // tm,),
            in_specs=[rows, weight(d, de), weight(d, de), weight(de, d)],
            out_specs=pl.BlockSpec((tm * SUBLANES, LANES), lambda i, eid, nv, nu: (i, 0)),
            scratch_shapes=[pltpu.VMEM((d, de), BF16), pltpu.VMEM((d, de), BF16),
                            pltpu.VMEM((de, d), BF16)]),
        out_shape=jax.ShapeDtypeStruct((p * SUBLANES, LANES), F32),
        compiler_params=_cparams(("arbitrary",)),
        name="moe_experts",
    )(blk_eid, blk_valid, n_used, xb, w_gate, w_up, w_down)


def _gather_rows(dest_ref, yb_hbm, ybuf, sem, tile, into, n_tokens, unroll):
    n_tok = dest_ref.shape[0] // 2

    def issue(t, carry):
        for choice in range(2):
            src = dest_ref[choice * n_tok + tile * n_tokens + t]
            _row_copy(yb_hbm, src, ybuf.at[into, choice], t, sem.at[into, choice]).start(
                priority=choice)
        return carry

    lax.fori_loop(0, n_tokens, issue, 0, unroll=unroll)


def _gathered_moe(r_ref, ybuf, sem, slot, n_tokens):
    for choice in range(2):
        _wait_rows(ybuf.at[slot, choice], n_tokens, sem.at[slot, choice])
    return (r_ref[:, REC_W0:REC_W0 + 1] * _load_token_tiles(ybuf.at[slot, 0])
            + r_ref[:, REC_W1:REC_W1 + 1] * _load_token_tiles(ybuf.at[slot, 1]))


def _combine_final_body(dest_ref, x_ref, r_ref, mod_ref, g_ref, yb_hbm, o_ref, ybuf, sem):
    tcm = x_ref.shape[0]
    i = pl.program_id(0)
    slot = i % 2

    @pl.when(i == 0)
    def _():
        _gather_rows(dest_ref, yb_hbm, ybuf, sem, 0, 0, tcm, DMA_UNROLL)

    @pl.when(i + 1 < pl.num_programs(0))
    def _():
        _gather_rows(dest_ref, yb_hbm, ybuf, sem, i + 1, 1 - slot, tcm, DMA_UNROLL)

    xn = x_ref[...] + mod_ref[0, 5:6, :] * _gathered_moe(r_ref, ybuf, sem, slot, tcm)
    ms = jnp.mean(xn * xn, axis=-1, keepdims=True)
    o_ref[...] = (xn * lax.rsqrt(ms + NORM_EPS)) * g_ref[...]


def _combine_final(x, route, mod, g_final, yb, dest, seq_len):
    t, d = x.shape
    tcm = COMBINE_TILE
    per_seq = seq_len // tcm
    return pl.pallas_call(
        _combine_final_body,
        grid_spec=pltpu.PrefetchScalarGridSpec(
            num_scalar_prefetch=1,
            grid=(t // tcm,),
            in_specs=[pl.BlockSpec((tcm, d), lambda i, *_: (i, 0)),
                      pl.BlockSpec((tcm, ROUTE_LANES), lambda i, *_: (i, 0)),
                      pl.BlockSpec((1, N_MOD, d), lambda i, *_: (i // per_seq, 0, 0)),
                      pl.BlockSpec((1, d), lambda i, *_: (0, 0)),
                      pl.BlockSpec(memory_space=pl.ANY)],
            out_specs=pl.BlockSpec((tcm, d), lambda i, *_: (i, 0)),
            scratch_shapes=[pltpu.VMEM((2, 2, tcm * SUBLANES, LANES), F32),
                            pltpu.SemaphoreType.DMA((2, 2))]),
        out_shape=jax.ShapeDtypeStruct((t, d), F32),
        compiler_params=_cparams(("arbitrary",)),
        name="moe_combine_final",
    )(dest, x, route, mod, g_final.reshape(1, d), yb)


def _combine_inproj_body(dest_ref, x_ref, r_ref, modp_ref, mod_ref, g_ref, w_ref, yb_hbm,
                         xo_ref, q_ref, k_ref, v_ref, u_ref, ybuf, sem):
    ts = x_ref.shape[1]
    per_seq = pl.num_programs(1)
    n_tiles = pl.num_programs(0) * per_seq
    i = pl.program_id(0) * per_seq + pl.program_id(1)
    slot = i % 2

    @pl.when(i == 0)
    def _():
        _gather_rows(dest_ref, yb_hbm, ybuf, sem, 0, 0, ts, DMA_UNROLL)

    moe = _gathered_moe(r_ref, ybuf, sem, slot, ts)
    _gather_rows(dest_ref, yb_hbm, ybuf, sem, jnp.minimum(i + 1, n_tiles - 1), 1 - slot, ts, True)
    xn = x_ref[0] + modp_ref[0, 5:6, :] * moe
    xo_ref[0] = xn
    _project_in(xn, mod_ref, g_ref, w_ref, q_ref, k_ref, v_ref, u_ref)

    @pl.when(i == n_tiles - 1)
    def _():
        for choice in range(2):
            _wait_rows(ybuf.at[1 - slot, choice], ts, sem.at[1 - slot, choice])


def _combine_input_projection(x, route, mod_prev, yb, dest, mod, g, w_in_bf, layer):
    b, s, d = x.shape
    din = w_in_bf.shape[2]
    dc = (din - 3 * D_ATTN) // 2
    ts = SEQ_TILE
    per_seq = s // ts
    tile = lambda n: pl.BlockSpec((1, ts, n), lambda i, j, *_: (i, j, 0))
    mods = pl.BlockSpec((1, N_MOD, d), lambda i, j, *_: (i, 0, 0))
    return pl.pallas_call(
        _combine_inproj_body,
        grid_spec=pltpu.PrefetchScalarGridSpec(
            num_scalar_prefetch=1,
            grid=(b, per_seq),
            in_specs=[tile(d),
                      pl.BlockSpec((ts, ROUTE_LANES), lambda i, j, *_: (i * per_seq + j, 0)),
                      mods, mods,
                      pl.BlockSpec((1, d), lambda i, j, *_: (0, 0)),
                      pl.BlockSpec((1, d, din), lambda i, j, *_: (layer, 0, 0)),
                      pl.BlockSpec(memory_space=pl.ANY)],
            out_specs=[tile(d), tile(D_ATTN), tile(D_ATTN), tile(D_ATTN), tile(dc)],
            scratch_shapes=[pltpu.VMEM((2, 2, ts * SUBLANES, LANES), F32),
                            pltpu.SemaphoreType.DMA((2, 2))]),
        out_shape=[jax.ShapeDtypeStruct((b, s, d), F32)]
        + [jax.ShapeDtypeStruct((b, s, D_ATTN), BF16)] * 3
        + [jax.ShapeDtypeStruct((b, s, dc), BF16)],
        compiler_params=_cparams(("arbitrary", "arbitrary")),
        name="combine_input_projection",
    )(dest, x, route, mod_prev, mod, g.reshape(1, d), w_in_bf, yb)


def _plan_blocks(counts, n_blocks):
    tm = MOE_BLOCK
    experts = jnp.arange(N_EXPERTS, dtype=jnp.int32)
    padded = ((counts + tm - 1) // tm) * tm
    pad_end = jnp.sum(jnp.where(experts[:, None] <= experts[None, :], padded[:, None], 0), axis=0)
    pad_start = pad_end - padded
    blk_row = jnp.arange(n_blocks, dtype=jnp.int32)[:, None] * tm
    inside = (pad_start[None, :] <= blk_row) & (blk_row < pad_end[None, :])
    blk_eid = jnp.where(jnp.any(inside, axis=1),
                        jnp.sum(jnp.where(inside, experts[None, :], 0), axis=1), N_EXPERTS - 1)
    rows_left = jnp.clip((pad_start + counts)[None, :] - blk_row, 0, tm)
    blk_valid = jnp.sum(jnp.where(inside, rows_left, 0), axis=1)
    n_used = jnp.sum(padded, keepdims=True) // tm
    i32 = lambda a: a.astype(jnp.int32)
    return i32(pad_start), i32(blk_eid), i32(blk_valid), i32(n_used)


def kernel(x, c, w_mod, b_mod, g_norm1, w_in, conv_w, conv_b, conv_ln_g, conv_ln_b, w_out, g_norm2, w_router_group, b_router_group, w_router_expert, b_router_expert, w_exp_gate, w_exp_up, w_exp_down, g_final):
    b, s, d = x.shape
    depth = w_mod.shape[0]
    t = b * s
    assert d == SUBLANES * LANES
    assert s % (ATTN_BLOCK * max(DILATIONS)) == 0 and s % SEQ_TILE == 0
    assert s % COMBINE_TILE == 0
    n_blocks = -(-2 * t // MOE_BLOCK) + N_EXPERTS
    n_rows = n_blocks * MOE_BLOCK

    mod_all = _modulation(c, w_mod, b_mod).reshape(depth, b, N_MOD, d)
    bias = _attn_bias_table()
    pad = ROUTE_LANES - N_GROUPS - N_EXPERTS
    w_router = jnp.concatenate(
        [w_router_group, w_router_expert, jnp.zeros((depth, d, pad), F32)], axis=-1)
    b_router = jnp.concatenate(
        [b_router_group, b_router_expert, jnp.zeros((depth, pad), F32)], axis=-1)
    b_router = b_router.reshape(depth, 1, ROUTE_LANES)
    wr_hi = w_router.astype(BF16)
    wr_lo = (w_router - wr_hi.astype(F32)).astype(BF16)
    w_in_bf = w_in.astype(BF16)
    w_out_bf = w_out.astype(BF16)

    pending = None
    for l in range(depth):
        mod = mod_all[l]
        if pending is None:
            q, k, v, u = _input_projection(x, mod, g_norm1[l], w_in_bf, l)
        else:
            x, q, k, v, u = _combine_input_projection(x, *pending, mod, g_norm1[l], w_in_bf, l)
        attn = _dilated_attention(q, k, v, bias)
        conv = _conformer_conv(u, conv_w[l], conv_b[l], conv_ln_g[l], conv_ln_b[l])
        x, h2, route, sel, seen = _output_projection(attn, conv, x, mod, g_norm2[l], w_out_bf,
                                                     wr_hi, wr_lo, b_router, l)
        route = route.reshape(t, ROUTE_LANES)
        counts = seen[0, N_GROUPS:N_GROUPS + N_EXPERTS].astype(jnp.int32)
        starts, blk_eid, blk_valid, n_used = _plan_blocks(counts, n_blocks)
        eid = sel[REC_LANE0:REC_LANE1 + 1].astype(jnp.int32) - N_GROUPS
        ranks = sel[REC_RANK0:REC_RANK1 + 1].astype(jnp.int32)
        onehot = eid[None] == jnp.arange(N_EXPERTS, dtype=jnp.int32)[:, None, None]
        dest = (jnp.sum(jnp.where(onehot, starts[:, None, None], 0), axis=0) + ranks).reshape(-1)
        xb = _dispatch(h2, dest, n_rows)
        yb = _experts(xb, blk_eid, blk_valid, n_used, w_exp_gate, w_exp_up, w_exp_down, l)
        pending = (route, mod, yb, dest)
    route, mod, yb, dest = pending
    return _combine_final(x.reshape(t, d), route, mod, g_final, yb, dest, s).reshape(b, s, d)
```
